```python
import jax, jax.numpy as jnp
from jax import lax
import numpy as np

D_MODEL = 4096
BATCH = 2
SEQ = 8192
DEPTH = 1

PLE_DIM = 256
MIX_WIDTH = D_MODEL
ATTN_HEADS = 16
ATTN_HEAD_DIM = 128
ATTN_WIDTH = ATTN_HEADS * ATTN_HEAD_DIM
DILATION_PATTERNS = ((128, 1), (512, 4), (2048, 16))
ATTN_BLOCK = 128
ROPE_THETA = 10000.0
MLSTM_HEADS = 4
MLSTM_QK_DIM = 256
MLSTM_V_DIM = 512
MLSTM_QK_WIDTH = MLSTM_HEADS * MLSTM_QK_DIM
MLSTM_V_WIDTH = MLSTM_HEADS * MLSTM_V_DIM
MLSTM_CHUNK = 64
CONV_WIDTH = 4
N_EXPERTS = 32
TOP_K = 4
EXPERT_FF = 1536
SWIGLU_LIMIT = 7.0
SWIGLU_ALPHA = 1.702
EXPERT_BLOCK = 128
NORM_EPS = 1e-6
IN_SPLITS = (ATTN_WIDTH, ATTN_WIDTH, ATTN_WIDTH, MLSTM_QK_WIDTH, MLSTM_QK_WIDTH,
             MLSTM_V_WIDTH, MLSTM_V_WIDTH, MLSTM_HEADS, MLSTM_HEADS)
IN_WIDTH = sum(IN_SPLITS)
IN_OFFSETS = tuple(int(o) for o in np.cumsum(IN_SPLITS)[:-1])

kernel_name = 'hybrid_dilated_attn_mlstm_moe_ple'


def rms_norm(x, g):
    xf = x.astype(jnp.float32)
    y = xf * lax.rsqrt(jnp.mean(xf * xf, axis=-1, keepdims=True) + NORM_EPS)
    return (y * g.astype(jnp.float32)).astype(x.dtype)


def apply_rope(t, positions):
    hd = t.shape[-1]
    half = hd // 2
    inv_freq = jnp.power(ROPE_THETA, -jnp.arange(half, dtype=jnp.float32) / half)
    ang = positions.astype(jnp.float32)[:, None, :, None] * inv_freq
    cos, sin = jnp.cos(ang), jnp.sin(ang)
    t1 = t[..., :half].astype(jnp.float32)
    t2 = t[..., half:].astype(jnp.float32)
    return jnp.concatenate([t1 * cos - t2 * sin, t2 * cos + t1 * sin], axis=-1).astype(t.dtype)


def banded_causal_attention(q, k, v, reach):
    *lead, L, hd = q.shape
    nb = -(-L // ATTN_BLOCK)
    Lp = nb * ATTN_BLOCK
    pad = [(0, 0)] * len(lead) + [(0, Lp - L), (0, 0)]
    q, k, v = (jnp.pad(t, pad).reshape(*lead, nb, ATTN_BLOCK, hd) for t in (q, k, v))

    def with_prev(t):
        prev = jnp.concatenate([jnp.zeros_like(t[..., :1, :, :]), t[..., :-1, :, :]], axis=-3)
        return jnp.concatenate([prev, t], axis=-2)

    kk, vv = with_prev(k), with_prev(v)
    s = jnp.einsum('...nqd,...nkd->...nqk', q, kk).astype(jnp.float32) * (hd ** -0.5)
    qi = jnp.arange(ATTN_BLOCK)[:, None]
    kj = jnp.arange(2 * ATTN_BLOCK)[None, :]
    dist = ATTN_BLOCK + qi - kj
    band = (dist >= 0) & (dist <= reach)
    kpos = jnp.arange(nb)[:, None] * ATTN_BLOCK + kj - ATTN_BLOCK
    mask = band[None] & (kpos >= 0)[:, None, :]
    s = jnp.where(mask, s, -jnp.inf)
    m = jnp.max(s, axis=-1, keepdims=True)
    pr = jnp.exp(s - m)
    l = jnp.sum(pr, axis=-1, keepdims=True)
    o = jnp.einsum('...nqk,...nkd->...nqd', pr.astype(vv.dtype), vv).astype(jnp.float32) / l
    lse = (m + jnp.log(l))[..., 0]
    o = o.reshape(*lead, Lp, hd)[..., :L, :]
    lse = lse.reshape(*lead, Lp)[..., :L]
    return o, lse


def dilated_window_attention(q, k, v):
    B, H, S, hd = q.shape
    outs, lses = [], []
    for window, dil in DILATION_PATTERNS:
        L = S // dil
        strided = lambda t: t.reshape(B, H, L, dil, hd).swapaxes(2, 3)
        o, lse = banded_causal_attention(strided(q), strided(k), strided(v), window // dil)
        outs.append(o.swapaxes(2, 3).reshape(B, H, S, hd))
        lses.append(lse.swapaxes(2, 3).reshape(B, H, S))
    wts = jax.nn.softmax(jnp.stack(lses), axis=0)
    return jnp.einsum('pbhs,pbhsd->bhsd', wts, jnp.stack(outs))


def causal_depthwise_conv(x, w, b):
    C = x.shape[-1]
    y = lax.conv_general_dilated(x, w[:, None, :].astype(x.dtype), window_strides=(1,),
                                 padding=[(CONV_WIDTH - 1, 0)],
                                 dimension_numbers=('NWC', 'WIO', 'NWC'),
                                 feature_group_count=C)
    return y + b.astype(x.dtype)


def mlstm_chunkwise(q, k, v, i_pre, f_pre):
    B, NH, S, dqk = q.shape
    dv = v.shape[-1]
    nc = S // MLSTM_CHUNK
    f32 = jnp.float32

    def chunks(t):
        return jnp.moveaxis(t.reshape(B, NH, nc, MLSTM_CHUNK, *t.shape[3:]), 2, 0)

    xs = (chunks(q.astype(f32) * (dqk ** -0.5)), chunks(k.astype(f32)), chunks(v.astype(f32)),
          chunks(i_pre.astype(f32)), chunks(jax.nn.log_sigmoid(f_pre.astype(f32))))
    causal = jnp.tril(jnp.ones((MLSTM_CHUNK, MLSTM_CHUNK), dtype=bool))

    def step(carry, inp):
        C, n, m = carry
        qq, kk, vv, ii, lf = inp
        b = jnp.cumsum(lf, axis=-1)
        g = b[..., -1]
        log_d = jnp.where(causal, b[..., :, None] - b[..., None, :] + ii[..., None, :], -jnp.inf)
        inter = b + m[..., None]
        m_t = jnp.maximum(inter, jnp.max(log_d, axis=-1))
        w_inter = jnp.exp(inter - m_t)
        s = jnp.einsum('bhtd,bhsd->bhts', qq, kk) * jnp.exp(log_d - m_t[..., None])
        num = w_inter[..., None] * jnp.einsum('bhtd,bhdv->bhtv', qq, C) + jnp.einsum('bhts,bhsv->bhtv', s, vv)
        den = w_inter * jnp.einsum('bhtd,bhd->bht', qq, n) + jnp.sum(s, axis=-1)
        h = num / jnp.maximum(jnp.abs(den), jnp.exp(-m_t))[..., None]
        log_w = g[..., None] - b + ii
        m_new = jnp.maximum(g + m, jnp.max(log_w, axis=-1))
        decay = jnp.exp(g + m - m_new)
        wk = kk * jnp.exp(log_w - m_new[..., None])[..., None]
        C = decay[..., None, None] * C + jnp.einsum('bhsd,bhsv->bhdv', wk, vv)
        n = decay[..., None] * n + jnp.sum(wk, axis=-2)
        return (C, n, m_new), h

    init = (jnp.zeros((B, NH, dqk, dv), f32), jnp.zeros((B, NH, dqk), f32), jnp.zeros((B, NH), f32))
    _, h = lax.scan(step, init, xs)
    return jnp.moveaxis(h, 0, 2).reshape(B, NH, S, dv)


def routed_experts(xf, w_router, b_router, w_gate_up, b_gate_up, w_down, b_down):
    T, D = xf.shape
    logits = (xf @ w_router).astype(jnp.float32) + b_router.astype(jnp.float32)
    top_logits, top_e = lax.top_k(logits, TOP_K)
    gates = jax.nn.softmax(top_logits, axis=-1)
    TK = T * TOP_K
    flat_e = top_e.reshape(TK)
    flat_tok = jnp.arange(TK, dtype=jnp.int32) // TOP_K
    order = jnp.argsort(flat_e)
    sorted_e = flat_e[order]
    counts = jnp.bincount(flat_e, length=N_EXPERTS)
    padded = (counts + EXPERT_BLOCK - 1) // EXPERT_BLOCK * EXPERT_BLOCK
    start = jnp.cumsum(counts) - counts
    pend = jnp.cumsum(padded)
    pstart = pend - padded
    dest = pstart[sorted_e] + jnp.arange(TK, dtype=jnp.int32) - start[sorted_e]
    n_rows = -(-(TK + N_EXPERTS * (EXPERT_BLOCK - 1)) // EXPERT_BLOCK) * EXPERT_BLOCK
    n_blk = n_rows // EXPERT_BLOCK
    row_tok = jnp.full((n_rows,), T, jnp.int32).at[dest].set(flat_tok[order])
    row_gate = jnp.zeros((n_rows,), jnp.float32).at[dest].set(gates.reshape(TK)[order])
    blk_e = jnp.minimum(jnp.searchsorted(pend, jnp.arange(n_blk) * EXPERT_BLOCK, side='right'),
                        N_EXPERTS - 1).astype(jnp.int32)
    x_ext = jnp.concatenate([xf, jnp.zeros((1, D), xf.dtype)], axis=0)

    def body(acc, blk):
        tok, gate, e = blk
        xb = x_ext[tok]
        gu = xb @ w_gate_up[e] + b_gate_up[e]
        glu_in, up = gu[:, :EXPERT_FF], gu[:, EXPERT_FF:]
        glu_in = jnp.minimum(glu_in, SWIGLU_LIMIT)
        up = jnp.clip(up, -SWIGLU_LIMIT, SWIGLU_LIMIT)
        act = (up + 1.0) * (glu_in * jax.nn.sigmoid(SWIGLU_ALPHA * glu_in))
        y = act @ w_down[e] + b_down[e]
        acc = acc.at[tok].add((y.astype(jnp.float32) * gate[:, None]).astype(acc.dtype))
        return acc, None

    acc, _ = lax.scan(body, jnp.zeros((T + 1, D), xf.dtype),
                      (row_tok.reshape(n_blk, EXPERT_BLOCK), row_gate.reshape(n_blk, EXPERT_BLOCK), blk_e))
    return acc[:T]


def setup_inputs(seed: int = 0) -> dict:
    key = jax.random.key(seed)
    ks = jax.random.split(key, 24)
    f32 = jnp.float32
    nrm = lambda k, shape, scale: jax.random.normal(k, shape, f32) * scale
    gain = lambda k, shape: 1.0 + 0.01 * jax.random.normal(k, shape, f32)
    x = nrm(ks[0], (BATCH, SEQ, D_MODEL), 1.0)
    p = nrm(ks[1], (DEPTH, BATCH, SEQ, PLE_DIM), 1.0)
    positions = (jax.random.randint(ks[2], (BATCH, 1), 0, 4096, dtype=jnp.int32)
                 + jnp.arange(SEQ, dtype=jnp.int32)[None, :])
    attn_norm_g = gain(ks[3], (DEPTH, D_MODEL))
    w_in = nrm(ks[4], (DEPTH, D_MODEL, IN_WIDTH), D_MODEL ** -0.5)
    conv_w = nrm(ks[5], (DEPTH, CONV_WIDTH, 2 * MLSTM_QK_WIDTH), CONV_WIDTH ** -0.5)
    conv_b = nrm(ks[6], (DEPTH, 2 * MLSTM_QK_WIDTH), 0.01)
    b_igate = nrm(ks[7], (DEPTH, MLSTM_HEADS), 0.1)
    b_fgate = 3.0 + 3.0 * jax.random.uniform(ks[8], (DEPTH, MLSTM_HEADS), f32)
    mh_norm_g = gain(ks[9], (DEPTH, MLSTM_V_WIDTH))
    w_out = nrm(ks[10], (DEPTH, MIX_WIDTH, D_MODEL), MIX_WIDTH ** -0.5)
    ffn_norm_g = gain(ks[11], (DEPTH, D_MODEL))
    w_router = nrm(ks[12], (DEPTH, D_MODEL, N_EXPERTS), D_MODEL ** -0.5)
    b_router = nrm(ks[13], (DEPTH, N_EXPERTS), 0.01)
    w_gate_up = nrm(ks[14], (DEPTH, N_EXPERTS, D_MODEL, 2 * EXPERT_FF), D_MODEL ** -0.5)
    b_gate_up = nrm(ks[15], (DEPTH, N_EXPERTS, 2 * EXPERT_FF), 0.01)
    w_down = nrm(ks[16], (DEPTH, N_EXPERTS, EXPERT_FF, D_MODEL), EXPERT_FF ** -0.5)
    b_down = nrm(ks[17], (DEPTH, N_EXPERTS, D_MODEL), 0.01)
    ple_norm_g = gain(ks[18], (DEPTH, D_MODEL))
    w_ple_gate = nrm(ks[19], (DEPTH, D_MODEL, D_MODEL), D_MODEL ** -0.5)
    w_ple_proj = nrm(ks[20], (DEPTH, PLE_DIM, D_MODEL), PLE_DIM ** -0.5)
    final_norm_g = gain(ks[21], (D_MODEL,))
    return {'x': x, 'p': p, 'positions': positions, 'attn_norm_g': attn_norm_g, 'w_in': w_in,
            'conv_w': conv_w, 'conv_b': conv_b, 'b_igate': b_igate, 'b_fgate': b_fgate,
            'mh_norm_g': mh_norm_g, 'w_out': w_out, 'ffn_norm_g': ffn_norm_g, 'w_router': w_router,
            'b_router': b_router, 'w_gate_up': w_gate_up, 'b_gate_up': b_gate_up, 'w_down': w_down,
            'b_down': b_down, 'ple_norm_g': ple_norm_g, 'w_ple_gate': w_ple_gate,
            'w_ple_proj': w_ple_proj, 'final_norm_g': final_norm_g}


def reference(x, p, positions, attn_norm_g, w_in, conv_w, conv_b, b_igate, b_fgate, mh_norm_g,
              w_out, ffn_norm_g, w_router, b_router, w_gate_up, b_gate_up, w_down, b_down,
              ple_norm_g, w_ple_gate, w_ple_proj, final_norm_g):
    B, S, D = x.shape
    f32 = jnp.float32
    heads = lambda t, nh: t.reshape(B, S, nh, -1).transpose(0, 2, 1, 3)
    h = x
    for i in range(DEPTH):
        a = rms_norm(h, attn_norm_g[i])
        z = a @ w_in[i]
        qa, ka, va, qm, km, vm, om, ig, fg = jnp.split(z, IN_OFFSETS, axis=-1)
        qa = apply_rope(heads(qa, ATTN_HEADS), positions)
        ka = apply_rope(heads(ka, ATTN_HEADS), positions)
        attn = dilated_window_attention(qa, ka, heads(va, ATTN_HEADS))
        attn = attn.transpose(0, 2, 1, 3).reshape(B, S, ATTN_WIDTH).astype(x.dtype)
        qk = jax.nn.silu(causal_depthwise_conv(jnp.concatenate([qm, km], axis=-1), conv_w[i], conv_b[i]))
        qm, km = jnp.split(qk, 2, axis=-1)
        i_pre = (ig.astype(f32) + b_igate[i].astype(f32)).transpose(0, 2, 1)
        f_pre = (fg.astype(f32) + b_fgate[i].astype(f32)).transpose(0, 2, 1)
        hm = mlstm_chunkwise(heads(qm, MLSTM_HEADS), heads(km, MLSTM_HEADS),
                             heads(vm, MLSTM_HEADS), i_pre, f_pre)
        hm = hm * lax.rsqrt(jnp.mean(hm * hm, axis=-1, keepdims=True) + NORM_EPS)
        hm = hm.transpose(0, 2, 1, 3).reshape(B, S, MLSTM_V_WIDTH) * mh_norm_g[i].astype(f32)
        mlstm = (jax.nn.sigmoid(om.astype(f32)) * hm).astype(x.dtype)
        h = h + jnp.concatenate([attn, mlstm], axis=-1) @ w_out[i]
        f = rms_norm(h, ffn_norm_g[i])
        h = h + routed_experts(f.reshape(B * S, D), w_router[i], b_router[i], w_gate_up[i],
                               b_gate_up[i], w_down[i], b_down[i]).reshape(B, S, D)
        gate = jax.nn.sigmoid(rms_norm(h, ple_norm_g[i]) @ w_ple_gate[i])
        h = h + gate * (p[i] @ w_ple_proj[i])
    return rms_norm(h, final_norm_g)
```

```python
import functools

import jax
import jax.numpy as jnp
from jax import lax
from jax.experimental import pallas as pl
from jax.experimental.pallas import tpu as pltpu

F32 = jnp.float32
BF16 = jnp.bfloat16
U32 = jnp.uint32
I32 = jnp.int32

V7X_VMEM_BYTES = 64 * 1024 * 1024
V7X_LANES = 128
VMEM_LIMIT = V7X_VMEM_BYTES - 8 * 1024 * 1024

ATTN_HEADS = 16
HEAD_DIM = 128
ATTN_WIDTH = ATTN_HEADS * HEAD_DIM
ATTN_BLOCK = 128
DILATIONS = (1, 4, 16)
ATTN_SUPER = ATTN_BLOCK * max(DILATIONS)
ROPE_THETA = 10000.0
MLSTM_HEADS = 4
MLSTM_QK_DIM = 256
MLSTM_V_DIM = 512
MLSTM_QK_WIDTH = MLSTM_HEADS * MLSTM_QK_DIM
MLSTM_V_WIDTH = MLSTM_HEADS * MLSTM_V_DIM
MLSTM_CHUNK = 256
CONV_WIDTH = 4
CONV_PAD = 8
N_EXPERTS = 32
TOP_K = 4
EXPERT_FF = 1536
SWIGLU_LIMIT = 7.0
SWIGLU_ALPHA = 1.702
NORM_EPS = 1e-6
NEG_BIG = -1e30

MM_TM = 1024
MM_TN = 1024
NORM_TM = 256
EXPERT_TM = 512
EXPERT_TF = 512
EXPERT_TN = 1024
COMBINE_TQ = 128


def _cparams(sem, **kw):
    return pltpu.CompilerParams(dimension_semantics=sem, vmem_limit_bytes=VMEM_LIMIT, **kw)


def _rmsnorm_body(x, g):
    r = lax.rsqrt(jnp.mean(x * x, axis=-1, keepdims=True) + NORM_EPS)
    return x * r * g


def _rmsnorm_kernel(x_ref, g_ref, o_ref):
    o_ref[...] = _rmsnorm_body(x_ref[...], g_ref[...]).astype(o_ref.dtype)


def rmsnorm(x, g, out_dtype):
    t, d = x.shape
    return pl.pallas_call(
        _rmsnorm_kernel,
        grid=(t // NORM_TM,),
        in_specs=[pl.BlockSpec((NORM_TM, d), lambda i: (i, 0)),
                  pl.BlockSpec((1, d), lambda i: (0, 0))],
        out_specs=pl.BlockSpec((NORM_TM, d), lambda i: (i, 0)),
        out_shape=jax.ShapeDtypeStruct((t, d), out_dtype),
        compiler_params=_cparams(("parallel",)),
        name="rmsnorm",
    )(x, g.reshape(1, d))


def _rope_table_kernel(pos_ref, freq_ref, sign_ref, cos_ref, sin_ref):
    ang = pos_ref[...].astype(F32) * freq_ref[...]
    cos_ref[...] = jnp.cos(ang)
    sin_ref[...] = jnp.sin(ang) * sign_ref[...]


def rope_tables(positions_col):
    t = positions_col.shape[0]
    half = HEAD_DIM // 2
    inv_freq = jnp.power(ROPE_THETA, -jnp.arange(half, dtype=F32) / half)
    freq = jnp.concatenate([inv_freq, inv_freq]).reshape(1, HEAD_DIM)
    sign = jnp.concatenate([-jnp.ones((half,), F32), jnp.ones((half,), F32)]).reshape(1, HEAD_DIM)
    tm = 512
    return pl.pallas_call(
        _rope_table_kernel,
        grid=(t // tm,),
        in_specs=[pl.BlockSpec((tm, 1), lambda i: (i, 0)),
                  pl.BlockSpec((1, HEAD_DIM), lambda i: (0, 0)),
                  pl.BlockSpec((1, HEAD_DIM), lambda i: (0, 0))],
        out_specs=[pl.BlockSpec((tm, HEAD_DIM), lambda i: (i, 0))] * 2,
        out_shape=[jax.ShapeDtypeStruct((t, HEAD_DIM), F32)] * 2,
        compiler_params=_cparams(("parallel",)),
        name="rope_tables",
    )(positions_col, freq, sign)


def _inproj_kernel(a_ref, w_ref, wg_ref, cos_ref, sin_ref, z_ref, gate_ref, *, n_q_tiles, n_k_tiles,
                   q_scale):
    j = pl.program_id(1)
    acc = jnp.dot(a_ref[...], w_ref[...], preferred_element_type=F32)

    @pl.when(j == 0)
    def _():
        gate_ref[...] = jnp.dot(a_ref[...], wg_ref[...], preferred_element_type=F32)

    def rope(scale):
        cos = cos_ref[...]
        sin = sin_ref[...]
        for c in range(acc.shape[1] // HEAD_DIM):
            t = acc[:, c * HEAD_DIM:(c + 1) * HEAD_DIM]
            r = pltpu.roll(t, HEAD_DIM // 2, axis=1)
            z_ref[:, c * HEAD_DIM:(c + 1) * HEAD_DIM] = ((t * cos + r * sin) * scale).astype(z_ref.dtype)

    @pl.when(j < n_q_tiles)
    def _():
        rope(q_scale)

    @pl.when((j >= n_q_tiles) & (j < n_q_tiles + n_k_tiles))
    def _():
        rope(1.0)

    @pl.when(j >= n_q_tiles + n_k_tiles)
    def _():
        z_ref[...] = acc.astype(z_ref.dtype)


def in_projection(a, w, w_gates, cos, sin, *, tm=MM_TM, tn=MM_TN):
    t, k = a.shape
    n = w.shape[1]
    tm, tn = min(tm, t), min(tn, n)
    kern = functools.partial(_inproj_kernel, n_q_tiles=ATTN_WIDTH // tn, n_k_tiles=ATTN_WIDTH // tn,
                             q_scale=HEAD_DIM ** -0.5)
    return pl.pallas_call(
        kern,
        grid=(t // tm, n // tn),
        in_specs=[pl.BlockSpec((tm, k), lambda i, j: (i, 0)),
                  pl.BlockSpec((k, tn), lambda i, j: (0, j)),
                  pl.BlockSpec((k, V7X_LANES), lambda i, j: (0, 0)),
                  pl.BlockSpec((tm, HEAD_DIM), lambda i, j: (i, 0)),
                  pl.BlockSpec((tm, HEAD_DIM), lambda i, j: (i, 0))],
        out_specs=[pl.BlockSpec((tm, tn), lambda i, j: (i, j)),
                   pl.BlockSpec((tm, V7X_LANES), lambda i, j: (i, 0))],
        out_shape=[jax.ShapeDtypeStruct((t, n), BF16),
                   jax.ShapeDtypeStruct((t, V7X_LANES), F32)],
        compiler_params=_cparams(("parallel", "arbitrary")),
        name="in_projection",
    )(a, w, w_gates, cos, sin)


def _band_block(qb, kb, vb, bias):
    s = lax.dot_general(qb, kb, (((1,), (1,)), ((), ())), preferred_element_type=F32) + bias
    m = jnp.max(s, axis=-1, keepdims=True)
    p = jnp.exp(s - m)
    l = jnp.sum(p, axis=-1, keepdims=True)
    o = jnp.dot(p.astype(BF16), vb, preferred_element_type=F32) / l
    return o, m + jnp.log(l)


def _attn_kernel(q_ref, kp_ref, kc_ref, vp_ref, vc_ref, o_ref, qf, kf, vf, ob, lb):
    sb = ATTN_SUPER
    blk = ATTN_BLOCK
    n = pl.program_id(2)
    qf[...] = q_ref[...].astype(F32)
    kf[0:sb, :] = kp_ref[...].astype(F32)
    kf[sb:2 * sb, :] = kc_ref[...].astype(F32)
    vf[0:sb, :] = vp_ref[...].astype(F32)
    vf[sb:2 * sb, :] = vc_ref[...].astype(F32)

    qi = lax.broadcasted_iota(I32, (blk, 2 * blk), 0)
    kj = lax.broadcasted_iota(I32, (blk, 2 * blk), 1)
    dist = blk + qi - kj
    band = (dist >= 0) & (dist <= blk)
    bias_in = jnp.where(band, 0.0, NEG_BIG).astype(F32)
    prev_ok = jnp.where(n > 0, 0.0, NEG_BIG).astype(F32)
    bias_edge = bias_in + jnp.where(kj < blk, prev_ok, 0.0)

    for p, d in enumerate(DILATIONS):
        per_class = sb // (d * blk)
        for r in range(d):
            for nb in range(per_class):
                q0 = r + d * blk * nb
                k0 = sb + q0 - d * blk
                if d == 1:
                    qb = qf[pl.ds(q0, blk), :]
                    kb = kf[pl.ds(k0, 2 * blk), :]
                    vb = vf[pl.ds(k0, 2 * blk), :]
                else:
                    qb = qf[pl.ds(q0, blk, stride=d), :]
                    kb = kf[pl.ds(k0, 2 * blk, stride=d), :]
                    vb = vf[pl.ds(k0, 2 * blk, stride=d), :]
                bias = bias_edge if nb == 0 else bias_in
                o, lse = _band_block(qb.astype(BF16), kb.astype(BF16), vb.astype(BF16), bias)
                lse_b = jnp.broadcast_to(lse, (blk, HEAD_DIM))
                if d == 1:
                    ob[p, pl.ds(q0, blk), :] = o
                    lb[p, pl.ds(q0, blk), :] = lse_b
                else:
                    ob[p, pl.ds(q0, blk, stride=d), :] = o
                    lb[p, pl.ds(q0, blk, stride=d), :] = lse_b

    l0, l1, l2 = lb[0], lb[1], lb[2]
    mx = jnp.maximum(jnp.maximum(l0, l1), l2)
    w0, w1, w2 = jnp.exp(l0 - mx), jnp.exp(l1 - mx), jnp.exp(l2 - mx)
    out = (w0 * ob[0] + w1 * ob[1] + w2 * ob[2]) / (w0 + w1 + w2)
    o_ref[...] = out.astype(o_ref.dtype)


def dilated_attention(z, batch, seq):
    t = z.shape[0]
    sb = ATTN_SUPER
    nsb = seq // sb
    h = ATTN_HEADS
    cur = lambda off: (lambda b, hh, n: (b * nsb + n, off + hh))
    prev = lambda off: (lambda b, hh, n: (b * nsb + jnp.maximum(n - 1, 0), off + hh))
    spec = lambda im: pl.BlockSpec((sb, HEAD_DIM), im)
    return pl.pallas_call(
        _attn_kernel,
        grid=(batch, h, nsb),
        in_specs=[spec(cur(0)), spec(prev(h)), spec(cur(h)), spec(prev(2 * h)), spec(cur(2 * h))],
        out_specs=spec(cur(0)),
        out_shape=jax.ShapeDtypeStruct((t, ATTN_WIDTH), BF16),
        scratch_shapes=[pltpu.VMEM((sb, HEAD_DIM), F32),
                        pltpu.VMEM((2 * sb, HEAD_DIM), F32),
                        pltpu.VMEM((2 * sb, HEAD_DIM), F32),
                        pltpu.VMEM((len(DILATIONS), sb, HEAD_DIM), F32),
                        pltpu.VMEM((len(DILATIONS), sb, HEAD_DIM), F32)],
        compiler_params=_cparams(("parallel", "parallel", "arbitrary")),
        name="dilated_attention",
    )(z, z, z, z, z)


def _log_sigmoid(x):
    return jnp.minimum(x, 0.0) - jnp.log(1.0 + jnp.exp(-jnp.abs(x)))


def _mlstm_kernel(q_ref, k_ref, v_ref, om_ref, g_ref, gb_ref, cw_ref, cb_ref, mhg_ref, o_ref,
                  xbuf, c_st, n_st, m_st):
    L = MLSTM_CHUNK
    dqk, dv, nh = MLSTM_QK_DIM, MLSTM_V_DIM, MLSTM_HEADS
    qw = nh * dqk
    c = pl.program_id(1)

    @pl.when(c == 0)
    def _():
        xbuf[0:CONV_PAD, :] = jnp.zeros((CONV_PAD, 2 * qw), F32)
        c_st[...] = jnp.zeros_like(c_st)
        n_st[...] = jnp.zeros_like(n_st)
        m_st[...] = jnp.zeros_like(m_st)

    @pl.when(c > 0)
    def _():
        xbuf[0:CONV_PAD, :] = xbuf[L:L + CONV_PAD, :]

    xbuf[CONV_PAD:CONV_PAD + L, 0:qw] = q_ref[...].astype(F32)
    xbuf[CONV_PAD:CONV_PAD + L, qw:2 * qw] = k_ref[...].astype(F32)
    y = cb_ref[...]
    for j in range(CONV_WIDTH):
        y = y + cw_ref[j:j + 1, :] * xbuf[pl.ds(CONV_PAD - CONV_WIDTH + 1 + j, L), :]
    qk = y * jax.nn.sigmoid(y)

    pre = g_ref[...] + gb_ref[...]
    lf = _log_sigmoid(pre)
    row = lax.broadcasted_iota(I32, (L, L), 0)
    col = lax.broadcasted_iota(I32, (L, L), 1)
    causal = col <= row
    tril = jnp.where(causal, 1.0, 0.0).astype(F32)
    bcum = jnp.dot(tril, lf, preferred_element_type=F32, precision=lax.Precision.HIGHEST)
    pre_t = pre.T
    bcum_t = bcum.T

    for h in range(nh):
        i_c = pre[:, h:h + 1]
        b_c = bcum[:, nh + h:nh + h + 1]
        i_r = pre_t[h:h + 1, :]
        b_r = bcum_t[nh + h:nh + h + 1, :]
        g = bcum[L - 1:L, nh + h:nh + h + 1]
        m = m_st[h, 0:1, 0:1]
        qq = (qk[:, h * dqk:(h + 1) * dqk] * (dqk ** -0.5))
        kk = qk[:, qw + h * dqk:qw + (h + 1) * dqk]
        vv = v_ref[:, h * dv:(h + 1) * dv]
        qb = qq.astype(BF16)

        log_d = jnp.where(causal, b_c - b_r + i_r, NEG_BIG)
        inter = b_c + m
        m_t = jnp.maximum(inter, jnp.max(log_d, axis=-1, keepdims=True))
        w_inter = jnp.exp(inter - m_t)
        s = lax.dot_general(qb, kk.astype(BF16), (((1,), (1,)), ((), ())),
                            preferred_element_type=F32) * jnp.exp(log_d - m_t)
        num = (w_inter * jnp.dot(qb, c_st[h].astype(BF16), preferred_element_type=F32)
               + jnp.dot(s.astype(BF16), vv, preferred_element_type=F32))
        den = (w_inter * jnp.sum(qq * n_st[h], axis=-1, keepdims=True)
               + jnp.sum(s, axis=-1, keepdims=True))
        hh = num / jnp.maximum(jnp.abs(den), jnp.exp(-m_t))

        log_w = g - b_c + i_c
        m_new = jnp.maximum(g + m, jnp.max(log_w, axis=0, keepdims=True))
        decay = jnp.exp(g + m - m_new)
        wk = kk * jnp.exp(log_w - m_new)
        c_st[h] = decay * c_st[h] + jnp.dot(wk.T.astype(BF16), vv, preferred_element_type=F32)
        n_st[h] = decay * n_st[h] + jnp.sum(wk, axis=0, keepdims=True)
        m_st[h] = jnp.broadcast_to(m_new, m_st.shape[1:])

        hn = hh * lax.rsqrt(jnp.mean(hh * hh, axis=-1, keepdims=True) + NORM_EPS)
        hn = hn * mhg_ref[:, h * dv:(h + 1) * dv]
        og = jax.nn.sigmoid(om_ref[:, h * dv:(h + 1) * dv].astype(F32))
        o_ref[:, h * dv:(h + 1) * dv] = (og * hn).astype(o_ref.dtype)


def mlstm(z, gates, gate_bias, conv_w, conv_b, mh_norm_g, batch, seq):
    t = z.shape[0]
    L = MLSTM_CHUNK
    nc = seq // L
    qw, vw = MLSTM_QK_WIDTH, MLSTM_V_WIDTH
    q_off = 3 * ATTN_WIDTH
    rows = lambda blk: (lambda b, c: (b * nc + c, blk))
    const = lambda b, c: (0, 0)
    return pl.pallas_call(
        _mlstm_kernel,
        grid=(batch, nc),
        in_specs=[pl.BlockSpec((L, qw), rows(q_off // qw)),
                  pl.BlockSpec((L, qw), rows(q_off // qw + 1)),
                  pl.BlockSpec((L, vw), rows((q_off + 2 * qw) // vw)),
                  pl.BlockSpec((L, vw), rows((q_off + 2 * qw) // vw + 1)),
                  pl.BlockSpec((L, V7X_LANES), rows(0)),
                  pl.BlockSpec((1, V7X_LANES), const),
                  pl.BlockSpec((CONV_WIDTH, 2 * qw), const),
                  pl.BlockSpec((1, 2 * qw), const),
                  pl.BlockSpec((1, vw), const)],
        out_specs=pl.BlockSpec((L, vw), rows(0)),
        out_shape=jax.ShapeDtypeStruct((t, vw), BF16),
        scratch_shapes=[pltpu.VMEM((L + CONV_PAD, 2 * qw), F32),
                        pltpu.VMEM((MLSTM_HEADS, MLSTM_QK_DIM, MLSTM_V_DIM), F32),
                        pltpu.VMEM((MLSTM_HEADS, 1, MLSTM_QK_DIM), F32),
                        pltpu.VMEM((MLSTM_HEADS, 8, V7X_LANES), F32)],
        compiler_params=_cparams(("parallel", "arbitrary")),
        name="mlstm",
    )(z, z, z, z, gates, gate_bias, conv_w, conv_b.reshape(1, 2 * qw), mh_norm_g.reshape(1, vw))


def _outproj_kernel(a1_ref, a2_ref, w_ref, res_ref, o_ref):
    k1 = a1_ref.shape[1]
    acc = jnp.dot(a1_ref[...], w_ref[0:k1, :], preferred_element_type=F32)
    acc = acc + jnp.dot(a2_ref[...], w_ref[k1:, :], preferred_element_type=F32)
    o_ref[...] = res_ref[...] + acc


def out_projection(a1, a2, w, res, *, tm=MM_TM, tn=MM_TN):
    t, k1 = a1.shape
    k2 = a2.shape[1]
    n = w.shape[1]
    tm, tn = min(tm, t), min(tn, n)
    return pl.pallas_call(
        _outproj_kernel,
        grid=(t // tm, n // tn),
        in_specs=[pl.BlockSpec((tm, k1), lambda i, j: (i, 0)),
                  pl.BlockSpec((tm, k2), lambda i, j: (i, 0)),
                  pl.BlockSpec((k1 + k2, tn), lambda i, j: (0, j)),
                  pl.BlockSpec((tm, tn), lambda i, j: (i, j))],
        out_specs=pl.BlockSpec((tm, tn), lambda i, j: (i, j)),
        out_shape=jax.ShapeDtypeStruct((t, n), F32),
        compiler_params=_cparams(("parallel", "arbitrary")),
        name="out_projection",
    )(a1, a2, w, res)


def _pack_bf16_pairs(f):
    half = f.shape[1] // 2
    fb = f.astype(BF16).astype(F32)
    lo = pltpu.bitcast(fb[:, :half], U32) >> 16
    hi = pltpu.bitcast(fb[:, half:], U32) & jnp.uint32(0xFFFF0000)
    return lo | hi


def _unpack_bf16_pairs(w):
    lo = pltpu.bitcast(w << 16, F32).astype(BF16)
    hi = pltpu.bitcast(w & jnp.uint32(0xFFFF0000), F32).astype(BF16)
    return jnp.concatenate([lo, hi], axis=-1)


def _router_kernel(h_ref, g_ref, wr_ref, br_ref, fpk_ref, e_ref, gate_ref, *, n_tiles):
    i = pl.program_id(0)
    f = _rmsnorm_body(h_ref[...], g_ref[...])
    fpk_ref[...] = jnp.where(i < n_tiles, _pack_bf16_pairs(f), jnp.uint32(0))
    logits = jnp.dot(f, wr_ref[...], preferred_element_type=F32,
                     precision=lax.Precision.HIGHEST) + br_ref[...]
    lane = lax.broadcasted_iota(I32, logits.shape, 1)
    cur = logits
    vals, idxs = [], []
    for _ in range(TOP_K):
        mx = jnp.max(cur, axis=-1, keepdims=True)
        idx = jnp.min(jnp.where(cur == mx, lane, V7X_LANES), axis=-1, keepdims=True)
        vals.append(mx)
        idxs.append(idx)
        cur = jnp.where(lane == idx, -jnp.inf, cur)
    exps = [jnp.exp(v - vals[0]) for v in vals]
    tot = exps[0] + exps[1] + exps[2] + exps[3]
    e_out = jnp.zeros(logits.shape, I32)
    g_out = jnp.zeros(logits.shape, F32)
    for k in range(TOP_K):
        e_out = jnp.where(lane == k, idxs[k], e_out)
        g_out = jnp.where(lane == k, exps[k] / tot, g_out)
    e_ref[...] = e_out
    gate_ref[...] = g_out


def ffn_norm_router(h, g, w_router, b_router, *, tm=NORM_TM):
    t, d = h.shape
    nt = t // tm
    wr = jnp.zeros((d, V7X_LANES), F32).at[:, :N_EXPERTS].set(w_router)
    br = jnp.full((1, V7X_LANES), NEG_BIG, F32).at[0, :N_EXPERTS].set(b_router)
    rows = lambda i: (jnp.minimum(i, nt - 1), 0)
    return pl.pallas_call(
        functools.partial(_router_kernel, n_tiles=nt),
        grid=(nt + 1,),
        in_specs=[pl.BlockSpec((tm, d), rows),
                  pl.BlockSpec((1, d), lambda i: (0, 0)),
                  pl.BlockSpec((d, V7X_LANES), lambda i: (0, 0)),
                  pl.BlockSpec((1, V7X_LANES), lambda i: (0, 0))],
        out_specs=[pl.BlockSpec((tm, d // 2), lambda i: (i, 0)),
                   pl.BlockSpec((tm, V7X_LANES), rows),
                   pl.BlockSpec((tm, V7X_LANES), rows)],
        out_shape=[jax.ShapeDtypeStruct((t + tm, d // 2), U32),
                   jax.ShapeDtypeStruct((t, V7X_LANES), I32),
                   jax.ShapeDtypeStruct((t, V7X_LANES), F32)],
        compiler_params=_cparams(("arbitrary",)),
        name="ffn_norm_router",
    )(h, g.reshape(1, d), wr, br)


def routing_tables(top_e, gates, n_tokens):
    tk = n_tokens * TOP_K
    tm = EXPERT_TM
    n_blk = -(-(tk + N_EXPERTS * (tm - 1)) // tm)
    n_rows = n_blk * tm
    flat_e = top_e.reshape(tk)
    order = jnp.argsort(flat_e)
    sorted_e = flat_e[order]
    counts = jnp.bincount(flat_e, length=N_EXPERTS)
    padded = (counts + tm - 1) // tm * tm
    start = jnp.cumsum(counts) - counts
    pend = jnp.cumsum(padded)
    pstart = pend - padded
    dest = (pstart[sorted_e] + jnp.arange(tk, dtype=I32) - start[sorted_e]).astype(I32)
    pos = jnp.zeros((tk,), I32).at[order].set(dest)
    row_tok = jnp.full((n_rows,), n_tokens, I32).at[dest].set((order // TOP_K).astype(I32))
    row_gate = jnp.zeros((n_rows,), F32).at[dest].set(gates.reshape(tk)[order])
    blk_e = jnp.minimum(jnp.searchsorted(pend, jnp.arange(n_blk) * tm, side='right'),
                        N_EXPERTS - 1).astype(I32)
    n_used = (pend[-1] // tm).astype(I32).reshape(1)
    return pos, row_tok, row_gate.reshape(n_rows, 1), blk_e, n_used


def _gather_rows_kernel(tok_ref, src_hbm, dst_hbm, sem, *, rows):
    i = pl.program_id(0)

    def row_copy(r):
        tok = tok_ref[i * rows + r]
        return pltpu.make_async_copy(src_hbm.at[pl.ds(tok, 1)], dst_hbm.at[pl.ds(i * rows + r, 1)], sem)

    def issue(r, carry):
        row_copy(r).start()
        return carry

    def drain(r, carry):
        row_copy(r).wait()
        return carry

    lax.fori_loop(0, rows, issue, 0)
    lax.fori_loop(0, rows, drain, 0)


def gather_rows(src, row_tok, *, rows=EXPERT_TM):
    n_rows = row_tok.shape[0]
    width = src.shape[1]
    grid_spec = pltpu.PrefetchScalarGridSpec(
        num_scalar_prefetch=1,
        grid=(n_rows // rows,),
        in_specs=[pl.BlockSpec(memory_space=pl.ANY)],
        out_specs=pl.BlockSpec(memory_space=pl.ANY),
        scratch_shapes=[pltpu.SemaphoreType.DMA(())],
    )
    return pl.pallas_call(
        functools.partial(_gather_rows_kernel, rows=rows),
        grid_spec=grid_spec,
        out_shape=jax.ShapeDtypeStruct((n_rows, width), src.dtype),
        compiler_params=_cparams(("arbitrary",), has_side_effects=True),
        name="gather_rows",
    )(row_tok, src)


def _expert_up_kernel(blk_e, nused, xs_ref, wg_ref, wu_ref, bg_ref, bu_ref, act_ref, xb):
    i = pl.program_id(0)
    f = pl.program_id(1)

    @pl.when(i < nused[0])
    def _():
        @pl.when(f == 0)
        def _():
            xb[...] = _unpack_bf16_pairs(xs_ref[...])

        x = xb[...]
        glu = jnp.dot(x, wg_ref[...], preferred_element_type=F32) + bg_ref[...]
        up = jnp.dot(x, wu_ref[...], preferred_element_type=F32) + bu_ref[...]
        glu = jnp.minimum(glu, SWIGLU_LIMIT)
        up = jnp.clip(up, -SWIGLU_LIMIT, SWIGLU_LIMIT)
        act_ref[...] = ((up + 1.0) * (glu * jax.nn.sigmoid(SWIGLU_ALPHA * glu))).astype(act_ref.dtype)

    @pl.when(i >= nused[0])
    def _():
        act_ref[...] = jnp.zeros_like(act_ref)


def expert_up(xs, w_gate_up, b_gate_up, blk_e, n_used, *, tm=EXPERT_TM, tf=EXPERT_TF):
    n_rows, half = xs.shape
    d = 2 * half
    ff = EXPERT_FF
    nf = ff // tf
    n_blk = n_rows // tm

    def blk(i, nu):
        return jnp.minimum(i, nu[0] - 1)

    def ftile(i, f, nu):
        return jnp.where(i < nu[0], f, nf - 1)

    grid_spec = pltpu.PrefetchScalarGridSpec(
        num_scalar_prefetch=2,
        grid=(n_blk, nf),
        in_specs=[pl.BlockSpec((tm, half), lambda i, f, be, nu: (blk(i, nu), 0)),
                  pl.BlockSpec((None, d, tf), lambda i, f, be, nu: (be[blk(i, nu)], 0, ftile(i, f, nu))),
                  pl.BlockSpec((None, d, tf), lambda i, f, be, nu: (be[blk(i, nu)], 0, nf + ftile(i, f, nu))),
                  pl.BlockSpec((None, 1, tf), lambda i, f, be, nu: (be[blk(i, nu)], 0, ftile(i, f, nu))),
                  pl.BlockSpec((None, 1, tf), lambda i, f, be, nu: (be[blk(i, nu)], 0, nf + ftile(i, f, nu)))],
        out_specs=pl.BlockSpec((tm, tf), lambda i, f, be, nu: (i, f)),
        scratch_shapes=[pltpu.VMEM((tm, d), BF16)],
    )
    return pl.pallas_call(
        _expert_up_kernel,
        grid_spec=grid_spec,
        out_shape=jax.ShapeDtypeStruct((n_rows, ff), BF16),
        compiler_params=_cparams(("arbitrary", "arbitrary")),
        name="expert_up",
    )(blk_e, n_used, xs, w_gate_up, w_gate_up, b_gate_up, b_gate_up)


def _expert_down_kernel(blk_e, nused, act_ref, wd_ref, bd_ref, gate_ref, y_ref):
    i = pl.program_id(0)

    @pl.when(i < nused[0])
    def _():
        y = jnp.dot(act_ref[...], wd_ref[...], preferred_element_type=F32) + bd_ref[...]
        y_ref[...] = y * gate_ref[...]

    @pl.when(i >= nused[0])
    def _():
        y_ref[...] = jnp.zeros_like(y_ref)


def expert_down(act, w_down, b_down, row_gate, blk_e, n_used, *, tm=EXPERT_TM, tn=EXPERT_TN):
    n_rows, ff = act.shape
    d = w_down.shape[2]
    tn = min(tn, d)
    nn = d // tn
    n_blk = n_rows // tm

    def blk(i, nu):
        return jnp.minimum(i, nu[0] - 1)

    def ntile(i, j, nu):
        return jnp.where(i < nu[0], j, nn - 1)

    grid_spec = pltpu.PrefetchScalarGridSpec(
        num_scalar_prefetch=2,
        grid=(n_blk, nn),
        in_specs=[pl.BlockSpec((tm, ff), lambda i, j, be, nu: (blk(i, nu), 0)),
                  pl.BlockSpec((None, ff, tn), lambda i, j, be, nu: (be[blk(i, nu)], 0, ntile(i, j, nu))),
                  pl.BlockSpec((None, 1, tn), lambda i, j, be, nu: (be[blk(i, nu)], 0, ntile(i, j, nu))),
                  pl.BlockSpec((tm, 1), lambda i, j, be, nu: (blk(i, nu), 0))],
        out_specs=pl.BlockSpec((tm, tn), lambda i, j, be, nu: (i, j)),
    )
    return pl.pallas_call(
        _expert_down_kernel,
        grid_spec=grid_spec,
        out_shape=jax.ShapeDtypeStruct((n_rows, d), F32),
        compiler_params=_cparams(("arbitrary", "arbitrary")),
        name="expert_down",
    )(blk_e, n_used, act, w_down, b_down, row_gate)


def _combine_kernel(pos_ref, y_hbm, h_ref, o_ref, buf, sem, *, tq):
    i = pl.program_id(0)

    def row_copy(t, k):
        row = pos_ref[(i * tq + t) * TOP_K + k]
        return pltpu.make_async_copy(y_hbm.at[pl.ds(row, 1)], buf.at[k, pl.ds(t, 1)], sem)

    def issue(t, carry):
        for k in range(TOP_K):
            row_copy(t, k).start()
        return carry

    def drain(t, carry):
        for k in range(TOP_K):
            row_copy(t, k).wait()
        return carry

    lax.fori_loop(0, tq, issue, 0)
    lax.fori_loop(0, tq, drain, 0)
    o_ref[...] = h_ref[...] + ((buf[0] + buf[1]) + (buf[2] + buf[3]))


def combine(y, pos, h, *, tq=COMBINE_TQ):
    t, d = h.shape
    grid_spec = pltpu.PrefetchScalarGridSpec(
        num_scalar_prefetch=1,
        grid=(t // tq,),
        in_specs=[pl.BlockSpec(memory_space=pl.ANY),
                  pl.BlockSpec((tq, d), lambda i, p: (i, 0))],
        out_specs=pl.BlockSpec((tq, d), lambda i, p: (i, 0)),
        scratch_shapes=[pltpu.VMEM((TOP_K, tq, d), F32), pltpu.SemaphoreType.DMA(())],
    )
    return pl.pallas_call(
        functools.partial(_combine_kernel, tq=tq),
        grid_spec=grid_spec,
        out_shape=jax.ShapeDtypeStruct((t, d), F32),
        compiler_params=_cparams(("arbitrary",)),
        name="combine",
    )(pos, y, h)


def _ple_kernel(a_ref, w_ref, p_ref, wp_ref, h_ref, o_ref):
    gate = jax.nn.sigmoid(jnp.dot(a_ref[...], w_ref[...], preferred_element_type=F32))
    emb = jnp.dot(p_ref[...], wp_ref[...], preferred_element_type=F32)
    o_ref[...] = h_ref[...] + gate * emb


def ple(a, w_gate, p, w_proj, h, *, tm=MM_TM // 2, tn=MM_TN):
    t, k = a.shape
    n = w_gate.shape[1]
    pk = p.shape[1]
    tm, tn = min(tm, t), min(tn, n)
    return pl.pallas_call(
        _ple_kernel,
        grid=(t // tm, n // tn),
        in_specs=[pl.BlockSpec((tm, k), lambda i, j: (i, 0)),
                  pl.BlockSpec((k, tn), lambda i, j: (0, j)),
                  pl.BlockSpec((tm, pk), lambda i, j: (i, 0)),
                  pl.BlockSpec((pk, tn), lambda i, j: (0, j)),
                  pl.BlockSpec((tm, tn), lambda i, j: (i, j))],
        out_specs=pl.BlockSpec((tm, tn), lambda i, j: (i, j)),
        out_shape=jax.ShapeDtypeStruct((t, n), F32),
        compiler_params=_cparams(("parallel", "arbitrary")),
        name="ple",
    )(a, w_gate, p, w_proj, h)


def kernel(x, p, positions, attn_norm_g, w_in, conv_w, conv_b, b_igate, b_fgate, mh_norm_g, w_out,
           ffn_norm_g, w_router, b_router, w_gate_up, b_gate_up, w_down, b_down, ple_norm_g, w_ple_gate,
           w_ple_proj, final_norm_g):
    batch, seq, d = x.shape
    depth = p.shape[0]
    t = batch * seq
    n_main = 3 * ATTN_WIDTH + 2 * MLSTM_QK_WIDTH + 2 * MLSTM_V_WIDTH
    nh = MLSTM_HEADS
    h = x.reshape(t, d)
    cos, sin = rope_tables(positions.reshape(t, 1))
    for i in range(depth):
        a = rmsnorm(h, attn_norm_g[i], BF16)
        w_main = w_in[i][:, :n_main].astype(BF16)
        w_gates = jnp.zeros((d, V7X_LANES), BF16).at[:, :2 * nh].set(w_in[i][:, n_main:].astype(BF16))
        z, gates = in_projection(a, w_main, w_gates, cos, sin)
        attn = dilated_attention(z, batch, seq)
        gate_bias = jnp.zeros((1, V7X_LANES), F32).at[0, :nh].set(b_igate[i]).at[0, nh:2 * nh].set(b_fgate[i])
        ml = mlstm(z, gates, gate_bias, conv_w[i], conv_b[i], mh_norm_g[i], batch, seq)
        h = out_projection(attn, ml, w_out[i].astype(BF16), h)
        fpk, top_e, top_g = ffn_norm_router(h, ffn_norm_g[i], w_router[i], b_router[i])
        pos, row_tok, row_gate, blk_e, n_used = routing_tables(top_e[:, :TOP_K], top_g[:, :TOP_K], t)
        xs = gather_rows(fpk, row_tok)
        act = expert_up(xs, w_gate_up[i].astype(BF16), b_gate_up[i].reshape(N_EXPERTS, 1, 2 * EXPERT_FF),
                        blk_e, n_used)
        y = expert_down(act, w_down[i].astype(BF16), b_down[i].reshape(N_EXPERTS, 1, d), row_gate, blk_e, n_used)
        h = combine(y, pos, h)
        a2 = rmsnorm(h, ple_norm_g[i], BF16)
        h = ple(a2, w_ple_gate[i].astype(BF16), p[i].reshape(t, -1).astype(BF16), w_ple_proj[i].astype(BF16), h)
    return rmsnorm(h, final_norm_g, x.dtype).reshape(batch, seq, d)
```

```python
import functools

import jax
import jax.numpy as jnp
from jax import lax
from jax.experimental import pallas as pl
from jax.experimental.pallas import tpu as pltpu

F32 = jnp.float32
BF16 = jnp.bfloat16
U32 = jnp.uint32
I32 = jnp.int32

V7X_VMEM_BYTES = 64 * 1024 * 1024
V7X_LANES = 128
VMEM_LIMIT = V7X_VMEM_BYTES - 8 * 1024 * 1024

ATTN_HEADS = 16
HEAD_DIM = 128
ATTN_WIDTH = ATTN_HEADS * HEAD_DIM
ATTN_BLOCK = 128
DILATIONS = (1, 4, 16)
ATTN_SUPER = ATTN_BLOCK * max(DILATIONS)
ROPE_THETA = 10000.0
MLSTM_HEADS = 4
MLSTM_QK_DIM = 256
MLSTM_V_DIM = 512
MLSTM_QK_WIDTH = MLSTM_HEADS * MLSTM_QK_DIM
MLSTM_V_WIDTH = MLSTM_HEADS * MLSTM_V_DIM
MLSTM_CHUNK = 256
CONV_WIDTH = 4
CONV_PAD = 8
N_EXPERTS = 32
TOP_K = 4
EXPERT_FF = 1536
SWIGLU_LIMIT = 7.0
SWIGLU_ALPHA = 1.702
NORM_EPS = 1e-6
NEG_BIG = -1e30

MM_TM = 1024
MM_TN = 1024
NORM_TM = 256
EXPERT_TM = 512
EXPERT_TF = 512
EXPERT_TN = 1024
COMBINE_TQ = 128


def _cparams(sem, **kw):
    return pltpu.CompilerParams(dimension_semantics=sem, vmem_limit_bytes=VMEM_LIMIT, **kw)


def _rmsnorm_body(x, g):
    r = lax.rsqrt(jnp.mean(x * x, axis=-1, keepdims=True) + NORM_EPS)
    return x * r * g


def _rmsnorm_kernel(x_ref, g_ref, o_ref):
    o_ref[...] = _rmsnorm_body(x_ref[...], g_ref[...]).astype(o_ref.dtype)


def rmsnorm(x, g, out_dtype):
    t, d = x.shape
    return pl.pallas_call(
        _rmsnorm_kernel,
        grid=(t // NORM_TM,),
        in_specs=[pl.BlockSpec((NORM_TM, d), lambda i: (i, 0)),
                  pl.BlockSpec((1, d), lambda i: (0, 0))],
        out_specs=pl.BlockSpec((NORM_TM, d), lambda i: (i, 0)),
        out_shape=jax.ShapeDtypeStruct((t, d), out_dtype),
        compiler_params=_cparams(("parallel",)),
        name="rmsnorm",
    )(x, g.reshape(1, d))


def _rope_table_kernel(pos_ref, freq_ref, sign_ref, cos_ref, sin_ref):
    ang = pos_ref[...].astype(F32) * freq_ref[...]
    cos_ref[...] = jnp.cos(ang)
    sin_ref[...] = jnp.sin(ang) * sign_ref[...]


def rope_tables(positions_col):
    t = positions_col.shape[0]
    half = HEAD_DIM // 2
    inv_freq = jnp.power(ROPE_THETA, -jnp.arange(half, dtype=F32) / half)
    freq = jnp.concatenate([inv_freq, inv_freq]).reshape(1, HEAD_DIM)
    sign = jnp.concatenate([-jnp.ones((half,), F32), jnp.ones((half,), F32)]).reshape(1, HEAD_DIM)
    tm = 512
    return pl.pallas_call(
        _rope_table_kernel,
        grid=(t // tm,),
        in_specs=[pl.BlockSpec((tm, 1), lambda i: (i, 0)),
                  pl.BlockSpec((1, HEAD_DIM), lambda i: (0, 0)),
                  pl.BlockSpec((1, HEAD_DIM), lambda i: (0, 0))],
        out_specs=[pl.BlockSpec((tm, HEAD_DIM), lambda i: (i, 0))] * 2,
        out_shape=[jax.ShapeDtypeStruct((t, HEAD_DIM), F32)] * 2,
        compiler_params=_cparams(("parallel",)),
        name="rope_tables",
    )(positions_col, freq, sign)


def _inproj_kernel(a_ref, w_ref, wg_ref, cos_ref, sin_ref, z_ref, gate_ref, *, n_q_tiles, n_k_tiles,
                   q_scale):
    j = pl.program_id(1)
    acc = jnp.dot(a_ref[...], w_ref[...], preferred_element_type=F32)

    @pl.when(j == 0)
    def _():
        gate_ref[...] = jnp.dot(a_ref[...], wg_ref[...], preferred_element_type=F32)

    def rope(scale):
        cos = cos_ref[...]
        sin = sin_ref[...]
        for c in range(acc.shape[1] // HEAD_DIM):
            t = acc[:, c * HEAD_DIM:(c + 1) * HEAD_DIM]
            r = pltpu.roll(t, HEAD_DIM // 2, axis=1)
            z_ref[:, c * HEAD_DIM:(c + 1) * HEAD_DIM] = ((t * cos + r * sin) * scale).astype(z_ref.dtype)

    @pl.when(j < n_q_tiles)
    def _():
        rope(q_scale)

    @pl.when((j >= n_q_tiles) & (j < n_q_tiles + n_k_tiles))
    def _():
        rope(1.0)

    @pl.when(j >= n_q_tiles + n_k_tiles)
    def _():
        z_ref[...] = acc.astype(z_ref.dtype)


def in_projection(a, w, w_gates, cos, sin, *, tm=MM_TM, tn=MM_TN):
    t, k = a.shape
    n = w.shape[1]
    tm, tn = min(tm, t), min(tn, n)
    kern = functools.partial(_inproj_kernel, n_q_tiles=ATTN_WIDTH // tn, n_k_tiles=ATTN_WIDTH // tn,
                             q_scale=HEAD_DIM ** -0.5)
    return pl.pallas_call(
        kern,
        grid=(t // tm, n // tn),
        in_specs=[pl.BlockSpec((tm, k), lambda i, j: (i, 0)),
                  pl.BlockSpec((k, tn), lambda i, j: (0, j)),
                  pl.BlockSpec((k, V7X_LANES), lambda i, j: (0, 0)),
                  pl.BlockSpec((tm, HEAD_DIM), lambda i, j: (i, 0)),
                  pl.BlockSpec((tm, HEAD_DIM), lambda i, j: (i, 0))],
        out_specs=[pl.BlockSpec((tm, tn), lambda i, j: (i, j)),
                   pl.BlockSpec((tm, V7X_LANES), lambda i, j: (i, 0))],
        out_shape=[jax.ShapeDtypeStruct((t, n), BF16),
                   jax.ShapeDtypeStruct((t, V7X_LANES), F32)],
        compiler_params=_cparams(("parallel", "arbitrary")),
        name="in_projection",
    )(a, w, w_gates, cos, sin)


def _band_block(qb, kb, vb, bias):
    s = lax.dot_general(qb, kb, (((1,), (1,)), ((), ())), preferred_element_type=F32) + bias
    m = jnp.max(s, axis=-1, keepdims=True)
    p = jnp.exp(s - m)
    l = jnp.sum(p, axis=-1, keepdims=True)
    o = jnp.dot(p.astype(BF16), vb, preferred_element_type=F32) / l
    return o, m + jnp.log(l)


def _attn_kernel(q_ref, kp_ref, kc_ref, vp_ref, vc_ref, o_ref, qf, kf, vf, ob, lb):
    sb = ATTN_SUPER
    blk = ATTN_BLOCK
    n = pl.program_id(2)
    qf[...] = q_ref[...].astype(F32)
    kf[0:sb, :] = kp_ref[...].astype(F32)
    kf[sb:2 * sb, :] = kc_ref[...].astype(F32)
    vf[0:sb, :] = vp_ref[...].astype(F32)
    vf[sb:2 * sb, :] = vc_ref[...].astype(F32)

    qi = lax.broadcasted_iota(I32, (blk, 2 * blk), 0)
    kj = lax.broadcasted_iota(I32, (blk, 2 * blk), 1)
    dist = blk + qi - kj
    band = (dist >= 0) & (dist <= blk)
    bias_in = jnp.where(band, 0.0, NEG_BIG).astype(F32)
    prev_ok = jnp.where(n > 0, 0.0, NEG_BIG).astype(F32)
    bias_edge = bias_in + jnp.where(kj < blk, prev_ok, 0.0)

    for p, d in enumerate(DILATIONS):
        per_class = sb // (d * blk)
        for r in range(d):
            for nb in range(per_class):
                q0 = r + d * blk * nb
                k0 = sb + q0 - d * blk
                if d == 1:
                    qb = qf[pl.ds(q0, blk), :]
                    kb = kf[pl.ds(k0, 2 * blk), :]
                    vb = vf[pl.ds(k0, 2 * blk), :]
                else:
                    qb = qf[pl.ds(q0, blk, stride=d), :]
                    kb = kf[pl.ds(k0, 2 * blk, stride=d), :]
                    vb = vf[pl.ds(k0, 2 * blk, stride=d), :]
                bias = bias_edge if nb == 0 else bias_in
                o, lse = _band_block(qb.astype(BF16), kb.astype(BF16), vb.astype(BF16), bias)
                lse_b = jnp.broadcast_to(lse, (blk, HEAD_DIM))
                if d == 1:
                    ob[p, pl.ds(q0, blk), :] = o
                    lb[p, pl.ds(q0, blk), :] = lse_b
                else:
                    ob[p, pl.ds(q0, blk, stride=d), :] = o
                    lb[p, pl.ds(q0, blk, stride=d), :] = lse_b

    l0, l1, l2 = lb[0], lb[1], lb[2]
    mx = jnp.maximum(jnp.maximum(l0, l1), l2)
    w0, w1, w2 = jnp.exp(l0 - mx), jnp.exp(l1 - mx), jnp.exp(l2 - mx)
    out = (w0 * ob[0] + w1 * ob[1] + w2 * ob[2]) / (w0 + w1 + w2)
    o_ref[...] = out.astype(o_ref.dtype)


def dilated_attention(z, batch, seq):
    t = z.shape[0]
    sb = ATTN_SUPER
    nsb = seq // sb
    h = ATTN_HEADS
    cur = lambda off: (lambda b, hh, n: (b * nsb + n, off + hh))
    prev = lambda off: (lambda b, hh, n: (b * nsb + jnp.maximum(n - 1, 0), off + hh))
    spec = lambda im: pl.BlockSpec((sb, HEAD_DIM), im)
    return pl.pallas_call(
        _attn_kernel,
        grid=(batch, h, nsb),
        in_specs=[spec(cur(0)), spec(prev(h)), spec(cur(h)), spec(prev(2 * h)), spec(cur(2 * h))],
        out_specs=spec(cur(0)),
        out_shape=jax.ShapeDtypeStruct((t, ATTN_WIDTH), BF16),
        scratch_shapes=[pltpu.VMEM((sb, HEAD_DIM), F32),
                        pltpu.VMEM((2 * sb, HEAD_DIM), F32),
                        pltpu.VMEM((2 * sb, HEAD_DIM), F32),
                        pltpu.VMEM((len(DILATIONS), sb, HEAD_DIM), F32),
                        pltpu.VMEM((len(DILATIONS), sb, HEAD_DIM), F32)],
        compiler_params=_cparams(("parallel", "parallel", "arbitrary")),
        name="dilated_attention",
    )(z, z, z, z, z)


def _log_sigmoid(x):
    return jnp.minimum(x, 0.0) - jnp.log(1.0 + jnp.exp(-jnp.abs(x)))


def _mlstm_kernel(q_ref, k_ref, v_ref, om_ref, g_ref, gb_ref, cw_ref, cb_ref, mhg_ref, o_ref,
                  xbuf, c_st, n_st, m_st):
    L = MLSTM_CHUNK
    dqk, dv, nh = MLSTM_QK_DIM, MLSTM_V_DIM, MLSTM_HEADS
    qw = nh * dqk
    c = pl.program_id(1)

    @pl.when(c == 0)
    def _():
        xbuf[0:CONV_PAD, :] = jnp.zeros((CONV_PAD, 2 * qw), F32)
        c_st[...] = jnp.zeros_like(c_st)
        n_st[...] = jnp.zeros_like(n_st)
        m_st[...] = jnp.zeros_like(m_st)

    @pl.when(c > 0)
    def _():
        xbuf[0:CONV_PAD, :] = xbuf[L:L + CONV_PAD, :]

    xbuf[CONV_PAD:CONV_PAD + L, 0:qw] = q_ref[...].astype(F32)
    xbuf[CONV_PAD:CONV_PAD + L, qw:2 * qw] = k_ref[...].astype(F32)
    y = cb_ref[...]
    for j in range(CONV_WIDTH):
        y = y + cw_ref[j:j + 1, :] * xbuf[pl.ds(CONV_PAD - CONV_WIDTH + 1 + j, L), :]
    qk = y * jax.nn.sigmoid(y)

    pre = g_ref[...] + gb_ref[...]
    lf = _log_sigmoid(pre)
    row = lax.broadcasted_iota(I32, (L, L), 0)
    col = lax.broadcasted_iota(I32, (L, L), 1)
    causal = col <= row
    tril = jnp.where(causal, 1.0, 0.0).astype(F32)
    bcum = jnp.dot(tril, lf, preferred_element_type=F32, precision=lax.Precision.HIGHEST)
    pre_t = pre.T
    bcum_t = bcum.T

    for h in range(nh):
        i_c = pre[:, h:h + 1]
        b_c = bcum[:, nh + h:nh + h + 1]
        i_r = pre_t[h:h + 1, :]
        b_r = bcum_t[nh + h:nh + h + 1, :]
        g = bcum[L - 1:L, nh + h:nh + h + 1]
        m = m_st[h, 0:1, 0:1]
        qq = (qk[:, h * dqk:(h + 1) * dqk] * (dqk ** -0.5))
        kk = qk[:, qw + h * dqk:qw + (h + 1) * dqk]
        vv = v_ref[:, h * dv:(h + 1) * dv]
        qb = qq.astype(BF16)

        log_d = jnp.where(causal, b_c - b_r + i_r, NEG_BIG)
        inter = b_c + m
        m_t = jnp.maximum(inter, jnp.max(log_d, axis=-1, keepdims=True))
        w_inter = jnp.exp(inter - m_t)
        s = lax.dot_general(qb, kk.astype(BF16), (((1,), (1,)), ((), ())),
                            preferred_element_type=F32) * jnp.exp(log_d - m_t)
        num = (w_inter * jnp.dot(qb, c_st[h].astype(BF16), preferred_element_type=F32)
               + jnp.dot(s.astype(BF16), vv, preferred_element_type=F32))
        den = (w_inter * jnp.sum(qq * n_st[h], axis=-1, keepdims=True)
               + jnp.sum(s, axis=-1, keepdims=True))
        hh = num / jnp.maximum(jnp.abs(den), jnp.exp(-m_t))

        log_w = g - b_c + i_c
        m_new = jnp.maximum(g + m, jnp.max(log_w, axis=0, keepdims=True))
        decay = jnp.exp(g + m - m_new)
        wk = kk * jnp.exp(log_w - m_new)
        c_st[h] = decay * c_st[h] + jnp.dot(wk.T.astype(BF16), vv, preferred_element_type=F32)
        n_st[h] = decay * n_st[h] + jnp.sum(wk, axis=0, keepdims=True)
        m_st[h] = jnp.broadcast_to(m_new, m_st.shape[1:])

        hn = hh * lax.rsqrt(jnp.mean(hh * hh, axis=-1, keepdims=True) + NORM_EPS)
        hn = hn * mhg_ref[:, h * dv:(h + 1) * dv]
        og = jax.nn.sigmoid(om_ref[:, h * dv:(h + 1) * dv].astype(F32))
        o_ref[:, h * dv:(h + 1) * dv] = (og * hn).astype(o_ref.dtype)


def mlstm(z, gates, gate_bias, conv_w, conv_b, mh_norm_g, batch, seq):
    t = z.shape[0]
    L = MLSTM_CHUNK
    nc = seq // L
    qw, vw = MLSTM_QK_WIDTH, MLSTM_V_WIDTH
    q_off = 3 * ATTN_WIDTH
    rows = lambda blk: (lambda b, c: (b * nc + c, blk))
    const = lambda b, c: (0, 0)
    return pl.pallas_call(
        _mlstm_kernel,
        grid=(batch, nc),
        in_specs=[pl.BlockSpec((L, qw), rows(q_off // qw)),
                  pl.BlockSpec((L, qw), rows(q_off // qw + 1)),
                  pl.BlockSpec((L, vw), rows((q_off + 2 * qw) // vw)),
                  pl.BlockSpec((L, vw), rows((q_off + 2 * qw) // vw + 1)),
                  pl.BlockSpec((L, V7X_LANES), rows(0)),
                  pl.BlockSpec((1, V7X_LANES), const),
                  pl.BlockSpec((CONV_WIDTH, 2 * qw), const),
                  pl.BlockSpec((1, 2 * qw), const),
                  pl.BlockSpec((1, vw), const)],
        out_specs=pl.BlockSpec((L, vw), rows(0)),
        out_shape=jax.ShapeDtypeStruct((t, vw), BF16),
        scratch_shapes=[pltpu.VMEM((L + CONV_PAD, 2 * qw), F32),
                        pltpu.VMEM((MLSTM_HEADS, MLSTM_QK_DIM, MLSTM_V_DIM), F32),
                        pltpu.VMEM((MLSTM_HEADS, 1, MLSTM_QK_DIM), F32),
                        pltpu.VMEM((MLSTM_HEADS, 8, V7X_LANES), F32)],
        compiler_params=_cparams(("parallel", "arbitrary")),
        name="mlstm",
    )(z, z, z, z, gates, gate_bias, conv_w, conv_b.reshape(1, 2 * qw), mh_norm_g.reshape(1, vw))


def _outproj_kernel(a1_ref, a2_ref, w_ref, res_ref, o_ref):
    k1 = a1_ref.shape[1]
    acc = jnp.dot(a1_ref[...], w_ref[0:k1, :], preferred_element_type=F32)
    acc = acc + jnp.dot(a2_ref[...], w_ref[k1:, :], preferred_element_type=F32)
    o_ref[...] = res_ref[...] + acc


def out_projection(a1, a2, w, res, *, tm=MM_TM, tn=MM_TN):
    t, k1 = a1.shape
    k2 = a2.shape[1]
    n = w.shape[1]
    tm, tn = min(tm, t), min(tn, n)
    return pl.pallas_call(
        _outproj_kernel,
        grid=(t // tm, n // tn),
        in_specs=[pl.BlockSpec((tm, k1), lambda i, j: (i, 0)),
                  pl.BlockSpec((tm, k2), lambda i, j: (i, 0)),
                  pl.BlockSpec((k1 + k2, tn), lambda i, j: (0, j)),
                  pl.BlockSpec((tm, tn), lambda i, j: (i, j))],
        out_specs=pl.BlockSpec((tm, tn), lambda i, j: (i, j)),
        out_shape=jax.ShapeDtypeStruct((t, n), F32),
        compiler_params=_cparams(("parallel", "arbitrary")),
        name="out_projection",
    )(a1, a2, w, res)


def _pack_bf16_pairs(f):
    half = f.shape[1] // 2
    fb = f.astype(BF16).astype(F32)
    lo = pltpu.bitcast(fb[:, :half], U32) >> 16
    hi = pltpu.bitcast(fb[:, half:], U32) & jnp.uint32(0xFFFF0000)
    return lo | hi


def _unpack_bf16_pairs(w):
    lo = pltpu.bitcast(w << 16, F32).astype(BF16)
    hi = pltpu.bitcast(w & jnp.uint32(0xFFFF0000), F32).astype(BF16)
    return jnp.concatenate([lo, hi], axis=-1)


def _router_kernel(h_ref, g_ref, wr_ref, br_ref, fpk_ref, e_ref, gate_ref, rank_ref, cnt_ref, carry):
    i = pl.program_id(0)

    @pl.when(i == 0)
    def _():
        carry[...] = jnp.zeros_like(carry)

    f = _rmsnorm_body(h_ref[...], g_ref[...])
    fpk_ref[...] = _pack_bf16_pairs(f)
    logits = jnp.dot(f, wr_ref[...], preferred_element_type=F32,
                     precision=lax.Precision.HIGHEST) + br_ref[...]
    lane = lax.broadcasted_iota(I32, logits.shape, 1)
    cur = logits
    vals, idxs = [], []
    for _ in range(TOP_K):
        mx = jnp.max(cur, axis=-1, keepdims=True)
        idx = jnp.min(jnp.where(cur == mx, lane, V7X_LANES), axis=-1, keepdims=True)
        vals.append(mx)
        idxs.append(idx)
        cur = jnp.where(lane == idx, -jnp.inf, cur)
    exps = [jnp.exp(v - vals[0]) for v in vals]
    tot = exps[0] + exps[1] + exps[2] + exps[3]
    e_out = jnp.zeros(logits.shape, I32)
    g_out = jnp.zeros(logits.shape, F32)
    for k in range(TOP_K):
        e_out = jnp.where(lane == k, idxs[k], e_out)
        g_out = jnp.where(lane == k, exps[k] / tot, g_out)
    e_ref[...] = e_out
    gate_ref[...] = g_out

    tm = logits.shape[0]
    onehots = [lane == idxs[k] for k in range(TOP_K)]
    member = jnp.zeros(logits.shape, F32)
    for k in range(TOP_K):
        member = jnp.where(onehots[k], 1.0, member)
    row = lax.broadcasted_iota(I32, (tm, tm), 0)
    col = lax.broadcasted_iota(I32, (tm, tm), 1)
    stril = jnp.where(col < row, 1.0, 0.0).astype(BF16)
    before = carry[0:1, :] + jnp.dot(stril, member.astype(BF16), preferred_element_type=F32)
    r_out = jnp.zeros(logits.shape, I32)
    for k in range(TOP_K):
        rk = jnp.sum(jnp.where(onehots[k], before, 0.0), axis=-1, keepdims=True)
        r_out = jnp.where(lane == k, rk.astype(I32), r_out)
    rank_ref[...] = r_out
    total = carry[0:1, :] + jnp.sum(member, axis=0, keepdims=True)
    carry[...] = jnp.broadcast_to(total, carry.shape)
    cnt_ref[...] = jnp.broadcast_to(total, cnt_ref.shape)


def ffn_norm_router(h, g, w_router, b_router, *, tm=NORM_TM):
    t, d = h.shape
    nt = t // tm
    wr = jnp.zeros((d, V7X_LANES), F32).at[:, :N_EXPERTS].set(w_router)
    br = jnp.full((1, V7X_LANES), NEG_BIG, F32).at[0, :N_EXPERTS].set(b_router)
    rows = lambda i: (i, 0)
    const = lambda i: (0, 0)
    return pl.pallas_call(
        _router_kernel,
        grid=(nt,),
        in_specs=[pl.BlockSpec((tm, d), rows),
                  pl.BlockSpec((1, d), const),
                  pl.BlockSpec((d, V7X_LANES), const),
                  pl.BlockSpec((1, V7X_LANES), const)],
        out_specs=[pl.BlockSpec((tm, d // 2), rows),
                   pl.BlockSpec((tm, V7X_LANES), rows),
                   pl.BlockSpec((tm, V7X_LANES), rows),
                   pl.BlockSpec((tm, V7X_LANES), rows),
                   pl.BlockSpec((8, V7X_LANES), const)],
        out_shape=[jax.ShapeDtypeStruct((t, d // 2), U32),
                   jax.ShapeDtypeStruct((t, V7X_LANES), I32),
                   jax.ShapeDtypeStruct((t, V7X_LANES), F32),
                   jax.ShapeDtypeStruct((t, V7X_LANES), I32),
                   jax.ShapeDtypeStruct((8, V7X_LANES), F32)],
        scratch_shapes=[pltpu.VMEM((8, V7X_LANES), F32)],
        compiler_params=_cparams(("arbitrary",)),
        name="ffn_norm_router",
    )(h, g.reshape(1, d), wr, br)


def routing_tables(top_e, rank, counts, n_tokens):
    tk = n_tokens * TOP_K
    tm = EXPERT_TM
    n_blk = -(-(tk + N_EXPERTS * (tm - 1)) // tm)
    padded = (counts + tm - 1) // tm * tm
    pend = jnp.cumsum(padded)
    pstart = pend - padded
    pos = (pstart[top_e] + rank).reshape(tk).astype(I32)
    blk_start = jnp.arange(n_blk, dtype=I32) * tm
    blk_e = jnp.minimum(jnp.searchsorted(pend, blk_start, side='right'), N_EXPERTS - 1).astype(I32)
    n_used = (pend[-1] // tm).astype(I32).reshape(1)
    is_last = jnp.any((blk_start[:, None] + tm == pend[None, :]) & (padded[None, :] > 0), axis=1)
    zero_blk = (is_last | (blk_start >= pend[-1])).astype(I32)
    return pos, blk_e, n_used, zero_blk


def _dispatch_kernel(pos_ref, zero_ref, f_ref, xs_hbm, zbuf, sem, zsem, *, rows, n_blk):
    i = pl.program_id(0)
    tm = f_ref.shape[0]

    def zero_copy(b):
        return pltpu.make_async_copy(zbuf, xs_hbm.at[pl.ds(b * rows, rows)], zsem)

    @pl.when(i == 0)
    def _():
        zbuf[...] = jnp.zeros_like(zbuf)

        def issue(b, c):
            @pl.when(zero_ref[b] != 0)
            def _():
                zero_copy(b).start()
            return c

        def drain(b, c):
            @pl.when(zero_ref[b] != 0)
            def _():
                zero_copy(b).wait()
            return c

        lax.fori_loop(0, n_blk, issue, 0)
        lax.fori_loop(0, n_blk, drain, 0)

    def row_copy(t, k):
        row = pos_ref[(i * tm + t) * TOP_K + k]
        return pltpu.make_async_copy(f_ref.at[pl.ds(t, 1)], xs_hbm.at[pl.ds(row, 1)], sem)

    def issue_rows(t, c):
        for k in range(TOP_K):
            row_copy(t, k).start()
        return c

    def drain_rows(t, c):
        for k in range(TOP_K):
            row_copy(t, k).wait()
        return c

    lax.fori_loop(0, tm, issue_rows, 0)
    lax.fori_loop(0, tm, drain_rows, 0)


def dispatch(fpk, pos, zero_blk, *, rows=EXPERT_TM, tm=NORM_TM):
    t, half = fpk.shape
    n_blk = zero_blk.shape[0]
    grid_spec = pltpu.PrefetchScalarGridSpec(
        num_scalar_prefetch=2,
        grid=(t // tm,),
        in_specs=[pl.BlockSpec((tm, half), lambda i, p, z: (i, 0))],
        out_specs=pl.BlockSpec(memory_space=pl.ANY),
        scratch_shapes=[pltpu.VMEM((rows, half), fpk.dtype), pltpu.SemaphoreType.DMA(()),
                        pltpu.SemaphoreType.DMA(())],
    )
    return pl.pallas_call(
        functools.partial(_dispatch_kernel, rows=rows, n_blk=n_blk),
        grid_spec=grid_spec,
        out_shape=jax.ShapeDtypeStruct((n_blk * rows, half), fpk.dtype),
        compiler_params=_cparams(("arbitrary",), has_side_effects=True),
        name="dispatch",
    )(pos, zero_blk, fpk)


def _expert_up_kernel(blk_e, nused, xs_ref, wg_ref, wu_ref, bg_ref, bu_ref, act_ref, xb):
    i = pl.program_id(0)
    f = pl.program_id(1)

    @pl.when(i < nused[0])
    def _():
        @pl.when(f == 0)
        def _():
            xb[...] = _unpack_bf16_pairs(xs_ref[...])

        x = xb[...]
        glu = jnp.dot(x, wg_ref[...], preferred_element_type=F32) + bg_ref[...]
        up = jnp.dot(x, wu_ref[...], preferred_element_type=F32) + bu_ref[...]
        glu = jnp.minimum(glu, SWIGLU_LIMIT)
        up = jnp.clip(up, -SWIGLU_LIMIT, SWIGLU_LIMIT)
        act_ref[...] = ((up + 1.0) * (glu * jax.nn.sigmoid(SWIGLU_ALPHA * glu))).astype(act_ref.dtype)

    @pl.when(i >= nused[0])
    def _():
        act_ref[...] = jnp.zeros_like(act_ref)


def expert_up(xs, w_gate_up, b_gate_up, blk_e, n_used, *, tm=EXPERT_TM, tf=EXPERT_TF):
    n_rows, half = xs.shape
    d = 2 * half
    ff = EXPERT_FF
    nf = ff // tf
    n_blk = n_rows // tm

    def blk(i, nu):
        return jnp.minimum(i, nu[0] - 1)

    def ftile(i, f, nu):
        return jnp.where(i < nu[0], f, nf - 1)

    grid_spec = pltpu.PrefetchScalarGridSpec(
        num_scalar_prefetch=2,
        grid=(n_blk, nf),
        in_specs=[pl.BlockSpec((tm, half), lambda i, f, be, nu: (blk(i, nu), 0)),
                  pl.BlockSpec((None, d, tf), lambda i, f, be, nu: (be[blk(i, nu)], 0, ftile(i, f, nu))),
                  pl.BlockSpec((None, d, tf), lambda i, f, be, nu: (be[blk(i, nu)], 0, nf + ftile(i, f, nu))),
                  pl.BlockSpec((None, 1, tf), lambda i, f, be, nu: (be[blk(i, nu)], 0, ftile(i, f, nu))),
                  pl.BlockSpec((None, 1, tf), lambda i, f, be, nu: (be[blk(i, nu)], 0, nf + ftile(i, f, nu)))],
        out_specs=pl.BlockSpec((tm, tf), lambda i, f, be, nu: (i, f)),
        scratch_shapes=[pltpu.VMEM((tm, d), BF16)],
    )
    return pl.pallas_call(
        _expert_up_kernel,
        grid_spec=grid_spec,
        out_shape=jax.ShapeDtypeStruct((n_rows, ff), BF16),
        compiler_params=_cparams(("arbitrary", "arbitrary")),
        name="expert_up",
    )(blk_e, n_used, xs, w_gate_up, w_gate_up, b_gate_up, b_gate_up)


def _expert_down_kernel(blk_e, nused, act_ref, wd_ref, bd_ref, y_ref):
    i = pl.program_id(0)

    @pl.when(i < nused[0])
    def _():
        y_ref[...] = jnp.dot(act_ref[...], wd_ref[...], preferred_element_type=F32) + bd_ref[...]

    @pl.when(i >= nused[0])
    def _():
        y_ref[...] = jnp.zeros_like(y_ref)


def expert_down(act, w_down, b_down, blk_e, n_used, *, tm=EXPERT_TM, tn=EXPERT_TN):
    n_rows, ff = act.shape
    d = w_down.shape[2]
    tn = min(tn, d)
    nn = d // tn
    n_blk = n_rows // tm

    def blk(i, nu):
        return jnp.minimum(i, nu[0] - 1)

    def ntile(i, j, nu):
        return jnp.where(i < nu[0], j, nn - 1)

    grid_spec = pltpu.PrefetchScalarGridSpec(
        num_scalar_prefetch=2,
        grid=(n_blk, nn),
        in_specs=[pl.BlockSpec((tm, ff), lambda i, j, be, nu: (blk(i, nu), 0)),
                  pl.BlockSpec((None, ff, tn), lambda i, j, be, nu: (be[blk(i, nu)], 0, ntile(i, j, nu))),
                  pl.BlockSpec((None, 1, tn), lambda i, j, be, nu: (be[blk(i, nu)], 0, ntile(i, j, nu)))],
        out_specs=pl.BlockSpec((tm, tn), lambda i, j, be, nu: (i, j)),
    )
    return pl.pallas_call(
        _expert_down_kernel,
        grid_spec=grid_spec,
        out_shape=jax.ShapeDtypeStruct((n_rows, d), F32),
        compiler_params=_cparams(("arbitrary", "arbitrary")),
        name="expert_down",
    )(blk_e, n_used, act, w_down, b_down)


def _combine_kernel(pos_ref, y_hbm, h_ref, gate_ref, g_ref, o_ref, a_ref, buf, sem, *, tq):
    i = pl.program_id(0)

    def row_copy(t, k):
        row = pos_ref[(i * tq + t) * TOP_K + k]
        return pltpu.make_async_copy(y_hbm.at[pl.ds(row, 1)], buf.at[k, pl.ds(t, 1)], sem)

    def issue(t, carry):
        for k in range(TOP_K):
            row_copy(t, k).start()
        return carry

    def drain(t, carry):
        for k in range(TOP_K):
            row_copy(t, k).wait()
        return carry

    lax.fori_loop(0, tq, issue, 0)
    lax.fori_loop(0, tq, drain, 0)
    gates = gate_ref[...]
    moe = ((gates[:, 0:1] * buf[0] + gates[:, 1:2] * buf[1])
           + (gates[:, 2:3] * buf[2] + gates[:, 3:4] * buf[3]))
    out = h_ref[...] + moe
    o_ref[...] = out
    a_ref[...] = _rmsnorm_body(out, g_ref[...]).astype(a_ref.dtype)


def combine(y, pos, h, gates, g_next, *, tq=COMBINE_TQ):
    t, d = h.shape
    rows = lambda i, p: (i, 0)
    grid_spec = pltpu.PrefetchScalarGridSpec(
        num_scalar_prefetch=1,
        grid=(t // tq,),
        in_specs=[pl.BlockSpec(memory_space=pl.ANY),
                  pl.BlockSpec((tq, d), rows),
                  pl.BlockSpec((tq, V7X_LANES), rows),
                  pl.BlockSpec((1, d), lambda i, p: (0, 0))],
        out_specs=[pl.BlockSpec((tq, d), rows), pl.BlockSpec((tq, d), rows)],
        scratch_shapes=[pltpu.VMEM((TOP_K, tq, d), F32), pltpu.SemaphoreType.DMA(())],
    )
    return pl.pallas_call(
        functools.partial(_combine_kernel, tq=tq),
        grid_spec=grid_spec,
        out_shape=[jax.ShapeDtypeStruct((t, d), F32), jax.ShapeDtypeStruct((t, d), BF16)],
        compiler_params=_cparams(("arbitrary",)),
        name="combine",
    )(pos, y, h, gates, g_next.reshape(1, d))


def _ple_kernel(a_ref, w_ref, p_ref, wp_ref, h_ref, o_ref):
    gate = jax.nn.sigmoid(jnp.dot(a_ref[...], w_ref[...], preferred_element_type=F32))
    emb = jnp.dot(p_ref[...], wp_ref[...], preferred_element_type=F32)
    o_ref[...] = h_ref[...] + gate * emb


def ple(a, w_gate, p, w_proj, h, *, tm=MM_TM // 2, tn=MM_TN):
    t, k = a.shape
    n = w_gate.shape[1]
    pk = p.shape[1]
    tm, tn = min(tm, t), min(tn, n)
    return pl.pallas_call(
        _ple_kernel,
        grid=(t // tm, n // tn),
        in_specs=[pl.BlockSpec((tm, k), lambda i, j: (i, 0)),
                  pl.BlockSpec((k, tn), lambda i, j: (0, j)),
                  pl.BlockSpec((tm, pk), lambda i, j: (i, 0)),
                  pl.BlockSpec((pk, tn), lambda i, j: (0, j)),
                  pl.BlockSpec((tm, tn), lambda i, j: (i, j))],
        out_specs=pl.BlockSpec((tm, tn), lambda i, j: (i, j)),
        out_shape=jax.ShapeDtypeStruct((t, n), F32),
        compiler_params=_cparams(("parallel", "arbitrary")),
        name="ple",
    )(a, w_gate, p, w_proj, h)


def kernel(x, p, positions, attn_norm_g, w_in, conv_w, conv_b, b_igate, b_fgate, mh_norm_g, w_out,
           ffn_norm_g, w_router, b_router, w_gate_up, b_gate_up, w_down, b_down, ple_norm_g, w_ple_gate,
           w_ple_proj, final_norm_g):
    batch, seq, d = x.shape
    depth = p.shape[0]
    t = batch * seq
    n_main = 3 * ATTN_WIDTH + 2 * MLSTM_QK_WIDTH + 2 * MLSTM_V_WIDTH
    nh = MLSTM_HEADS
    h = x.reshape(t, d)
    cos, sin = rope_tables(positions.reshape(t, 1))
    for i in range(depth):
        a = rmsnorm(h, attn_norm_g[i], BF16)
        w_main = w_in[i][:, :n_main].astype(BF16)
        w_gates = jnp.zeros((d, V7X_LANES), BF16).at[:, :2 * nh].set(w_in[i][:, n_main:].astype(BF16))
        z, gates = in_projection(a, w_main, w_gates, cos, sin)
        attn = dilated_attention(z, batch, seq)
        gate_bias = jnp.zeros((1, V7X_LANES), F32).at[0, :nh].set(b_igate[i]).at[0, nh:2 * nh].set(b_fgate[i])
        ml = mlstm(z, gates, gate_bias, conv_w[i], conv_b[i], mh_norm_g[i], batch, seq)
        h = out_projection(attn, ml, w_out[i].astype(BF16), h)
        fpk, top_e, top_g, rank, counts = ffn_norm_router(h, ffn_norm_g[i], w_router[i], b_router[i])
        pos, blk_e, n_used, zero_blk = routing_tables(top_e[:, :TOP_K], rank[:, :TOP_K],
                                                      counts[0, :N_EXPERTS].astype(I32), t)
        xs = dispatch(fpk, pos, zero_blk)
        act = expert_up(xs, w_gate_up[i].astype(BF16), b_gate_up[i].reshape(N_EXPERTS, 1, 2 * EXPERT_FF),
                        blk_e, n_used)
        y = expert_down(act, w_down[i].astype(BF16), b_down[i].reshape(N_EXPERTS, 1, d), blk_e, n_used)
        h, a2 = combine(y, pos, h, top_g, ple_norm_g[i])
        h = ple(a2, w_ple_gate[i].astype(BF16), p[i].reshape(t, -1).astype(BF16), w_ple_proj[i].astype(BF16), h)
    return rmsnorm(h, final_norm_g, x.dtype).reshape(batch, seq, d)
```

```python
import functools

import jax
import jax.numpy as jnp
from jax import lax
from jax.experimental import pallas as pl
from jax.experimental.pallas import tpu as pltpu

F32 = jnp.float32
BF16 = jnp.bfloat16
U32 = jnp.uint32
I32 = jnp.int32

V7X_VMEM_BYTES = 64 * 1024 * 1024
V7X_LANES = 128
VMEM_LIMIT = V7X_VMEM_BYTES - 8 * 1024 * 1024

ATTN_HEADS = 16
HEAD_DIM = 128
ATTN_WIDTH = ATTN_HEADS * HEAD_DIM
ATTN_BLOCK = 128
DILATIONS = (1, 4, 16)
ATTN_SUPER = ATTN_BLOCK * max(DILATIONS)
ROPE_THETA = 10000.0
MLSTM_HEADS = 4
MLSTM_QK_DIM = 256
MLSTM_V_DIM = 512
MLSTM_QK_WIDTH = MLSTM_HEADS * MLSTM_QK_DIM
MLSTM_V_WIDTH = MLSTM_HEADS * MLSTM_V_DIM
MLSTM_CHUNK = 256
CONV_WIDTH = 4
CONV_PAD = 8
N_EXPERTS = 32
TOP_K = 4
EXPERT_FF = 1536
SWIGLU_LIMIT = 7.0
SWIGLU_ALPHA = 1.702
NORM_EPS = 1e-6
NEG_BIG = -1e30

MM_TM = 1024
MM_TN = 1024
NORM_TM = 256
EXPERT_TM = 512
EXPERT_TF = 256
EXPERT_TN = 2048
COMBINE_TQ = 128
DMA_ISSUE_UNROLL = 4


def _cparams(sem, **kw):
    return pltpu.CompilerParams(dimension_semantics=sem, vmem_limit_bytes=VMEM_LIMIT, **kw)


def _rmsnorm_body(x, g):
    r = lax.rsqrt(jnp.mean(x * x, axis=-1, keepdims=True) + NORM_EPS)
    return x * r * g


def _rmsnorm_kernel(x_ref, g_ref, o_ref):
    o_ref[...] = _rmsnorm_body(x_ref[...], g_ref[...]).astype(o_ref.dtype)


def rmsnorm(x, g, out_dtype):
    t, d = x.shape
    return pl.pallas_call(
        _rmsnorm_kernel,
        grid=(t // NORM_TM,),
        in_specs=[pl.BlockSpec((NORM_TM, d), lambda i: (i, 0)),
                  pl.BlockSpec((1, d), lambda i: (0, 0))],
        out_specs=pl.BlockSpec((NORM_TM, d), lambda i: (i, 0)),
        out_shape=jax.ShapeDtypeStruct((t, d), out_dtype),
        compiler_params=_cparams(("parallel",)),
        name="rmsnorm",
    )(x, g.reshape(1, d))


def _rope_table_kernel(pos_ref, freq_ref, sign_ref, cos_ref, sin_ref):
    ang = pos_ref[...].astype(F32) * freq_ref[...]
    cos_ref[...] = jnp.cos(ang)
    sin_ref[...] = jnp.sin(ang) * sign_ref[...]


def rope_tables(positions_col):
    t = positions_col.shape[0]
    half = HEAD_DIM // 2
    inv_freq = jnp.power(ROPE_THETA, -jnp.arange(half, dtype=F32) / half)
    freq = jnp.concatenate([inv_freq, inv_freq]).reshape(1, HEAD_DIM)
    sign = jnp.concatenate([-jnp.ones((half,), F32), jnp.ones((half,), F32)]).reshape(1, HEAD_DIM)
    tm = 512
    return pl.pallas_call(
        _rope_table_kernel,
        grid=(t // tm,),
        in_specs=[pl.BlockSpec((tm, 1), lambda i: (i, 0)),
                  pl.BlockSpec((1, HEAD_DIM), lambda i: (0, 0)),
                  pl.BlockSpec((1, HEAD_DIM), lambda i: (0, 0))],
        out_specs=[pl.BlockSpec((tm, HEAD_DIM), lambda i: (i, 0))] * 2,
        out_shape=[jax.ShapeDtypeStruct((t, HEAD_DIM), F32)] * 2,
        compiler_params=_cparams(("parallel",)),
        name="rope_tables",
    )(positions_col, freq, sign)


def _inproj_kernel(a_ref, w_ref, wg_ref, cos_ref, sin_ref, z_ref, gate_ref, *, n_q_tiles, n_k_tiles,
                   q_scale):
    j = pl.program_id(1)
    acc = jnp.dot(a_ref[...], w_ref[...], preferred_element_type=F32)

    @pl.when(j == 0)
    def _():
        gate_ref[...] = jnp.dot(a_ref[...], wg_ref[...], preferred_element_type=F32)

    def rope(scale):
        cos = cos_ref[...]
        sin = sin_ref[...]
        for c in range(acc.shape[1] // HEAD_DIM):
            t = acc[:, c * HEAD_DIM:(c + 1) * HEAD_DIM]
            r = pltpu.roll(t, HEAD_DIM // 2, axis=1)
            z_ref[:, c * HEAD_DIM:(c + 1) * HEAD_DIM] = ((t * cos + r * sin) * scale).astype(z_ref.dtype)

    @pl.when(j < n_q_tiles)
    def _():
        rope(q_scale)

    @pl.when((j >= n_q_tiles) & (j < n_q_tiles + n_k_tiles))
    def _():
        rope(1.0)

    @pl.when(j >= n_q_tiles + n_k_tiles)
    def _():
        z_ref[...] = acc.astype(z_ref.dtype)


def in_projection(a, w, n, w_gates, cos, sin, *, tm=MM_TM, tn=MM_TN):
    t, k = a.shape
    tm, tn = min(tm, t), min(tn, n)
    kern = functools.partial(_inproj_kernel, n_q_tiles=ATTN_WIDTH // tn, n_k_tiles=ATTN_WIDTH // tn,
                             q_scale=HEAD_DIM ** -0.5)
    return pl.pallas_call(
        kern,
        grid=(t // tm, n // tn),
        in_specs=[pl.BlockSpec((tm, k), lambda i, j: (i, 0)),
                  pl.BlockSpec((k, tn), lambda i, j: (0, j)),
                  pl.BlockSpec((k, V7X_LANES), lambda i, j: (0, 0)),
                  pl.BlockSpec((tm, HEAD_DIM), lambda i, j: (i, 0)),
                  pl.BlockSpec((tm, HEAD_DIM), lambda i, j: (i, 0))],
        out_specs=[pl.BlockSpec((tm, tn), lambda i, j: (i, j)),
                   pl.BlockSpec((tm, V7X_LANES), lambda i, j: (i, 0))],
        out_shape=[jax.ShapeDtypeStruct((t, n), BF16),
                   jax.ShapeDtypeStruct((t, V7X_LANES), F32)],
        compiler_params=_cparams(("parallel", "arbitrary")),
        name="in_projection",
    )(a, w, w_gates, cos, sin)


def _band_block(qb, kb, vb, bias):
    s = lax.dot_general(qb, kb, (((1,), (1,)), ((), ())), preferred_element_type=F32) + bias
    m = jnp.max(s, axis=-1, keepdims=True)
    p = jnp.exp(s - m)
    l = jnp.sum(p, axis=-1, keepdims=True)
    o = jnp.dot(p.astype(BF16), vb, preferred_element_type=F32) / l
    return o, m + jnp.log(l)


def _attn_kernel(q_ref, kp_ref, kc_ref, vp_ref, vc_ref, o_ref, qf, kf, vf, ob, lb):
    sb = ATTN_SUPER
    blk = ATTN_BLOCK
    n = pl.program_id(2)
    qf[...] = q_ref[...].astype(F32)
    kf[0:sb, :] = kp_ref[...].astype(F32)
    kf[sb:2 * sb, :] = kc_ref[...].astype(F32)
    vf[0:sb, :] = vp_ref[...].astype(F32)
    vf[sb:2 * sb, :] = vc_ref[...].astype(F32)

    qi = lax.broadcasted_iota(I32, (blk, 2 * blk), 0)
    kj = lax.broadcasted_iota(I32, (blk, 2 * blk), 1)
    dist = blk + qi - kj
    band = (dist >= 0) & (dist <= blk)
    bias_in = jnp.where(band, 0.0, NEG_BIG).astype(F32)
    prev_ok = jnp.where(n > 0, 0.0, NEG_BIG).astype(F32)
    bias_edge = bias_in + jnp.where(kj < blk, prev_ok, 0.0)

    for p, d in enumerate(DILATIONS):
        per_class = sb // (d * blk)
        for r in range(d):
            for nb in range(per_class):
                q0 = r + d * blk * nb
                k0 = sb + q0 - d * blk
                if d == 1:
                    qb = qf[pl.ds(q0, blk), :]
                    kb = kf[pl.ds(k0, 2 * blk), :]
                    vb = vf[pl.ds(k0, 2 * blk), :]
                else:
                    qb = qf[pl.ds(q0, blk, stride=d), :]
                    kb = kf[pl.ds(k0, 2 * blk, stride=d), :]
                    vb = vf[pl.ds(k0, 2 * blk, stride=d), :]
                bias = bias_edge if nb == 0 else bias_in
                o, lse = _band_block(qb.astype(BF16), kb.astype(BF16), vb.astype(BF16), bias)
                lse_b = jnp.broadcast_to(lse, (blk, HEAD_DIM))
                if d == 1:
                    ob[p, pl.ds(q0, blk), :] = o
                    lb[p, pl.ds(q0, blk), :] = lse_b
                else:
                    ob[p, pl.ds(q0, blk, stride=d), :] = o
                    lb[p, pl.ds(q0, blk, stride=d), :] = lse_b

    l0, l1, l2 = lb[0], lb[1], lb[2]
    mx = jnp.maximum(jnp.maximum(l0, l1), l2)
    w0, w1, w2 = jnp.exp(l0 - mx), jnp.exp(l1 - mx), jnp.exp(l2 - mx)
    out = (w0 * ob[0] + w1 * ob[1] + w2 * ob[2]) / (w0 + w1 + w2)
    o_ref[...] = out.astype(o_ref.dtype)


def dilated_attention(z, batch, seq):
    t = z.shape[0]
    sb = ATTN_SUPER
    nsb = seq // sb
    h = ATTN_HEADS
    cur = lambda off: (lambda b, hh, n: (b * nsb + n, off + hh))
    prev = lambda off: (lambda b, hh, n: (b * nsb + jnp.maximum(n - 1, 0), off + hh))
    spec = lambda im: pl.BlockSpec((sb, HEAD_DIM), im)
    return pl.pallas_call(
        _attn_kernel,
        grid=(batch, h, nsb),
        in_specs=[spec(cur(0)), spec(prev(h)), spec(cur(h)), spec(prev(2 * h)), spec(cur(2 * h))],
        out_specs=spec(cur(0)),
        out_shape=jax.ShapeDtypeStruct((t, ATTN_WIDTH), BF16),
        scratch_shapes=[pltpu.VMEM((sb, HEAD_DIM), F32),
                        pltpu.VMEM((2 * sb, HEAD_DIM), F32),
                        pltpu.VMEM((2 * sb, HEAD_DIM), F32),
                        pltpu.VMEM((len(DILATIONS), sb, HEAD_DIM), F32),
                        pltpu.VMEM((len(DILATIONS), sb, HEAD_DIM), F32)],
        compiler_params=_cparams(("parallel", "parallel", "arbitrary")),
        name="dilated_attention",
    )(z, z, z, z, z)


def _log_sigmoid(x):
    return jnp.minimum(x, 0.0) - jnp.log(1.0 + jnp.exp(-jnp.abs(x)))


def _mlstm_kernel(q_ref, k_ref, v_ref, om_ref, g_ref, gb_ref, cw_ref, cb_ref, mhg_ref, o_ref,
                  xbuf, c_st, n_st, m_st):
    L = MLSTM_CHUNK
    dqk, dv, nh = MLSTM_QK_DIM, MLSTM_V_DIM, MLSTM_HEADS
    qw = nh * dqk
    c = pl.program_id(1)

    @pl.when(c == 0)
    def _():
        xbuf[0:CONV_PAD, :] = jnp.zeros((CONV_PAD, 2 * qw), F32)
        c_st[...] = jnp.zeros_like(c_st)
        n_st[...] = jnp.zeros_like(n_st)
        m_st[...] = jnp.zeros_like(m_st)

    @pl.when(c > 0)
    def _():
        xbuf[0:CONV_PAD, :] = xbuf[L:L + CONV_PAD, :]

    xbuf[CONV_PAD:CONV_PAD + L, 0:qw] = q_ref[...].astype(F32)
    xbuf[CONV_PAD:CONV_PAD + L, qw:2 * qw] = k_ref[...].astype(F32)
    y = cb_ref[...]
    for j in range(CONV_WIDTH):
        y = y + cw_ref[j:j + 1, :] * xbuf[pl.ds(CONV_PAD - CONV_WIDTH + 1 + j, L), :]
    qk = y * jax.nn.sigmoid(y)

    pre = g_ref[...] + gb_ref[...]
    lf = _log_sigmoid(pre)
    row = lax.broadcasted_iota(I32, (L, L), 0)
    col = lax.broadcasted_iota(I32, (L, L), 1)
    causal = col <= row
    tril = jnp.where(causal, 1.0, 0.0).astype(F32)
    bcum = jnp.dot(tril, lf, preferred_element_type=F32, precision=lax.Precision.HIGHEST)
    pre_t = pre.T
    bcum_t = bcum.T

    for h in range(nh):
        i_c = pre[:, h:h + 1]
        b_c = bcum[:, nh + h:nh + h + 1]
        i_r = pre_t[h:h + 1, :]
        b_r = bcum_t[nh + h:nh + h + 1, :]
        g = bcum[L - 1:L, nh + h:nh + h + 1]
        m = m_st[h, 0:1, 0:1]
        qq = (qk[:, h * dqk:(h + 1) * dqk] * (dqk ** -0.5))
        kk = qk[:, qw + h * dqk:qw + (h + 1) * dqk]
        vv = v_ref[:, h * dv:(h + 1) * dv]
        qb = qq.astype(BF16)

        log_d = jnp.where(causal, b_c - b_r + i_r, NEG_BIG)
        inter = b_c + m
        m_t = jnp.maximum(inter, jnp.max(log_d, axis=-1, keepdims=True))
        w_inter = jnp.exp(inter - m_t)
        s = lax.dot_general(qb, kk.astype(BF16), (((1,), (1,)), ((), ())),
                            preferred_element_type=F32) * jnp.exp(log_d - m_t)
        num = (w_inter * jnp.dot(qb, c_st[h].astype(BF16), preferred_element_type=F32)
               + jnp.dot(s.astype(BF16), vv, preferred_element_type=F32))
        den = (w_inter * jnp.sum(qq * n_st[h], axis=-1, keepdims=True)
               + jnp.sum(s, axis=-1, keepdims=True))
        hh = num / jnp.maximum(jnp.abs(den), jnp.exp(-m_t))

        log_w = g - b_c + i_c
        m_new = jnp.maximum(g + m, jnp.max(log_w, axis=0, keepdims=True))
        decay = jnp.exp(g + m - m_new)
        wk = kk * jnp.exp(log_w - m_new)
        c_st[h] = decay * c_st[h] + jnp.dot(wk.T.astype(BF16), vv, preferred_element_type=F32)
        n_st[h] = decay * n_st[h] + jnp.sum(wk, axis=0, keepdims=True)
        m_st[h] = jnp.broadcast_to(m_new, m_st.shape[1:])

        hn = hh * lax.rsqrt(jnp.mean(hh * hh, axis=-1, keepdims=True) + NORM_EPS)
        hn = hn * mhg_ref[:, h * dv:(h + 1) * dv]
        og = jax.nn.sigmoid(om_ref[:, h * dv:(h + 1) * dv].astype(F32))
        o_ref[:, h * dv:(h + 1) * dv] = (og * hn).astype(o_ref.dtype)


def mlstm(z, gates, gate_bias, conv_w, conv_b, mh_norm_g, batch, seq):
    t = z.shape[0]
    L = MLSTM_CHUNK
    nc = seq // L
    qw, vw = MLSTM_QK_WIDTH, MLSTM_V_WIDTH
    q_off = 3 * ATTN_WIDTH
    rows = lambda blk: (lambda b, c: (b * nc + c, blk))
    const = lambda b, c: (0, 0)
    return pl.pallas_call(
        _mlstm_kernel,
        grid=(batch, nc),
        in_specs=[pl.BlockSpec((L, qw), rows(q_off // qw)),
                  pl.BlockSpec((L, qw), rows(q_off // qw + 1)),
                  pl.BlockSpec((L, vw), rows((q_off + 2 * qw) // vw)),
                  pl.BlockSpec((L, vw), rows((q_off + 2 * qw) // vw + 1)),
                  pl.BlockSpec((L, V7X_LANES), rows(0)),
                  pl.BlockSpec((1, V7X_LANES), const),
                  pl.BlockSpec((CONV_WIDTH, 2 * qw), const),
                  pl.BlockSpec((1, 2 * qw), const),
                  pl.BlockSpec((1, vw), const)],
        out_specs=pl.BlockSpec((L, vw), rows(0)),
        out_shape=jax.ShapeDtypeStruct((t, vw), BF16),
        scratch_shapes=[pltpu.VMEM((L + CONV_PAD, 2 * qw), F32),
                        pltpu.VMEM((MLSTM_HEADS, MLSTM_QK_DIM, MLSTM_V_DIM), F32),
                        pltpu.VMEM((MLSTM_HEADS, 1, MLSTM_QK_DIM), F32),
                        pltpu.VMEM((MLSTM_HEADS, 8, V7X_LANES), F32)],
        compiler_params=_cparams(("parallel", "arbitrary")),
        name="mlstm",
    )(z, z, z, z, gates, gate_bias, conv_w, conv_b.reshape(1, 2 * qw), mh_norm_g.reshape(1, vw))


def _outproj_kernel(a1_ref, a2_ref, w_ref, res_ref, o_ref):
    k1 = a1_ref.shape[1]
    acc = jnp.dot(a1_ref[...], w_ref[0:k1, :], preferred_element_type=F32)
    acc = acc + jnp.dot(a2_ref[...], w_ref[k1:, :], preferred_element_type=F32)
    o_ref[...] = res_ref[...] + acc


def out_projection(a1, a2, w, res, *, tm=MM_TM, tn=MM_TN):
    t, k1 = a1.shape
    k2 = a2.shape[1]
    n = w.shape[1]
    tm, tn = min(tm, t), min(tn, n)
    return pl.pallas_call(
        _outproj_kernel,
        grid=(t // tm, n // tn),
        in_specs=[pl.BlockSpec((tm, k1), lambda i, j: (i, 0)),
                  pl.BlockSpec((tm, k2), lambda i, j: (i, 0)),
                  pl.BlockSpec((k1 + k2, tn), lambda i, j: (0, j)),
                  pl.BlockSpec((tm, tn), lambda i, j: (i, j))],
        out_specs=pl.BlockSpec((tm, tn), lambda i, j: (i, j)),
        out_shape=jax.ShapeDtypeStruct((t, n), F32),
        compiler_params=_cparams(("parallel", "arbitrary")),
        name="out_projection",
    )(a1, a2, w, res)


def _pack_bf16_pairs(f):
    half = f.shape[1] // 2
    fb = f.astype(BF16).astype(F32)
    lo = pltpu.bitcast(fb[:, :half], U32) >> 16
    hi = pltpu.bitcast(fb[:, half:], U32) & jnp.uint32(0xFFFF0000)
    return lo | hi


def _unpack_bf16_pairs(w):
    lo = pltpu.bitcast(w << 16, F32).astype(BF16)
    hi = pltpu.bitcast(w & jnp.uint32(0xFFFF0000), F32).astype(BF16)
    return lo, hi


def _router_kernel(h_ref, g_ref, wr_ref, br_ref, fpk_ref, e_ref, gate_ref, rank_ref, cnt_ref, carry):
    i = pl.program_id(0)

    @pl.when(i == 0)
    def _():
        carry[...] = jnp.zeros_like(carry)

    f = _rmsnorm_body(h_ref[...], g_ref[...])
    fpk_ref[...] = _pack_bf16_pairs(f)
    logits = jnp.dot(f, wr_ref[...], preferred_element_type=F32,
                     precision=lax.Precision.HIGHEST) + br_ref[...]
    lane = lax.broadcasted_iota(I32, logits.shape, 1)
    cur = logits
    vals, idxs = [], []
    for _ in range(TOP_K):
        mx = jnp.max(cur, axis=-1, keepdims=True)
        idx = jnp.min(jnp.where(cur == mx, lane, V7X_LANES), axis=-1, keepdims=True)
        vals.append(mx)
        idxs.append(idx)
        cur = jnp.where(lane == idx, -jnp.inf, cur)
    exps = [jnp.exp(v - vals[0]) for v in vals]
    tot = exps[0] + exps[1] + exps[2] + exps[3]
    e_out = jnp.zeros(logits.shape, I32)
    g_out = jnp.zeros(logits.shape, F32)
    for k in range(TOP_K):
        e_out = jnp.where(lane == k, idxs[k], e_out)
        g_out = jnp.where(lane == k, exps[k] / tot, g_out)
    e_ref[...] = e_out
    gate_ref[...] = g_out

    tm = logits.shape[0]
    onehots = [lane == idxs[k] for k in range(TOP_K)]
    member = jnp.zeros(logits.shape, F32)
    for k in range(TOP_K):
        member = jnp.where(onehots[k], 1.0, member)
    row = lax.broadcasted_iota(I32, (tm, tm), 0)
    col = lax.broadcasted_iota(I32, (tm, tm), 1)
    stril = jnp.where(col < row, 1.0, 0.0).astype(BF16)
    before = carry[0:1, :] + jnp.dot(stril, member.astype(BF16), preferred_element_type=F32)
    r_out = jnp.zeros(logits.shape, I32)
    for k in range(TOP_K):
        rk = jnp.sum(jnp.where(onehots[k], before, 0.0), axis=-1, keepdims=True)
        r_out = jnp.where(lane == k, rk.astype(I32), r_out)
    rank_ref[...] = r_out
    total = carry[0:1, :] + jnp.sum(member, axis=0, keepdims=True)
    carry[...] = jnp.broadcast_to(total, carry.shape)
    cnt_ref[...] = jnp.broadcast_to(total, cnt_ref.shape)


def ffn_norm_router(h, g, w_router, b_router, *, tm=NORM_TM):
    t, d = h.shape
    nt = t // tm
    wr = jnp.zeros((d, V7X_LANES), F32).at[:, :N_EXPERTS].set(w_router)
    br = jnp.full((1, V7X_LANES), NEG_BIG, F32).at[0, :N_EXPERTS].set(b_router)
    rows = lambda i: (i, 0)
    const = lambda i: (0, 0)
    return pl.pallas_call(
        _router_kernel,
        grid=(nt,),
        in_specs=[pl.BlockSpec((tm, d), rows),
                  pl.BlockSpec((1, d), const),
                  pl.BlockSpec((d, V7X_LANES), const),
                  pl.BlockSpec((1, V7X_LANES), const)],
        out_specs=[pl.BlockSpec((tm, d // 2), rows),
                   pl.BlockSpec((tm, V7X_LANES), rows),
                   pl.BlockSpec((tm, V7X_LANES), rows),
                   pl.BlockSpec((tm, V7X_LANES), rows),
                   pl.BlockSpec((8, V7X_LANES), const)],
        out_shape=[jax.ShapeDtypeStruct((t, d // 2), U32),
                   jax.ShapeDtypeStruct((t, V7X_LANES), I32),
                   jax.ShapeDtypeStruct((t, V7X_LANES), F32),
                   jax.ShapeDtypeStruct((t, V7X_LANES), I32),
                   jax.ShapeDtypeStruct((8, V7X_LANES), F32)],
        scratch_shapes=[pltpu.VMEM((8, V7X_LANES), F32)],
        compiler_params=_cparams(("arbitrary",)),
        name="ffn_norm_router",
    )(h, g.reshape(1, d), wr, br)


def routing_tables(top_e, rank, counts, n_tokens):
    tk = n_tokens * TOP_K
    tm = EXPERT_TM
    n_blk = -(-(tk + N_EXPERTS * (tm - 1)) // tm)
    padded = (counts + tm - 1) // tm * tm
    pend = jnp.cumsum(padded)
    pstart = pend - padded
    pos = (pstart[top_e] + rank).reshape(tk).astype(I32)
    blk_start = jnp.arange(n_blk, dtype=I32) * tm
    blk_e = jnp.minimum(jnp.sum(blk_start[:, None] >= pend[None, :], axis=1), N_EXPERTS - 1).astype(I32)
    n_used = (pend[-1] // tm).astype(I32).reshape(1)
    is_last = jnp.any((blk_start[:, None] + tm == pend[None, :]) & (padded[None, :] > 0), axis=1)
    zero_blk = (is_last | (blk_start >= pend[-1])).astype(I32)
    return pos, zero_blk, (blk_e, n_used[0], pstart // tm, padded // tm, n_blk)


SCHED_ACTIVE = 1
SCHED_FIRST = 2


def resident_schedule(layout, n_col):
    blk_e, n_used, first_blk, n_blks, n_blk = layout
    s = jnp.arange(n_blk * n_col, dtype=I32)
    n_active = n_col * n_used
    active = s < n_active
    sc = jnp.minimum(s, jnp.maximum(n_active - 1, 0))
    e = blk_e[jnp.minimum(sc // n_col, n_blk - 1)]
    b0 = first_blk[e]
    nb = jnp.maximum(n_blks[e], 1)
    local = sc - n_col * b0
    col = local // nb
    j = local % nb
    blk = b0 + j
    out_blk = jnp.where(active, blk, s // n_col)
    out_col = jnp.where(active, col, s % n_col)
    flags = jnp.where(active, SCHED_ACTIVE + SCHED_FIRST * (j == 0), 0)
    as_i32 = lambda a: a.astype(I32)
    return tuple(map(as_i32, (blk, out_blk, out_col, e, col, flags)))


def _dispatch_kernel(pos_ref, zero_ref, f_ref, xs_hbm, zbuf, sem, zsem, *, rows, n_blk):
    i = pl.program_id(0)
    tm = f_ref.shape[0]

    def zero_copy(b):
        return pltpu.make_async_copy(zbuf, xs_hbm.at[pl.ds(b * rows, rows)], zsem)

    @pl.when(i == 0)
    def _():
        zbuf[...] = jnp.zeros_like(zbuf)

        def issue(b, c):
            @pl.when(zero_ref[b] != 0)
            def _():
                zero_copy(b).start()
            return c

        def drain(b, c):
            @pl.when(zero_ref[b] != 0)
            def _():
                zero_copy(b).wait()
            return c

        lax.fori_loop(0, n_blk, issue, 0)
        lax.fori_loop(0, n_blk, drain, 0)

    def row_copy(t, k):
        row = pos_ref[(i * tm + t) * TOP_K + k]
        return pltpu.make_async_copy(f_ref.at[pl.ds(t, 1)], xs_hbm.at[pl.ds(row, 1)], sem)

    def issue_rows(t, c):
        for k in range(TOP_K):
            row_copy(t, k).start()
        return c

    def drain_rows(t, c):
        for k in range(TOP_K):
            row_copy(t, k).wait()
        return c

    lax.fori_loop(0, tm, issue_rows, 0, unroll=DMA_ISSUE_UNROLL)
    lax.fori_loop(0, tm, drain_rows, 0, unroll=DMA_ISSUE_UNROLL)


def dispatch(fpk, pos, zero_blk, *, rows=EXPERT_TM, tm=NORM_TM):
    t, half = fpk.shape
    n_blk = zero_blk.shape[0]
    grid_spec = pltpu.PrefetchScalarGridSpec(
        num_scalar_prefetch=2,
        grid=(t // tm,),
        in_specs=[pl.BlockSpec((tm, half), lambda i, p, z: (i, 0))],
        out_specs=pl.BlockSpec(memory_space=pl.ANY),
        scratch_shapes=[pltpu.VMEM((rows, half), fpk.dtype), pltpu.SemaphoreType.DMA(()),
                        pltpu.SemaphoreType.DMA(())],
    )
    return pl.pallas_call(
        functools.partial(_dispatch_kernel, rows=rows, n_blk=n_blk),
        grid_spec=grid_spec,
        out_shape=jax.ShapeDtypeStruct((n_blk * rows, half), fpk.dtype),
        compiler_params=_cparams(("arbitrary",), has_side_effects=True),
        name="dispatch",
    )(pos, zero_blk, fpk)


def _expert_up_kernel(xb_ref, ob_ref, oc_ref, se_ref, sc_ref, flag_ref, xs_ref, wg_ref, wu_ref, bg_ref, bu_ref,
                      act_ref, wgb, wub):
    s = pl.program_id(0)
    half = xs_ref.shape[1]
    flag = flag_ref[s]

    @pl.when((flag & SCHED_FIRST) != 0)
    def _():
        wgb[...] = wg_ref[...].astype(BF16)
        wub[...] = wu_ref[...].astype(BF16)

    @pl.when((flag & SCHED_ACTIVE) != 0)
    def _():
        lo, hi = _unpack_bf16_pairs(xs_ref[...])
        glu = (jnp.dot(lo, wgb[0:half, :], preferred_element_type=F32)
               + jnp.dot(hi, wgb[half:, :], preferred_element_type=F32) + bg_ref[...])
        up = (jnp.dot(lo, wub[0:half, :], preferred_element_type=F32)
              + jnp.dot(hi, wub[half:, :], preferred_element_type=F32) + bu_ref[...])
        glu = jnp.minimum(glu, SWIGLU_LIMIT)
        up = jnp.clip(up, -SWIGLU_LIMIT, SWIGLU_LIMIT)
        act_ref[...] = ((up + 1.0) * (glu * jax.nn.sigmoid(SWIGLU_ALPHA * glu))).astype(act_ref.dtype)

    @pl.when((flag & SCHED_ACTIVE) == 0)
    def _():
        act_ref[...] = jnp.zeros_like(act_ref)


def expert_up(xs, w_gate_up, b_gate_up, sched, *, tm=EXPERT_TM, tf=EXPERT_TF):
    n_rows, half = xs.shape
    d = 2 * half
    ff = EXPERT_FF
    nf = ff // tf
    n_steps = sched[0].shape[0]
    grid_spec = pltpu.PrefetchScalarGridSpec(
        num_scalar_prefetch=6,
        grid=(n_steps,),
        in_specs=[pl.BlockSpec((tm, half), lambda s, xb, ob, oc, se, sc, fl: (xb[s], 0)),
                  pl.BlockSpec((None, d, tf), lambda s, xb, ob, oc, se, sc, fl: (se[s], 0, sc[s])),
                  pl.BlockSpec((None, d, tf), lambda s, xb, ob, oc, se, sc, fl: (se[s], 0, nf + sc[s])),
                  pl.BlockSpec((None, 1, tf), lambda s, xb, ob, oc, se, sc, fl: (se[s], 0, sc[s])),
                  pl.BlockSpec((None, 1, tf), lambda s, xb, ob, oc, se, sc, fl: (se[s], 0, nf + sc[s]))],
        out_specs=pl.BlockSpec((tm, tf), lambda s, xb, ob, oc, se, sc, fl: (ob[s], oc[s])),
        scratch_shapes=[pltpu.VMEM((d, tf), BF16), pltpu.VMEM((d, tf), BF16)],
    )
    return pl.pallas_call(
        _expert_up_kernel,
        grid_spec=grid_spec,
        out_shape=jax.ShapeDtypeStruct((n_rows, ff), BF16),
        compiler_params=_cparams(("arbitrary",)),
        name="expert_up",
    )(*sched, xs, w_gate_up, w_gate_up, b_gate_up, b_gate_up)


def _expert_down_kernel(xb_ref, ob_ref, oc_ref, se_ref, sc_ref, flag_ref, act_ref, wd_ref, bd_ref, y_ref, wdb):
    s = pl.program_id(0)
    flag = flag_ref[s]

    @pl.when((flag & SCHED_FIRST) != 0)
    def _():
        wdb[...] = wd_ref[...].astype(BF16)

    @pl.when((flag & SCHED_ACTIVE) != 0)
    def _():
        y_ref[...] = jnp.dot(act_ref[...], wdb[...], preferred_element_type=F32) + bd_ref[...]

    @pl.when((flag & SCHED_ACTIVE) == 0)
    def _():
        y_ref[...] = jnp.zeros_like(y_ref)


def expert_down(act, w_down, b_down, sched, *, tm=EXPERT_TM, tn=EXPERT_TN):
    n_rows, ff = act.shape
    d = w_down.shape[2]
    n_steps = sched[0].shape[0]
    grid_spec = pltpu.PrefetchScalarGridSpec(
        num_scalar_prefetch=6,
        grid=(n_steps,),
        in_specs=[pl.BlockSpec((tm, ff), lambda s, xb, ob, oc, se, sc, fl: (xb[s], 0)),
                  pl.BlockSpec((None, ff, tn), lambda s, xb, ob, oc, se, sc, fl: (se[s], 0, sc[s])),
                  pl.BlockSpec((None, 1, tn), lambda s, xb, ob, oc, se, sc, fl: (se[s], 0, sc[s]))],
        out_specs=pl.BlockSpec((tm, tn), lambda s, xb, ob, oc, se, sc, fl: (ob[s], oc[s])),
        scratch_shapes=[pltpu.VMEM((ff, tn), BF16)],
    )
    return pl.pallas_call(
        _expert_down_kernel,
        grid_spec=grid_spec,
        out_shape=jax.ShapeDtypeStruct((n_rows, d), F32),
        compiler_params=_cparams(("arbitrary",)),
        name="expert_down",
    )(*sched, act, w_down, b_down)


def _combine_kernel(pos_ref, y_hbm, h_ref, gate_ref, g_ref, o_ref, a_ref, buf, sem, *, tq, n_steps):
    i = pl.program_id(0)
    slot = i % 2

    def row_copy(step, sl, t, k):
        row = pos_ref[(step * tq + t) * TOP_K + k]
        return pltpu.make_async_copy(y_hbm.at[pl.ds(row, 1)], buf.at[sl, k, pl.ds(t, 1)], sem.at[sl])

    def issue(step, sl):
        def body(t, carry):
            for k in range(TOP_K):
                row_copy(step, sl, t, k).start()
            return carry

        lax.fori_loop(0, tq, body, 0, unroll=DMA_ISSUE_UNROLL)

    @pl.when(i == 0)
    def _():
        issue(0, 0)

    @pl.when(i + 1 < n_steps)
    def _():
        issue(i + 1, 1 - slot)

    def drain(t, carry):
        for k in range(TOP_K):
            row_copy(i, slot, t, k).wait()
        return carry

    lax.fori_loop(0, tq, drain, 0, unroll=DMA_ISSUE_UNROLL)
    gates = gate_ref[...]
    moe = ((gates[:, 0:1] * buf[slot, 0] + gates[:, 1:2] * buf[slot, 1])
           + (gates[:, 2:3] * buf[slot, 2] + gates[:, 3:4] * buf[slot, 3]))
    out = h_ref[...] + moe
    o_ref[...] = out
    a_ref[...] = _rmsnorm_body(out, g_ref[...]).astype(a_ref.dtype)


def combine(y, pos, h, gates, g_next, *, tq=COMBINE_TQ):
    t, d = h.shape
    rows = lambda i, p: (i, 0)
    grid_spec = pltpu.PrefetchScalarGridSpec(
        num_scalar_prefetch=1,
        grid=(t // tq,),
        in_specs=[pl.BlockSpec(memory_space=pl.ANY),
                  pl.BlockSpec((tq, d), rows),
                  pl.BlockSpec((tq, V7X_LANES), rows),
                  pl.BlockSpec((1, d), lambda i, p: (0, 0))],
        out_specs=[pl.BlockSpec((tq, d), rows), pl.BlockSpec((tq, d), rows)],
        scratch_shapes=[pltpu.VMEM((2, TOP_K, tq, d), F32), pltpu.SemaphoreType.DMA((2,))],
    )
    return pl.pallas_call(
        functools.partial(_combine_kernel, tq=tq, n_steps=t // tq),
        grid_spec=grid_spec,
        out_shape=[jax.ShapeDtypeStruct((t, d), F32), jax.ShapeDtypeStruct((t, d), BF16)],
        compiler_params=_cparams(("arbitrary",)),
        name="combine",
    )(pos, y, h, gates, g_next.reshape(1, d))


def _ple_kernel(a_ref, w_ref, p_ref, wp_ref, h_ref, o_ref):
    gate = jax.nn.sigmoid(jnp.dot(a_ref[...], w_ref[...], preferred_element_type=F32))
    emb = jnp.dot(p_ref[...], wp_ref[...], preferred_element_type=F32)
    o_ref[...] = h_ref[...] + gate * emb


def ple(a, w_gate, p, w_proj, h, *, tm=MM_TM // 2, tn=MM_TN):
    t, k = a.shape
    n = w_gate.shape[1]
    pk = p.shape[1]
    tm, tn = min(tm, t), min(tn, n)
    return pl.pallas_call(
        _ple_kernel,
        grid=(t // tm, n // tn),
        in_specs=[pl.BlockSpec((tm, k), lambda i, j: (i, 0)),
                  pl.BlockSpec((k, tn), lambda i, j: (0, j)),
                  pl.BlockSpec((tm, pk), lambda i, j: (i, 0)),
                  pl.BlockSpec((pk, tn), lambda i, j: (0, j)),
                  pl.BlockSpec((tm, tn), lambda i, j: (i, j))],
        out_specs=pl.BlockSpec((tm, tn), lambda i, j: (i, j)),
        out_shape=jax.ShapeDtypeStruct((t, n), F32),
        compiler_params=_cparams(("parallel", "arbitrary")),
        name="ple",
    )(a, w_gate, p, w_proj, h)


def kernel(x, p, positions, attn_norm_g, w_in, conv_w, conv_b, b_igate, b_fgate, mh_norm_g, w_out,
           ffn_norm_g, w_router, b_router, w_gate_up, b_gate_up, w_down, b_down, ple_norm_g, w_ple_gate,
           w_ple_proj, final_norm_g):
    batch, seq, d = x.shape
    depth = p.shape[0]
    t = batch * seq
    n_main = 3 * ATTN_WIDTH + 2 * MLSTM_QK_WIDTH + 2 * MLSTM_V_WIDTH
    nh = MLSTM_HEADS
    h = x.reshape(t, d)
    cos, sin = rope_tables(positions.reshape(t, 1))
    for i in range(depth):
        a = rmsnorm(h, attn_norm_g[i], BF16)
        w_gates = jnp.zeros((d, V7X_LANES), BF16).at[:, :2 * nh].set(w_in[i][:, n_main:].astype(BF16))
        z, gates = in_projection(a, w_in[i].astype(BF16), n_main, w_gates, cos, sin)
        attn = dilated_attention(z, batch, seq)
        gate_bias = jnp.zeros((1, V7X_LANES), F32).at[0, :nh].set(b_igate[i]).at[0, nh:2 * nh].set(b_fgate[i])
        ml = mlstm(z, gates, gate_bias, conv_w[i], conv_b[i], mh_norm_g[i], batch, seq)
        h = out_projection(attn, ml, w_out[i].astype(BF16), h)
        fpk, top_e, top_g, rank, counts = ffn_norm_router(h, ffn_norm_g[i], w_router[i], b_router[i])
        pos, zero_blk, layout = routing_tables(top_e[:, :TOP_K], rank[:, :TOP_K],
                                               counts[0, :N_EXPERTS].astype(I32), t)
        xs = dispatch(fpk, pos, zero_blk)
        act = expert_up(xs, w_gate_up[i], b_gate_up[i].reshape(N_EXPERTS, 1, 2 * EXPERT_FF),
                        resident_schedule(layout, EXPERT_FF // EXPERT_TF))
        tn = min(EXPERT_TN, d)
        y = expert_down(act, w_down[i], b_down[i].reshape(N_EXPERTS, 1, d),
                        resident_schedule(layout, d // tn), tn=tn)
        h, a2 = combine(y, pos, h, top_g, ple_norm_g[i])
        h = ple(a2, w_ple_gate[i].astype(BF16), p[i].reshape(t, -1).astype(BF16), w_ple_proj[i].astype(BF16), h)
    return rmsnorm(h, final_norm_g, x.dtype).reshape(batch, seq, d)
```

```python
import functools

import jax
import jax.numpy as jnp
from jax import lax
from jax.experimental import pallas as pl
from jax.experimental.pallas import tpu as pltpu

F32 = jnp.float32
BF16 = jnp.bfloat16
U32 = jnp.uint32
I32 = jnp.int32

V7X_VMEM_BYTES = 64 * 1024 * 1024
V7X_LANES = 128
VMEM_LIMIT = V7X_VMEM_BYTES - 8 * 1024 * 1024

ATTN_HEADS = 16
HEAD_DIM = 128
ATTN_WIDTH = ATTN_HEADS * HEAD_DIM
ATTN_BLOCK = 128
DILATIONS = (1, 4, 16)
ATTN_SUPER = ATTN_BLOCK * max(DILATIONS)
ROPE_THETA = 10000.0
MLSTM_HEADS = 4
MLSTM_QK_DIM = 256
MLSTM_V_DIM = 512
MLSTM_QK_WIDTH = MLSTM_HEADS * MLSTM_QK_DIM
MLSTM_V_WIDTH = MLSTM_HEADS * MLSTM_V_DIM
MLSTM_CHUNK = 256
CONV_WIDTH = 4
CONV_PAD = 8
N_EXPERTS = 32
TOP_K = 4
EXPERT_FF = 1536
SWIGLU_LIMIT = 7.0
SWIGLU_ALPHA = 1.702
NORM_EPS = 1e-6
NEG_BIG = -1e30

MM_TM = 1024
MM_TN = 1024
NORM_TM = 256
EXPERT_TM = 512
EXPERT_TF = 512
EXPERT_TN = 2048
COMBINE_TQ = 128
DMA_ISSUE_UNROLL = 4


def _cparams(sem, **kw):
    return pltpu.CompilerParams(dimension_semantics=sem, vmem_limit_bytes=VMEM_LIMIT, **kw)


def _rmsnorm_body(x, g):
    r = lax.rsqrt(jnp.mean(x * x, axis=-1, keepdims=True) + NORM_EPS)
    return x * r * g


def _rmsnorm_kernel(x_ref, g_ref, o_ref):
    o_ref[...] = _rmsnorm_body(x_ref[...], g_ref[...]).astype(o_ref.dtype)


def rmsnorm(x, g, out_dtype):
    t, d = x.shape
    return pl.pallas_call(
        _rmsnorm_kernel,
        grid=(t // NORM_TM,),
        in_specs=[pl.BlockSpec((NORM_TM, d), lambda i: (i, 0)),
                  pl.BlockSpec((1, d), lambda i: (0, 0))],
        out_specs=pl.BlockSpec((NORM_TM, d), lambda i: (i, 0)),
        out_shape=jax.ShapeDtypeStruct((t, d), out_dtype),
        compiler_params=_cparams(("parallel",)),
        name="rmsnorm",
    )(x, g.reshape(1, d))


def _rope_table_kernel(pos_ref, freq_ref, sign_ref, cos_ref, sin_ref):
    ang = pos_ref[...].astype(F32) * freq_ref[...]
    cos_ref[...] = jnp.cos(ang)
    sin_ref[...] = jnp.sin(ang) * sign_ref[...]


def rope_tables(positions_col):
    t = positions_col.shape[0]
    half = HEAD_DIM // 2
    inv_freq = jnp.power(ROPE_THETA, -jnp.arange(half, dtype=F32) / half)
    freq = jnp.concatenate([inv_freq, inv_freq]).reshape(1, HEAD_DIM)
    sign = jnp.concatenate([-jnp.ones((half,), F32), jnp.ones((half,), F32)]).reshape(1, HEAD_DIM)
    tm = 512
    return pl.pallas_call(
        _rope_table_kernel,
        grid=(t // tm,),
        in_specs=[pl.BlockSpec((tm, 1), lambda i: (i, 0)),
                  pl.BlockSpec((1, HEAD_DIM), lambda i: (0, 0)),
                  pl.BlockSpec((1, HEAD_DIM), lambda i: (0, 0))],
        out_specs=[pl.BlockSpec((tm, HEAD_DIM), lambda i: (i, 0))] * 2,
        out_shape=[jax.ShapeDtypeStruct((t, HEAD_DIM), F32)] * 2,
        compiler_params=_cparams(("parallel",)),
        name="rope_tables",
    )(positions_col, freq, sign)


def _inproj_kernel(a_ref, w_ref, wg_ref, cos_ref, sin_ref, z_ref, gate_ref, *, n_q_tiles, n_k_tiles,
                   q_scale):
    j = pl.program_id(1)

    @pl.when(j == 0)
    def _():
        gate_ref[...] = jnp.dot(a_ref[...], wg_ref[...], preferred_element_type=F32)

    def rope(scale):
        cos = cos_ref[...]
        sin = sin_ref[...]
        pair = 2 * HEAD_DIM
        for c0 in range(0, z_ref.shape[1], pair):
            acc = jnp.dot(a_ref[...], w_ref[:, c0:c0 + pair], preferred_element_type=F32)
            for c in range(c0, c0 + pair, HEAD_DIM):
                t = acc[:, c - c0:c - c0 + HEAD_DIM]
                r = pltpu.roll(t, HEAD_DIM // 2, axis=1)
                z_ref[:, c:c + HEAD_DIM] = ((t * cos + r * sin) * scale).astype(z_ref.dtype)

    @pl.when(j < n_q_tiles)
    def _():
        rope(q_scale)

    @pl.when((j >= n_q_tiles) & (j < n_q_tiles + n_k_tiles))
    def _():
        rope(1.0)

    @pl.when(j >= n_q_tiles + n_k_tiles)
    def _():
        z_ref[...] = jnp.dot(a_ref[...], w_ref[...], preferred_element_type=F32).astype(z_ref.dtype)


def in_projection(a, w, n, w_gates, cos, sin, *, tm=MM_TM, tn=MM_TN):
    t, k = a.shape
    tm, tn = min(tm, t), min(tn, n)
    kern = functools.partial(_inproj_kernel, n_q_tiles=ATTN_WIDTH // tn, n_k_tiles=ATTN_WIDTH // tn,
                             q_scale=HEAD_DIM ** -0.5)
    return pl.pallas_call(
        kern,
        grid=(t // tm, n // tn),
        in_specs=[pl.BlockSpec((tm, k), lambda i, j: (i, 0)),
                  pl.BlockSpec((k, tn), lambda i, j: (0, j)),
                  pl.BlockSpec((k, V7X_LANES), lambda i, j: (0, 0)),
                  pl.BlockSpec((tm, HEAD_DIM), lambda i, j: (i, 0)),
                  pl.BlockSpec((tm, HEAD_DIM), lambda i, j: (i, 0))],
        out_specs=[pl.BlockSpec((tm, tn), lambda i, j: (i, j)),
                   pl.BlockSpec((tm, V7X_LANES), lambda i, j: (i, 0))],
        out_shape=[jax.ShapeDtypeStruct((t, n), BF16),
                   jax.ShapeDtypeStruct((t, V7X_LANES), F32)],
        compiler_params=_cparams(("parallel", "arbitrary")),
        name="in_projection",
    )(a, w, w_gates, cos, sin)


def _band_block(qb, kb, vb, bias):
    s = lax.dot_general(qb, kb, (((1,), (1,)), ((), ())), preferred_element_type=F32) + bias
    m = jnp.max(s, axis=-1, keepdims=True)
    p = jnp.exp(s - m)
    l = jnp.sum(p, axis=-1, keepdims=True)
    o = jnp.dot(p.astype(BF16), vb, preferred_element_type=F32) / l
    return o, m + jnp.log(l)


def _attn_kernel(q_ref, kp_ref, kc_ref, vp_ref, vc_ref, o_ref, qf, kf, vf, ob, lb):
    sb = ATTN_SUPER
    blk = ATTN_BLOCK
    n = pl.program_id(2)
    qf[...] = q_ref[...].astype(F32)
    kf[0:sb, :] = kp_ref[...].astype(F32)
    kf[sb:2 * sb, :] = kc_ref[...].astype(F32)
    vf[0:sb, :] = vp_ref[...].astype(F32)
    vf[sb:2 * sb, :] = vc_ref[...].astype(F32)

    qi = lax.broadcasted_iota(I32, (blk, 2 * blk), 0)
    kj = lax.broadcasted_iota(I32, (blk, 2 * blk), 1)
    dist = blk + qi - kj
    band = (dist >= 0) & (dist <= blk)
    bias_in = jnp.where(band, 0.0, NEG_BIG).astype(F32)
    prev_ok = jnp.where(n > 0, 0.0, NEG_BIG).astype(F32)
    bias_edge = bias_in + jnp.where(kj < blk, prev_ok, 0.0)

    for p, d in enumerate(DILATIONS):
        per_class = sb // (d * blk)
        for r in range(d):
            for nb in range(per_class):
                q0 = r + d * blk * nb
                k0 = sb + q0 - d * blk
                if d == 1:
                    qb = qf[pl.ds(q0, blk), :]
                    kb = kf[pl.ds(k0, 2 * blk), :]
                    vb = vf[pl.ds(k0, 2 * blk), :]
                else:
                    qb = qf[pl.ds(q0, blk, stride=d), :]
                    kb = kf[pl.ds(k0, 2 * blk, stride=d), :]
                    vb = vf[pl.ds(k0, 2 * blk, stride=d), :]
                bias = bias_edge if nb == 0 else bias_in
                o, lse = _band_block(qb.astype(BF16), kb.astype(BF16), vb.astype(BF16), bias)
                lse_b = jnp.broadcast_to(lse, (blk, HEAD_DIM))
                if d == 1:
                    ob[p, pl.ds(q0, blk), :] = o
                    lb[p, pl.ds(q0, blk), :] = lse_b
                else:
                    ob[p, pl.ds(q0, blk, stride=d), :] = o
                    lb[p, pl.ds(q0, blk, stride=d), :] = lse_b

    l0, l1, l2 = lb[0], lb[1], lb[2]
    mx = jnp.maximum(jnp.maximum(l0, l1), l2)
    w0, w1, w2 = jnp.exp(l0 - mx), jnp.exp(l1 - mx), jnp.exp(l2 - mx)
    out = (w0 * ob[0] + w1 * ob[1] + w2 * ob[2]) / (w0 + w1 + w2)
    o_ref[...] = out.astype(o_ref.dtype)


def dilated_attention(z, batch, seq):
    t = z.shape[0]
    sb = ATTN_SUPER
    nsb = seq // sb
    h = ATTN_HEADS
    cur = lambda off: (lambda b, hh, n: (b * nsb + n, off + hh))
    prev = lambda off: (lambda b, hh, n: (b * nsb + jnp.maximum(n - 1, 0), off + hh))
    spec = lambda im: pl.BlockSpec((sb, HEAD_DIM), im)
    return pl.pallas_call(
        _attn_kernel,
        grid=(batch, h, nsb),
        in_specs=[spec(cur(0)), spec(prev(h)), spec(cur(h)), spec(prev(2 * h)), spec(cur(2 * h))],
        out_specs=spec(cur(0)),
        out_shape=jax.ShapeDtypeStruct((t, ATTN_WIDTH), BF16),
        scratch_shapes=[pltpu.VMEM((sb, HEAD_DIM), F32),
                        pltpu.VMEM((2 * sb, HEAD_DIM), F32),
                        pltpu.VMEM((2 * sb, HEAD_DIM), F32),
                        pltpu.VMEM((len(DILATIONS), sb, HEAD_DIM), F32),
                        pltpu.VMEM((len(DILATIONS), sb, HEAD_DIM), F32)],
        compiler_params=_cparams(("parallel", "parallel", "arbitrary")),
        name="dilated_attention",
    )(z, z, z, z, z)


def _log_sigmoid(x):
    return jnp.minimum(x, 0.0) - jnp.log(1.0 + jnp.exp(-jnp.abs(x)))


def _mlstm_kernel(q_ref, k_ref, v_ref, om_ref, g_ref, gb_ref, cw_ref, cb_ref, mhg_ref, o_ref,
                  xbuf, c_st, n_st, m_st):
    L = MLSTM_CHUNK
    dqk, dv, nh = MLSTM_QK_DIM, MLSTM_V_DIM, MLSTM_HEADS
    qw = nh * dqk
    c = pl.program_id(1)

    @pl.when(c == 0)
    def _():
        xbuf[0:CONV_PAD, :] = jnp.zeros((CONV_PAD, 2 * qw), F32)
        c_st[...] = jnp.zeros_like(c_st)
        n_st[...] = jnp.zeros_like(n_st)
        m_st[...] = jnp.zeros_like(m_st)

    @pl.when(c > 0)
    def _():
        xbuf[0:CONV_PAD, :] = xbuf[L:L + CONV_PAD, :]

    xbuf[CONV_PAD:CONV_PAD + L, 0:qw] = q_ref[...].astype(F32)
    xbuf[CONV_PAD:CONV_PAD + L, qw:2 * qw] = k_ref[...].astype(F32)
    y = cb_ref[...]
    for j in range(CONV_WIDTH):
        y = y + cw_ref[j:j + 1, :] * xbuf[pl.ds(CONV_PAD - CONV_WIDTH + 1 + j, L), :]
    qk = y * jax.nn.sigmoid(y)

    pre = g_ref[...] + gb_ref[...]
    lf = _log_sigmoid(pre)
    row = lax.broadcasted_iota(I32, (L, L), 0)
    col = lax.broadcasted_iota(I32, (L, L), 1)
    causal = col <= row
    tril = jnp.where(causal, 1.0, 0.0).astype(F32)
    bcum = jnp.dot(tril, lf, preferred_element_type=F32, precision=lax.Precision.HIGHEST)
    pre_t = pre.T
    bcum_t = bcum.T

    for h in range(nh):
        i_c = pre[:, h:h + 1]
        b_c = bcum[:, nh + h:nh + h + 1]
        i_r = pre_t[h:h + 1, :]
        b_r = bcum_t[nh + h:nh + h + 1, :]
        g = bcum[L - 1:L, nh + h:nh + h + 1]
        m = m_st[h, 0:1, 0:1]
        qq = (qk[:, h * dqk:(h + 1) * dqk] * (dqk ** -0.5))
        kk = qk[:, qw + h * dqk:qw + (h + 1) * dqk]
        vv = v_ref[:, h * dv:(h + 1) * dv]
        qb = qq.astype(BF16)

        log_d = jnp.where(causal, b_c - b_r + i_r, NEG_BIG)
        inter = b_c + m
        m_t = jnp.maximum(inter, jnp.max(log_d, axis=-1, keepdims=True))
        w_inter = jnp.exp(inter - m_t)
        s = lax.dot_general(qb, kk.astype(BF16), (((1,), (1,)), ((), ())),
                            preferred_element_type=F32) * jnp.exp(log_d - m_t)
        num = (w_inter * jnp.dot(qb, c_st[h].astype(BF16), preferred_element_type=F32)
               + jnp.dot(s.astype(BF16), vv, preferred_element_type=F32))
        den = (w_inter * jnp.sum(qq * n_st[h], axis=-1, keepdims=True)
               + jnp.sum(s, axis=-1, keepdims=True))
        hh = num / jnp.maximum(jnp.abs(den), jnp.exp(-m_t))

        log_w = g - b_c + i_c
        m_new = jnp.maximum(g + m, jnp.max(log_w, axis=0, keepdims=True))
        decay = jnp.exp(g + m - m_new)
        wk = kk * jnp.exp(log_w - m_new)
        c_st[h] = decay * c_st[h] + jnp.dot(wk.T.astype(BF16), vv, preferred_element_type=F32)
        n_st[h] = decay * n_st[h] + jnp.sum(wk, axis=0, keepdims=True)
        m_st[h] = jnp.broadcast_to(m_new, m_st.shape[1:])

        hn = hh * lax.rsqrt(jnp.mean(hh * hh, axis=-1, keepdims=True) + NORM_EPS)
        hn = hn * mhg_ref[:, h * dv:(h + 1) * dv]
        og = jax.nn.sigmoid(om_ref[:, h * dv:(h + 1) * dv].astype(F32))
        o_ref[:, h * dv:(h + 1) * dv] = (og * hn).astype(o_ref.dtype)


def mlstm(z, gates, gate_bias, conv_w, conv_b, mh_norm_g, batch, seq):
    t = z.shape[0]
    L = MLSTM_CHUNK
    nc = seq // L
    qw, vw = MLSTM_QK_WIDTH, MLSTM_V_WIDTH
    q_off = 3 * ATTN_WIDTH
    rows = lambda blk: (lambda b, c: (b * nc + c, blk))
    const = lambda b, c: (0, 0)
    return pl.pallas_call(
        _mlstm_kernel,
        grid=(batch, nc),
        in_specs=[pl.BlockSpec((L, qw), rows(q_off // qw)),
                  pl.BlockSpec((L, qw), rows(q_off // qw + 1)),
                  pl.BlockSpec((L, vw), rows((q_off + 2 * qw) // vw)),
                  pl.BlockSpec((L, vw), rows((q_off + 2 * qw) // vw + 1)),
                  pl.BlockSpec((L, V7X_LANES), rows(0)),
                  pl.BlockSpec((1, V7X_LANES), const),
                  pl.BlockSpec((CONV_WIDTH, 2 * qw), const),
                  pl.BlockSpec((1, 2 * qw), const),
                  pl.BlockSpec((1, vw), const)],
        out_specs=pl.BlockSpec((L, vw), rows(0)),
        out_shape=jax.ShapeDtypeStruct((t, vw), BF16),
        scratch_shapes=[pltpu.VMEM((L + CONV_PAD, 2 * qw), F32),
                        pltpu.VMEM((MLSTM_HEADS, MLSTM_QK_DIM, MLSTM_V_DIM), F32),
                        pltpu.VMEM((MLSTM_HEADS, 1, MLSTM_QK_DIM), F32),
                        pltpu.VMEM((MLSTM_HEADS, 8, V7X_LANES), F32)],
        compiler_params=_cparams(("parallel", "arbitrary")),
        name="mlstm",
    )(z, z, z, z, gates, gate_bias, conv_w, conv_b.reshape(1, 2 * qw), mh_norm_g.reshape(1, vw))


def _outproj_kernel(a1_ref, a2_ref, w_ref, res_ref, o_ref):
    k1 = a1_ref.shape[1]
    acc = jnp.dot(a1_ref[...], w_ref[0:k1, :], preferred_element_type=F32)
    acc = acc + jnp.dot(a2_ref[...], w_ref[k1:, :], preferred_element_type=F32)
    o_ref[...] = res_ref[...] + acc


def out_projection(a1, a2, w, res, *, tm=MM_TM, tn=MM_TN):
    t, k1 = a1.shape
    k2 = a2.shape[1]
    n = w.shape[1]
    tm, tn = min(tm, t), min(tn, n)
    return pl.pallas_call(
        _outproj_kernel,
        grid=(t // tm, n // tn),
        in_specs=[pl.BlockSpec((tm, k1), lambda i, j: (i, 0)),
                  pl.BlockSpec((tm, k2), lambda i, j: (i, 0)),
                  pl.BlockSpec((k1 + k2, tn), lambda i, j: (0, j)),
                  pl.BlockSpec((tm, tn), lambda i, j: (i, j))],
        out_specs=pl.BlockSpec((tm, tn), lambda i, j: (i, j)),
        out_shape=jax.ShapeDtypeStruct((t, n), F32),
        compiler_params=_cparams(("parallel", "arbitrary")),
        name="out_projection",
    )(a1, a2, w, res)


def _pack_bf16_pairs(f):
    half = f.shape[1] // 2
    fb = f.astype(BF16).astype(F32)
    lo = pltpu.bitcast(fb[:, :half], U32) >> 16
    hi = pltpu.bitcast(fb[:, half:], U32) & jnp.uint32(0xFFFF0000)
    return lo | hi


def _unpack_bf16_pairs(w):
    lo = pltpu.bitcast(w << 16, F32).astype(BF16)
    hi = pltpu.bitcast(w & jnp.uint32(0xFFFF0000), F32).astype(BF16)
    return lo, hi


def _router_kernel(h_ref, g_ref, wr_ref, br_ref, fpk_ref, e_ref, gate_ref, rank_ref, cnt_ref, carry):
    i = pl.program_id(0)

    @pl.when(i == 0)
    def _():
        carry[...] = jnp.zeros_like(carry)

    f = _rmsnorm_body(h_ref[...], g_ref[...])
    fpk_ref[...] = _pack_bf16_pairs(f)
    logits = jnp.dot(f, wr_ref[...], preferred_element_type=F32,
                     precision=lax.Precision.HIGHEST) + br_ref[...]
    lane = lax.broadcasted_iota(I32, logits.shape, 1)
    cur = logits
    vals, idxs = [], []
    for _ in range(TOP_K):
        mx = jnp.max(cur, axis=-1, keepdims=True)
        idx = jnp.min(jnp.where(cur == mx, lane, V7X_LANES), axis=-1, keepdims=True)
        vals.append(mx)
        idxs.append(idx)
        cur = jnp.where(lane == idx, -jnp.inf, cur)
    exps = [jnp.exp(v - vals[0]) for v in vals]
    tot = exps[0] + exps[1] + exps[2] + exps[3]
    e_out = jnp.zeros(logits.shape, I32)
    g_out = jnp.zeros(logits.shape, F32)
    for k in range(TOP_K):
        e_out = jnp.where(lane == k, idxs[k], e_out)
        g_out = jnp.where(lane == k, exps[k] / tot, g_out)
    e_ref[...] = e_out
    gate_ref[...] = g_out

    tm = logits.shape[0]
    onehots = [lane == idxs[k] for k in range(TOP_K)]
    member = jnp.zeros(logits.shape, F32)
    for k in range(TOP_K):
        member = jnp.where(onehots[k], 1.0, member)
    row = lax.broadcasted_iota(I32, (tm, tm), 0)
    col = lax.broadcasted_iota(I32, (tm, tm), 1)
    stril = jnp.where(col < row, 1.0, 0.0).astype(BF16)
    before = carry[0:1, :] + jnp.dot(stril, member.astype(BF16), preferred_element_type=F32)
    r_out = jnp.zeros(logits.shape, I32)
    for k in range(TOP_K):
        rk = jnp.sum(jnp.where(onehots[k], before, 0.0), axis=-1, keepdims=True)
        r_out = jnp.where(lane == k, rk.astype(I32), r_out)
    rank_ref[...] = r_out
    total = carry[0:1, :] + jnp.sum(member, axis=0, keepdims=True)
    carry[...] = jnp.broadcast_to(total, carry.shape)
    cnt_ref[...] = jnp.broadcast_to(total, cnt_ref.shape)


def ffn_norm_router(h, g, w_router, b_router, *, tm=NORM_TM):
    t, d = h.shape
    nt = t // tm
    wr = jnp.zeros((d, V7X_LANES), F32).at[:, :N_EXPERTS].set(w_router)
    br = jnp.full((1, V7X_LANES), NEG_BIG, F32).at[0, :N_EXPERTS].set(b_router)
    rows = lambda i: (i, 0)
    const = lambda i: (0, 0)
    return pl.pallas_call(
        _router_kernel,
        grid=(nt,),
        in_specs=[pl.BlockSpec((tm, d), rows),
                  pl.BlockSpec((1, d), const),
                  pl.BlockSpec((d, V7X_LANES), const),
                  pl.BlockSpec((1, V7X_LANES), const)],
        out_specs=[pl.BlockSpec((tm, d // 2), rows),
                   pl.BlockSpec((tm, V7X_LANES), rows),
                   pl.BlockSpec((tm, V7X_LANES), rows),
                   pl.BlockSpec((tm, V7X_LANES), rows),
                   pl.BlockSpec((8, V7X_LANES), const)],
        out_shape=[jax.ShapeDtypeStruct((t, d // 2), U32),
                   jax.ShapeDtypeStruct((t, V7X_LANES), I32),
                   jax.ShapeDtypeStruct((t, V7X_LANES), F32),
                   jax.ShapeDtypeStruct((t, V7X_LANES), I32),
                   jax.ShapeDtypeStruct((8, V7X_LANES), F32)],
        scratch_shapes=[pltpu.VMEM((8, V7X_LANES), F32)],
        compiler_params=_cparams(("arbitrary",)),
        name="ffn_norm_router",
    )(h, g.reshape(1, d), wr, br)


def routing_tables(top_e, rank, counts, n_tokens):
    tk = n_tokens * TOP_K
    tm = EXPERT_TM
    n_blk = -(-(tk + N_EXPERTS * (tm - 1)) // tm)
    padded = (counts + tm - 1) // tm * tm
    pend = jnp.cumsum(padded)
    pstart = pend - padded
    pos = (pstart[top_e] + rank).reshape(tk).astype(I32)
    blk_start = jnp.arange(n_blk, dtype=I32) * tm
    blk_e = jnp.minimum(jnp.sum(blk_start[:, None] >= pend[None, :], axis=1), N_EXPERTS - 1).astype(I32)
    n_used = (pend[-1] // tm).astype(I32).reshape(1)
    is_last = jnp.any((blk_start[:, None] + tm == pend[None, :]) & (padded[None, :] > 0), axis=1)
    zero_blk = (is_last | (blk_start >= pend[-1])).astype(I32)
    return pos, zero_blk, (blk_e, n_used[0], pstart // tm, padded // tm, n_blk)


SCHED_ACTIVE = 1
SCHED_FIRST = 2
SCHED_PREFETCH = 4


def resident_schedule(layout, n_col):
    blk_e, n_used, first_blk, n_blks, n_blk = layout
    s = jnp.arange(n_blk * n_col, dtype=I32)
    n_active = n_col * n_used
    active = s < n_active
    sc = jnp.minimum(s, jnp.maximum(n_active - 1, 0))
    e = blk_e[jnp.minimum(sc // n_col, n_blk - 1)]
    b0 = first_blk[e]
    nb = jnp.maximum(n_blks[e], 1)
    local = sc - n_col * b0
    col = local // nb
    j = local % nb
    blk = b0 + j
    out_blk = jnp.where(active, blk, s // n_col)
    out_col = jnp.where(active, col, s % n_col)
    nxt = sc - j + nb
    has_next = nxt < n_active
    nxt = jnp.minimum(nxt, jnp.maximum(n_active - 1, 0))
    flags = jnp.where(active, SCHED_ACTIVE + (j == 0) * (SCHED_FIRST + SCHED_PREFETCH * has_next), 0)
    as_i32 = lambda a: a.astype(I32)
    return tuple(map(as_i32, (blk, out_blk, out_col, e, col, flags, e[nxt], col[nxt])))


def _dispatch_kernel(pos_ref, zero_ref, f_ref, xs_hbm, zbuf, sem, zsem, *, rows, n_blk):
    i = pl.program_id(0)
    tm = f_ref.shape[0]

    def zero_copy(b):
        return pltpu.make_async_copy(zbuf, xs_hbm.at[pl.ds(b * rows, rows)], zsem)

    @pl.when(i == 0)
    def _():
        zbuf[...] = jnp.zeros_like(zbuf)

        def issue(b, c):
            @pl.when(zero_ref[b] != 0)
            def _():
                zero_copy(b).start()
            return c

        def drain(b, c):
            @pl.when(zero_ref[b] != 0)
            def _():
                zero_copy(b).wait()
            return c

        lax.fori_loop(0, n_blk, issue, 0)
        lax.fori_loop(0, n_blk, drain, 0)

    def row_copy(t, k):
        row = pos_ref[(i * tm + t) * TOP_K + k]
        return pltpu.make_async_copy(f_ref.at[pl.ds(t, 1)], xs_hbm.at[pl.ds(row, 1)], sem)

    def issue_rows(t, c):
        for k in range(TOP_K):
            row_copy(t, k).start()
        return c

    def drain_rows(t, c):
        for k in range(TOP_K):
            row_copy(t, k).wait()
        return c

    lax.fori_loop(0, tm, issue_rows, 0, unroll=DMA_ISSUE_UNROLL)
    lax.fori_loop(0, tm, drain_rows, 0, unroll=DMA_ISSUE_UNROLL)


def dispatch(fpk, pos, zero_blk, *, rows=EXPERT_TM, tm=NORM_TM):
    t, half = fpk.shape
    n_blk = zero_blk.shape[0]
    grid_spec = pltpu.PrefetchScalarGridSpec(
        num_scalar_prefetch=2,
        grid=(t // tm,),
        in_specs=[pl.BlockSpec((tm, half), lambda i, p, z: (i, 0))],
        out_specs=pl.BlockSpec(memory_space=pl.ANY),
        scratch_shapes=[pltpu.VMEM((rows, half), fpk.dtype), pltpu.SemaphoreType.DMA(()),
                        pltpu.SemaphoreType.DMA(())],
    )
    return pl.pallas_call(
        functools.partial(_dispatch_kernel, rows=rows, n_blk=n_blk),
        grid_spec=grid_spec,
        out_shape=jax.ShapeDtypeStruct((n_blk * rows, half), fpk.dtype),
        compiler_params=_cparams(("arbitrary",), has_side_effects=True),
        name="dispatch",
    )(pos, zero_blk, fpk)


def _weight_tile_refresh(s, flag, tile_copies, cast):
    @pl.when((flag & SCHED_FIRST) != 0)
    def _():
        @pl.when(s == 0)
        def _():
            for cp in tile_copies(False):
                cp.start()

        for cp in tile_copies(False):
            cp.wait()
        cast()

        @pl.when((flag & SCHED_PREFETCH) != 0)
        def _():
            for cp in tile_copies(True):
                cp.start()


def _expert_up_kernel(xb_ref, ob_ref, oc_ref, se_ref, sc_ref, flag_ref, ne_ref, nc_ref, xs_ref, w_hbm, bg_ref,
                      bu_ref, act_ref, wstage, wgb, wub, sem, *, tf, nf):
    s = pl.program_id(0)
    half = xs_ref.shape[1]
    flag = flag_ref[s]

    def tile_copies(following):
        e = (ne_ref if following else se_ref)[s]
        c = (nc_ref if following else sc_ref)[s]
        gate_cols = pl.ds(pl.multiple_of(c * tf, tf), tf)
        up_cols = pl.ds(pl.multiple_of((nf + c) * tf, tf), tf)
        return (pltpu.make_async_copy(w_hbm.at[e, :, gate_cols], wstage.at[0], sem.at[0]),
                pltpu.make_async_copy(w_hbm.at[e, :, up_cols], wstage.at[1], sem.at[1]))

    def cast():
        wgb[...] = wstage[0].astype(BF16)
        wub[...] = wstage[1].astype(BF16)

    _weight_tile_refresh(s, flag, tile_copies, cast)

    @pl.when((flag & SCHED_ACTIVE) != 0)
    def _():
        lo, hi = _unpack_bf16_pairs(xs_ref[...])
        glu = (jnp.dot(lo, wgb[0:half, :], preferred_element_type=F32)
               + jnp.dot(hi, wgb[half:, :], preferred_element_type=F32) + bg_ref[...])
        up = (jnp.dot(lo, wub[0:half, :], preferred_element_type=F32)
              + jnp.dot(hi, wub[half:, :], preferred_element_type=F32) + bu_ref[...])
        glu = jnp.minimum(glu, SWIGLU_LIMIT)
        up = jnp.clip(up, -SWIGLU_LIMIT, SWIGLU_LIMIT)
        act_ref[...] = ((up + 1.0) * (glu * jax.nn.sigmoid(SWIGLU_ALPHA * glu))).astype(act_ref.dtype)

    @pl.when((flag & SCHED_ACTIVE) == 0)
    def _():
        act_ref[...] = jnp.zeros_like(act_ref)


def expert_up(xs, w_gate_up, b_gate_up, sched, *, tm=EXPERT_TM, tf=EXPERT_TF):
    n_rows, half = xs.shape
    d = 2 * half
    ff = EXPERT_FF
    nf = ff // tf
    n_steps = sched[0].shape[0]
    grid_spec = pltpu.PrefetchScalarGridSpec(
        num_scalar_prefetch=len(sched),
        grid=(n_steps,),
        in_specs=[pl.BlockSpec((tm, half), lambda s, *t: (t[0][s], 0)),
                  pl.BlockSpec(memory_space=pl.ANY),
                  pl.BlockSpec((None, 1, tf), lambda s, *t: (t[3][s], 0, t[4][s])),
                  pl.BlockSpec((None, 1, tf), lambda s, *t: (t[3][s], 0, nf + t[4][s]))],
        out_specs=pl.BlockSpec((tm, tf), lambda s, *t: (t[1][s], t[2][s])),
        scratch_shapes=[pltpu.VMEM((2, d, tf), F32), pltpu.VMEM((d, tf), BF16), pltpu.VMEM((d, tf), BF16),
                        pltpu.SemaphoreType.DMA((2,))],
    )
    return pl.pallas_call(
        functools.partial(_expert_up_kernel, tf=tf, nf=nf),
        grid_spec=grid_spec,
        out_shape=jax.ShapeDtypeStruct((n_rows, ff), BF16),
        compiler_params=_cparams(("arbitrary",)),
        name="expert_up",
    )(*sched, xs, w_gate_up, b_gate_up, b_gate_up)


def _expert_down_kernel(xb_ref, ob_ref, oc_ref, se_ref, sc_ref, flag_ref, ne_ref, nc_ref, act_ref, w_hbm, bd_ref,
                        y_ref, wstage, wdb, sem, *, tn):
    s = pl.program_id(0)
    flag = flag_ref[s]

    def tile_copies(following):
        e = (ne_ref if following else se_ref)[s]
        c = (nc_ref if following else sc_ref)[s]
        cols = pl.ds(pl.multiple_of(c * tn, tn), tn)
        return (pltpu.make_async_copy(w_hbm.at[e, :, cols], wstage, sem),)

    def cast():
        wdb[...] = wstage[...].astype(BF16)

    _weight_tile_refresh(s, flag, tile_copies, cast)

    @pl.when((flag & SCHED_ACTIVE) != 0)
    def _():
        y_ref[...] = jnp.dot(act_ref[...], wdb[...], preferred_element_type=F32) + bd_ref[...]

    @pl.when((flag & SCHED_ACTIVE) == 0)
    def _():
        y_ref[...] = jnp.zeros_like(y_ref)


def expert_down(act, w_down, b_down, sched, *, tm=EXPERT_TM, tn=EXPERT_TN):
    n_rows, ff = act.shape
    d = w_down.shape[2]
    n_steps = sched[0].shape[0]
    grid_spec = pltpu.PrefetchScalarGridSpec(
        num_scalar_prefetch=len(sched),
        grid=(n_steps,),
        in_specs=[pl.BlockSpec((tm, ff), lambda s, *t: (t[0][s], 0)),
                  pl.BlockSpec(memory_space=pl.ANY),
                  pl.BlockSpec((None, 1, tn), lambda s, *t: (t[3][s], 0, t[4][s]))],
        out_specs=pl.BlockSpec((tm, tn), lambda s, *t: (t[1][s], t[2][s])),
        scratch_shapes=[pltpu.VMEM((ff, tn), F32), pltpu.VMEM((ff, tn), BF16), pltpu.SemaphoreType.DMA(())],
    )
    return pl.pallas_call(
        functools.partial(_expert_down_kernel, tn=tn),
        grid_spec=grid_spec,
        out_shape=jax.ShapeDtypeStruct((n_rows, d), F32),
        compiler_params=_cparams(("arbitrary",)),
        name="expert_down",
    )(*sched, act, w_down, b_down)


def _combine_kernel(pos_ref, y_hbm, h_ref, gate_ref, g_ref, o_ref, a_ref, buf, sem, *, tq, n_steps):
    i = pl.program_id(0)
    slot = i % 2

    def row_copy(step, sl, t, k):
        row = pos_ref[(step * tq + t) * TOP_K + k]
        return pltpu.make_async_copy(y_hbm.at[pl.ds(row, 1)], buf.at[sl, k, pl.ds(t, 1)], sem.at[sl])

    def issue(step, sl):
        def body(t, carry):
            for k in range(TOP_K):
                row_copy(step, sl, t, k).start()
            return carry

        lax.fori_loop(0, tq, body, 0, unroll=DMA_ISSUE_UNROLL)

    @pl.when(i == 0)
    def _():
        issue(0, 0)

    @pl.when(i + 1 < n_steps)
    def _():
        issue(i + 1, 1 - slot)

    def drain(t, carry):
        for k in range(TOP_K):
            row_copy(i, slot, t, k).wait()
        return carry

    lax.fori_loop(0, tq, drain, 0, unroll=DMA_ISSUE_UNROLL)
    gates = gate_ref[...]
    moe = ((gates[:, 0:1] * buf[slot, 0] + gates[:, 1:2] * buf[slot, 1])
           + (gates[:, 2:3] * buf[slot, 2] + gates[:, 3:4] * buf[slot, 3]))
    out = h_ref[...] + moe
    o_ref[...] = out
    a_ref[...] = _rmsnorm_body(out, g_ref[...]).astype(a_ref.dtype)


def combine(y, pos, h, gates, g_next, *, tq=COMBINE_TQ):
    t, d = h.shape
    rows = lambda i, p: (i, 0)
    grid_spec = pltpu.PrefetchScalarGridSpec(
        num_scalar_prefetch=1,
        grid=(t // tq,),
        in_specs=[pl.BlockSpec(memory_space=pl.ANY),
                  pl.BlockSpec((tq, d), rows),
                  pl.BlockSpec((tq, V7X_LANES), rows),
                  pl.BlockSpec((1, d), lambda i, p: (0, 0))],
        out_specs=[pl.BlockSpec((tq, d), rows), pl.BlockSpec((tq, d), rows)],
        scratch_shapes=[pltpu.VMEM((2, TOP_K, tq, d), F32), pltpu.SemaphoreType.DMA((2,))],
    )
    return pl.pallas_call(
        functools.partial(_combine_kernel, tq=tq, n_steps=t // tq),
        grid_spec=grid_spec,
        out_shape=[jax.ShapeDtypeStruct((t, d), F32), jax.ShapeDtypeStruct((t, d), BF16)],
        compiler_params=_cparams(("arbitrary",)),
        name="combine",
    )(pos, y, h, gates, g_next.reshape(1, d))


def _ple_kernel(a_ref, w_ref, p_ref, wp_ref, h_ref, o_ref):
    gate = jax.nn.sigmoid(jnp.dot(a_ref[...], w_ref[...], preferred_element_type=F32))
    emb = jnp.dot(p_ref[...], wp_ref[...], preferred_element_type=F32)
    o_ref[...] = h_ref[...] + gate * emb


def ple(a, w_gate, p, w_proj, h, *, tm=MM_TM // 2, tn=MM_TN):
    t, k = a.shape
    n = w_gate.shape[1]
    pk = p.shape[1]
    tm, tn = min(tm, t), min(tn, n)
    return pl.pallas_call(
        _ple_kernel,
        grid=(t // tm, n // tn),
        in_specs=[pl.BlockSpec((tm, k), lambda i, j: (i, 0)),
                  pl.BlockSpec((k, tn), lambda i, j: (0, j)),
                  pl.BlockSpec((tm, pk), lambda i, j: (i, 0)),
                  pl.BlockSpec((pk, tn), lambda i, j: (0, j)),
                  pl.BlockSpec((tm, tn), lambda i, j: (i, j))],
        out_specs=pl.BlockSpec((tm, tn), lambda i, j: (i, j)),
        out_shape=jax.ShapeDtypeStruct((t, n), F32),
        compiler_params=_cparams(("parallel", "arbitrary")),
        name="ple",
    )(a, w_gate, p, w_proj, h)


def kernel(x, p, positions, attn_norm_g, w_in, conv_w, conv_b, b_igate, b_fgate, mh_norm_g, w_out,
           ffn_norm_g, w_router, b_router, w_gate_up, b_gate_up, w_down, b_down, ple_norm_g, w_ple_gate,
           w_ple_proj, final_norm_g):
    batch, seq, d = x.shape
    depth = p.shape[0]
    t = batch * seq
    n_main = 3 * ATTN_WIDTH + 2 * MLSTM_QK_WIDTH + 2 * MLSTM_V_WIDTH
    nh = MLSTM_HEADS
    h = x.reshape(t, d)
    cos, sin = rope_tables(positions.reshape(t, 1))
    for i in range(depth):
        a = rmsnorm(h, attn_norm_g[i], BF16)
        w_gates = jnp.zeros((d, V7X_LANES), BF16).at[:, :2 * nh].set(w_in[i][:, n_main:].astype(BF16))
        z, gates = in_projection(a, w_in[i].astype(BF16), n_main, w_gates, cos, sin)
        attn = dilated_attention(z, batch, seq)
        gate_bias = jnp.zeros((1, V7X_LANES), F32).at[0, :nh].set(b_igate[i]).at[0, nh:2 * nh].set(b_fgate[i])
        ml = mlstm(z, gates, gate_bias, conv_w[i], conv_b[i], mh_norm_g[i], batch, seq)
        h = out_projection(attn, ml, w_out[i].astype(BF16), h)
        fpk, top_e, top_g, rank, counts = ffn_norm_router(h, ffn_norm_g[i], w_router[i], b_router[i])
        pos, zero_blk, layout = routing_tables(top_e[:, :TOP_K], rank[:, :TOP_K],
                                               counts[0, :N_EXPERTS].astype(I32), t)
        xs = dispatch(fpk, pos, zero_blk)
        act = expert_up(xs, w_gate_up[i], b_gate_up[i].reshape(N_EXPERTS, 1, 2 * EXPERT_FF),
                        resident_schedule(layout, EXPERT_FF // EXPERT_TF))
        tn = min(EXPERT_TN, d)
        y = expert_down(act, w_down[i], b_down[i].reshape(N_EXPERTS, 1, d),
                        resident_schedule(layout, d // tn), tn=tn)
        h, a2 = combine(y, pos, h, top_g, ple_norm_g[i])
        h = ple(a2, w_ple_gate[i].astype(BF16), p[i].reshape(t, -1).astype(BF16), w_ple_proj[i].astype(BF16), h)
    return rmsnorm(h, final_norm_g, x.dtype).reshape(batch, seq, d)
```

```python
import functools

import jax
import jax.numpy as jnp
from jax import lax
from jax.experimental import pallas as pl
from jax.experimental.pallas import tpu as pltpu

F32 = jnp.float32
BF16 = jnp.bfloat16
U32 = jnp.uint32
I32 = jnp.int32

V7X_VMEM_BYTES = 64 * 1024 * 1024
V7X_LANES = 128
VMEM_LIMIT = V7X_VMEM_BYTES - 8 * 1024 * 1024

ATTN_HEADS = 16
HEAD_DIM = 128
ATTN_WIDTH = ATTN_HEADS * HEAD_DIM
ATTN_BLOCK = 128
DILATIONS = (1, 4, 16)
ATTN_SUPER = ATTN_BLOCK * max(DILATIONS)
ROPE_THETA = 10000.0
MLSTM_HEADS = 4
MLSTM_QK_DIM = 256
MLSTM_V_DIM = 512
MLSTM_QK_WIDTH = MLSTM_HEADS * MLSTM_QK_DIM
MLSTM_V_WIDTH = MLSTM_HEADS * MLSTM_V_DIM
MLSTM_CHUNK = 256
CONV_WIDTH = 4
CONV_PAD = 8
N_EXPERTS = 32
TOP_K = 4
EXPERT_FF = 1536
SWIGLU_LIMIT = 7.0
SWIGLU_ALPHA = 1.702
NORM_EPS = 1e-6
NEG_BIG = -1e30

MM_TM = 1024
MM_TN = 1024
NORM_TM = 256
EXPERT_TM = 512
EXPERT_TF = 512
EXPERT_TN = 2048
COMBINE_TQ = 128
DMA_ISSUE_UNROLL = 4


def _cparams(sem, **kw):
    return pltpu.CompilerParams(dimension_semantics=sem, vmem_limit_bytes=VMEM_LIMIT, **kw)


def _rmsnorm_body(x, g):
    r = lax.rsqrt(jnp.mean(x * x, axis=-1, keepdims=True) + NORM_EPS)
    return x * r * g


def _rmsnorm_kernel(x_ref, g_ref, o_ref):
    o_ref[...] = _rmsnorm_body(x_ref[...], g_ref[...]).astype(o_ref.dtype)


def rmsnorm(x, g, out_dtype):
    t, d = x.shape
    return pl.pallas_call(
        _rmsnorm_kernel,
        grid=(t // NORM_TM,),
        in_specs=[pl.BlockSpec((NORM_TM, d), lambda i: (i, 0)),
                  pl.BlockSpec((1, d), lambda i: (0, 0))],
        out_specs=pl.BlockSpec((NORM_TM, d), lambda i: (i, 0)),
        out_shape=jax.ShapeDtypeStruct((t, d), out_dtype),
        compiler_params=_cparams(("parallel",)),
        name="rmsnorm",
    )(x, g.reshape(1, d))


def _rope_table_kernel(pos_ref, freq_ref, sign_ref, cos_ref, sin_ref):
    ang = pos_ref[...].astype(F32) * freq_ref[...]
    cos_ref[...] = jnp.cos(ang)
    sin_ref[...] = jnp.sin(ang) * sign_ref[...]


def rope_tables(positions_col):
    t = positions_col.shape[0]
    half = HEAD_DIM // 2
    inv_freq = jnp.power(ROPE_THETA, -jnp.arange(half, dtype=F32) / half)
    freq = jnp.concatenate([inv_freq, inv_freq]).reshape(1, HEAD_DIM)
    sign = jnp.concatenate([-jnp.ones((half,), F32), jnp.ones((half,), F32)]).reshape(1, HEAD_DIM)
    tm = 512
    return pl.pallas_call(
        _rope_table_kernel,
        grid=(t // tm,),
        in_specs=[pl.BlockSpec((tm, 1), lambda i: (i, 0)),
                  pl.BlockSpec((1, HEAD_DIM), lambda i: (0, 0)),
                  pl.BlockSpec((1, HEAD_DIM), lambda i: (0, 0))],
        out_specs=[pl.BlockSpec((tm, HEAD_DIM), lambda i: (i, 0))] * 2,
        out_shape=[jax.ShapeDtypeStruct((t, HEAD_DIM), F32)] * 2,
        compiler_params=_cparams(("parallel",)),
        name="rope_tables",
    )(positions_col, freq, sign)


def _inproj_kernel(a_ref, w_ref, wg_ref, cos_ref, sin_ref, z_ref, gate_ref, *, n_q_tiles, n_k_tiles,
                   q_scale):
    j = pl.program_id(1)

    @pl.when(j == 0)
    def _():
        gate_ref[...] = jnp.dot(a_ref[...], wg_ref[...], preferred_element_type=F32)

    def rope(scale):
        cos = cos_ref[...]
        sin = sin_ref[...]
        pair = 2 * HEAD_DIM
        for c0 in range(0, z_ref.shape[1], pair):
            acc = jnp.dot(a_ref[...], w_ref[:, c0:c0 + pair], preferred_element_type=F32)
            for c in range(c0, c0 + pair, HEAD_DIM):
                t = acc[:, c - c0:c - c0 + HEAD_DIM]
                r = pltpu.roll(t, HEAD_DIM // 2, axis=1)
                z_ref[:, c:c + HEAD_DIM] = ((t * cos + r * sin) * scale).astype(z_ref.dtype)

    @pl.when(j < n_q_tiles)
    def _():
        rope(q_scale)

    @pl.when((j >= n_q_tiles) & (j < n_q_tiles + n_k_tiles))
    def _():
        rope(1.0)

    @pl.when(j >= n_q_tiles + n_k_tiles)
    def _():
        z_ref[...] = jnp.dot(a_ref[...], w_ref[...], preferred_element_type=F32).astype(z_ref.dtype)


def in_projection(a, w, n, w_gates, cos, sin, *, tm=MM_TM, tn=MM_TN):
    t, k = a.shape
    tm, tn = min(tm, t), min(tn, n)
    kern = functools.partial(_inproj_kernel, n_q_tiles=ATTN_WIDTH // tn, n_k_tiles=ATTN_WIDTH // tn,
                             q_scale=HEAD_DIM ** -0.5)
    return pl.pallas_call(
        kern,
        grid=(t // tm, n // tn),
        in_specs=[pl.BlockSpec((tm, k), lambda i, j: (i, 0)),
                  pl.BlockSpec((k, tn), lambda i, j: (0, j)),
                  pl.BlockSpec((k, V7X_LANES), lambda i, j: (0, 0)),
                  pl.BlockSpec((tm, HEAD_DIM), lambda i, j: (i, 0)),
                  pl.BlockSpec((tm, HEAD_DIM), lambda i, j: (i, 0))],
        out_specs=[pl.BlockSpec((tm, tn), lambda i, j: (i, j)),
                   pl.BlockSpec((tm, V7X_LANES), lambda i, j: (i, 0))],
        out_shape=[jax.ShapeDtypeStruct((t, n), BF16),
                   jax.ShapeDtypeStruct((t, V7X_LANES), F32)],
        compiler_params=_cparams(("parallel", "arbitrary")),
        name="in_projection",
    )(a, w, w_gates, cos, sin)


def _band_block(qb, kb, vb, bias):
    s = lax.dot_general(qb, kb, (((1,), (1,)), ((), ())), preferred_element_type=F32) + bias
    m = jnp.max(s, axis=-1, keepdims=True)
    p = jnp.exp(s - m)
    l = jnp.sum(p, axis=-1, keepdims=True)
    o = jnp.dot(p.astype(BF16), vb, preferred_element_type=F32) / l
    return o, m + jnp.log(l)


def _attn_kernel(q_ref, kp_ref, kc_ref, vp_ref, vc_ref, o_ref, qf, kf, vf, ob, lb):
    sb = ATTN_SUPER
    blk = ATTN_BLOCK
    n = pl.program_id(2)
    qf[...] = q_ref[...].astype(F32)
    kf[0:sb, :] = kp_ref[...].astype(F32)
    kf[sb:2 * sb, :] = kc_ref[...].astype(F32)
    vf[0:sb, :] = vp_ref[...].astype(F32)
    vf[sb:2 * sb, :] = vc_ref[...].astype(F32)

    qi = lax.broadcasted_iota(I32, (blk, 2 * blk), 0)
    kj = lax.broadcasted_iota(I32, (blk, 2 * blk), 1)
    dist = blk + qi - kj
    band = (dist >= 0) & (dist <= blk)
    bias_in = jnp.where(band, 0.0, NEG_BIG).astype(F32)
    prev_ok = jnp.where(n > 0, 0.0, NEG_BIG).astype(F32)
    bias_edge = bias_in + jnp.where(kj < blk, prev_ok, 0.0)

    for p, d in enumerate(DILATIONS):
        per_class = sb // (d * blk)
        for r in range(d):
            for nb in range(per_class):
                q0 = r + d * blk * nb
                k0 = sb + q0 - d * blk
                if d == 1:
                    qb = qf[pl.ds(q0, blk), :]
                    kb = kf[pl.ds(k0, 2 * blk), :]
                    vb = vf[pl.ds(k0, 2 * blk), :]
                else:
                    qb = qf[pl.ds(q0, blk, stride=d), :]
                    kb = kf[pl.ds(k0, 2 * blk, stride=d), :]
                    vb = vf[pl.ds(k0, 2 * blk, stride=d), :]
                bias = bias_edge if nb == 0 else bias_in
                o, lse = _band_block(qb.astype(BF16), kb.astype(BF16), vb.astype(BF16), bias)
                lse_b = jnp.broadcast_to(lse, (blk, HEAD_DIM))
                if d == 1:
                    ob[p, pl.ds(q0, blk), :] = o
                    lb[p, pl.ds(q0, blk), :] = lse_b
                else:
                    ob[p, pl.ds(q0, blk, stride=d), :] = o
                    lb[p, pl.ds(q0, blk, stride=d), :] = lse_b

    l0, l1, l2 = lb[0], lb[1], lb[2]
    mx = jnp.maximum(jnp.maximum(l0, l1), l2)
    w0, w1, w2 = jnp.exp(l0 - mx), jnp.exp(l1 - mx), jnp.exp(l2 - mx)
    out = (w0 * ob[0] + w1 * ob[1] + w2 * ob[2]) / (w0 + w1 + w2)
    o_ref[...] = out.astype(o_ref.dtype)


def dilated_attention(z, batch, seq):
    t = z.shape[0]
    sb = ATTN_SUPER
    nsb = seq // sb
    h = ATTN_HEADS
    cur = lambda off: (lambda b, hh, n: (b * nsb + n, off + hh))
    prev = lambda off: (lambda b, hh, n: (b * nsb + jnp.maximum(n - 1, 0), off + hh))
    spec = lambda im: pl.BlockSpec((sb, HEAD_DIM), im)
    return pl.pallas_call(
        _attn_kernel,
        grid=(batch, h, nsb),
        in_specs=[spec(cur(0)), spec(prev(h)), spec(cur(h)), spec(prev(2 * h)), spec(cur(2 * h))],
        out_specs=spec(cur(0)),
        out_shape=jax.ShapeDtypeStruct((t, ATTN_WIDTH), BF16),
        scratch_shapes=[pltpu.VMEM((sb, HEAD_DIM), F32),
                        pltpu.VMEM((2 * sb, HEAD_DIM), F32),
                        pltpu.VMEM((2 * sb, HEAD_DIM), F32),
                        pltpu.VMEM((len(DILATIONS), sb, HEAD_DIM), F32),
                        pltpu.VMEM((len(DILATIONS), sb, HEAD_DIM), F32)],
        compiler_params=_cparams(("parallel", "parallel", "arbitrary")),
        name="dilated_attention",
    )(z, z, z, z, z)


def _log_sigmoid(x):
    return jnp.minimum(x, 0.0) - jnp.log(1.0 + jnp.exp(-jnp.abs(x)))


def _mlstm_kernel(q_ref, k_ref, v_ref, om_ref, g_ref, gb_ref, cw_ref, cb_ref, mhg_ref, o_ref,
                  xbuf, c_st, n_st, m_st):
    L = MLSTM_CHUNK
    dqk, dv, nh = MLSTM_QK_DIM, MLSTM_V_DIM, MLSTM_HEADS
    qw = nh * dqk
    c = pl.program_id(1)

    @pl.when(c == 0)
    def _():
        xbuf[0:CONV_PAD, :] = jnp.zeros((CONV_PAD, 2 * qw), F32)
        c_st[...] = jnp.zeros_like(c_st)
        n_st[...] = jnp.zeros_like(n_st)
        m_st[...] = jnp.zeros_like(m_st)

    @pl.when(c > 0)
    def _():
        xbuf[0:CONV_PAD, :] = xbuf[L:L + CONV_PAD, :]

    xbuf[CONV_PAD:CONV_PAD + L, 0:qw] = q_ref[...].astype(F32)
    xbuf[CONV_PAD:CONV_PAD + L, qw:2 * qw] = k_ref[...].astype(F32)
    y = cb_ref[...]
    for j in range(CONV_WIDTH):
        y = y + cw_ref[j:j + 1, :] * xbuf[pl.ds(CONV_PAD - CONV_WIDTH + 1 + j, L), :]
    qk = y * jax.nn.sigmoid(y)

    pre = g_ref[...] + gb_ref[...]
    lf = _log_sigmoid(pre)
    row = lax.broadcasted_iota(I32, (L, L), 0)
    col = lax.broadcasted_iota(I32, (L, L), 1)
    causal = col <= row
    tril = jnp.where(causal, 1.0, 0.0).astype(F32)
    bcum = jnp.dot(tril, lf, preferred_element_type=F32, precision=lax.Precision.HIGHEST)
    pre_t = pre.T
    bcum_t = bcum.T

    for h in range(nh):
        i_c = pre[:, h:h + 1]
        b_c = bcum[:, nh + h:nh + h + 1]
        i_r = pre_t[h:h + 1, :]
        b_r = bcum_t[nh + h:nh + h + 1, :]
        g = bcum[L - 1:L, nh + h:nh + h + 1]
        m = m_st[h, 0:1, 0:1]
        qq = (qk[:, h * dqk:(h + 1) * dqk] * (dqk ** -0.5))
        kk = qk[:, qw + h * dqk:qw + (h + 1) * dqk]
        vv = v_ref[:, h * dv:(h + 1) * dv]
        qb = qq.astype(BF16)

        log_d = jnp.where(causal, b_c - b_r + i_r, NEG_BIG)
        inter = b_c + m
        m_t = jnp.maximum(inter, jnp.max(log_d, axis=-1, keepdims=True))
        w_inter = jnp.exp(inter - m_t)
        s = lax.dot_general(qb, kk.astype(BF16), (((1,), (1,)), ((), ())),
                            preferred_element_type=F32) * jnp.exp(log_d - m_t)
        num = (w_inter * jnp.dot(qb, c_st[h].astype(BF16), preferred_element_type=F32)
               + jnp.dot(s.astype(BF16), vv, preferred_element_type=F32))
        den = (w_inter * jnp.sum(qq * n_st[h], axis=-1, keepdims=True)
               + jnp.sum(s, axis=-1, keepdims=True))
        hh = num / jnp.maximum(jnp.abs(den), jnp.exp(-m_t))

        log_w = g - b_c + i_c
        m_new = jnp.maximum(g + m, jnp.max(log_w, axis=0, keepdims=True))
        decay = jnp.exp(g + m - m_new)
        wk = kk * jnp.exp(log_w - m_new)
        c_st[h] = decay * c_st[h] + jnp.dot(wk.T.astype(BF16), vv, preferred_element_type=F32)
        n_st[h] = decay * n_st[h] + jnp.sum(wk, axis=0, keepdims=True)
        m_st[h] = jnp.broadcast_to(m_new, m_st.shape[1:])

        hn = hh * lax.rsqrt(jnp.mean(hh * hh, axis=-1, keepdims=True) + NORM_EPS)
        hn = hn * mhg_ref[:, h * dv:(h + 1) * dv]
        og = jax.nn.sigmoid(om_ref[:, h * dv:(h + 1) * dv].astype(F32))
        o_ref[:, h * dv:(h + 1) * dv] = (og * hn).astype(o_ref.dtype)


def mlstm(z, gates, gate_bias, conv_w, conv_b, mh_norm_g, batch, seq):
    t = z.shape[0]
    L = MLSTM_CHUNK
    nc = seq // L
    qw, vw = MLSTM_QK_WIDTH, MLSTM_V_WIDTH
    q_off = 3 * ATTN_WIDTH
    rows = lambda blk: (lambda b, c: (b * nc + c, blk))
    const = lambda b, c: (0, 0)
    return pl.pallas_call(
        _mlstm_kernel,
        grid=(batch, nc),
        in_specs=[pl.BlockSpec((L, qw), rows(q_off // qw)),
                  pl.BlockSpec((L, qw), rows(q_off // qw + 1)),
                  pl.BlockSpec((L, vw), rows((q_off + 2 * qw) // vw)),
                  pl.BlockSpec((L, vw), rows((q_off + 2 * qw) // vw + 1)),
                  pl.BlockSpec((L, V7X_LANES), rows(0)),
                  pl.BlockSpec((1, V7X_LANES), const),
                  pl.BlockSpec((CONV_WIDTH, 2 * qw), const),
                  pl.BlockSpec((1, 2 * qw), const),
                  pl.BlockSpec((1, vw), const)],
        out_specs=pl.BlockSpec((L, vw), rows(0)),
        out_shape=jax.ShapeDtypeStruct((t, vw), BF16),
        scratch_shapes=[pltpu.VMEM((L + CONV_PAD, 2 * qw), F32),
                        pltpu.VMEM((MLSTM_HEADS, MLSTM_QK_DIM, MLSTM_V_DIM), F32),
                        pltpu.VMEM((MLSTM_HEADS, 1, MLSTM_QK_DIM), F32),
                        pltpu.VMEM((MLSTM_HEADS, 8, V7X_LANES), F32)],
        compiler_params=_cparams(("parallel", "arbitrary")),
        name="mlstm",
    )(z, z, z, z, gates, gate_bias, conv_w, conv_b.reshape(1, 2 * qw), mh_norm_g.reshape(1, vw))


def _outproj_kernel(a1_ref, a2_ref, w_ref, res_ref, o_ref):
    k1 = a1_ref.shape[1]
    acc = jnp.dot(a1_ref[...], w_ref[0:k1, :], preferred_element_type=F32)
    acc = acc + jnp.dot(a2_ref[...], w_ref[k1:, :], preferred_element_type=F32)
    o_ref[...] = res_ref[...] + acc


def out_projection(a1, a2, w, res, *, tm=MM_TM, tn=MM_TN):
    t, k1 = a1.shape
    k2 = a2.shape[1]
    n = w.shape[1]
    tm, tn = min(tm, t), min(tn, n)
    return pl.pallas_call(
        _outproj_kernel,
        grid=(t // tm, n // tn),
        in_specs=[pl.BlockSpec((tm, k1), lambda i, j: (i, 0)),
                  pl.BlockSpec((tm, k2), lambda i, j: (i, 0)),
                  pl.BlockSpec((k1 + k2, tn), lambda i, j: (0, j)),
                  pl.BlockSpec((tm, tn), lambda i, j: (i, j))],
        out_specs=pl.BlockSpec((tm, tn), lambda i, j: (i, j)),
        out_shape=jax.ShapeDtypeStruct((t, n), F32),
        compiler_params=_cparams(("parallel", "arbitrary")),
        name="out_projection",
    )(a1, a2, w, res)


def _pack_bf16_pairs(f):
    half = f.shape[1] // 2
    fb = f.astype(BF16).astype(F32)
    lo = pltpu.bitcast(fb[:, :half], U32) >> 16
    hi = pltpu.bitcast(fb[:, half:], U32) & jnp.uint32(0xFFFF0000)
    return lo | hi


def _unpack_bf16_pairs(w):
    lo = pltpu.bitcast(w << 16, F32).astype(BF16)
    hi = pltpu.bitcast(w & jnp.uint32(0xFFFF0000), F32).astype(BF16)
    return lo, hi


def _router_kernel(h_ref, g_ref, wr_ref, br_ref, fpk_ref, e_ref, gate_ref, rank_ref, cnt_ref, carry):
    i = pl.program_id(0)

    @pl.when(i == 0)
    def _():
        carry[...] = jnp.zeros_like(carry)

    f = _rmsnorm_body(h_ref[...], g_ref[...])
    fpk_ref[...] = _pack_bf16_pairs(f)
    f_hi = f.astype(BF16)
    f_lo = (f - f_hi.astype(F32)).astype(BF16)
    logits = (jnp.dot(f_hi, wr_ref[0], preferred_element_type=F32)
              + (jnp.dot(f_lo, wr_ref[0], preferred_element_type=F32)
                 + jnp.dot(f_hi, wr_ref[1], preferred_element_type=F32))) + br_ref[...]
    lane = lax.broadcasted_iota(I32, logits.shape, 1)
    cur = logits
    vals, idxs = [], []
    for _ in range(TOP_K):
        mx = jnp.max(cur, axis=-1, keepdims=True)
        idx = jnp.min(jnp.where(cur == mx, lane, V7X_LANES), axis=-1, keepdims=True)
        vals.append(mx)
        idxs.append(idx)
        cur = jnp.where(lane == idx, -jnp.inf, cur)
    exps = [jnp.exp(v - vals[0]) for v in vals]
    tot = exps[0] + exps[1] + exps[2] + exps[3]
    e_out = jnp.zeros(logits.shape, I32)
    g_out = jnp.zeros(logits.shape, F32)
    for k in range(TOP_K):
        e_out = jnp.where(lane == k, idxs[k], e_out)
        g_out = jnp.where(lane == k, exps[k] / tot, g_out)
    e_ref[...] = e_out
    gate_ref[...] = g_out

    tm = logits.shape[0]
    onehots = [lane == idxs[k] for k in range(TOP_K)]
    member = jnp.zeros(logits.shape, F32)
    for k in range(TOP_K):
        member = jnp.where(onehots[k], 1.0, member)
    row = lax.broadcasted_iota(I32, (tm, tm), 0)
    col = lax.broadcasted_iota(I32, (tm, tm), 1)
    stril = jnp.where(col < row, 1.0, 0.0).astype(BF16)
    before = carry[0:1, :] + jnp.dot(stril, member.astype(BF16), preferred_element_type=F32)
    r_out = jnp.zeros(logits.shape, I32)
    for k in range(TOP_K):
        rk = jnp.sum(jnp.where(onehots[k], before, 0.0), axis=-1, keepdims=True)
        r_out = jnp.where(lane == k, rk.astype(I32), r_out)
    rank_ref[...] = r_out
    total = carry[0:1, :] + jnp.sum(member, axis=0, keepdims=True)
    carry[...] = jnp.broadcast_to(total, carry.shape)
    cnt_ref[...] = jnp.broadcast_to(total, cnt_ref.shape)


def ffn_norm_router(h, g, w_router, b_router, *, tm=NORM_TM):
    t, d = h.shape
    nt = t // tm
    wr = jnp.zeros((d, V7X_LANES), F32).at[:, :N_EXPERTS].set(w_router)
    wr_hi = wr.astype(BF16)
    wr = jnp.stack([wr_hi, (wr - wr_hi.astype(F32)).astype(BF16)])
    br = jnp.full((1, V7X_LANES), NEG_BIG, F32).at[0, :N_EXPERTS].set(b_router)
    rows = lambda i: (i, 0)
    const = lambda i: (0, 0)
    return pl.pallas_call(
        _router_kernel,
        grid=(nt,),
        in_specs=[pl.BlockSpec((tm, d), rows),
                  pl.BlockSpec((1, d), const),
                  pl.BlockSpec((2, d, V7X_LANES), lambda i: (0, 0, 0)),
                  pl.BlockSpec((1, V7X_LANES), const)],
        out_specs=[pl.BlockSpec((tm, d // 2), rows),
                   pl.BlockSpec((tm, V7X_LANES), rows),
                   pl.BlockSpec((tm, V7X_LANES), rows),
                   pl.BlockSpec((tm, V7X_LANES), rows),
                   pl.BlockSpec((8, V7X_LANES), const)],
        out_shape=[jax.ShapeDtypeStruct((t, d // 2), U32),
                   jax.ShapeDtypeStruct((t, V7X_LANES), I32),
                   jax.ShapeDtypeStruct((t, V7X_LANES), F32),
                   jax.ShapeDtypeStruct((t, V7X_LANES), I32),
                   jax.ShapeDtypeStruct((8, V7X_LANES), F32)],
        scratch_shapes=[pltpu.VMEM((8, V7X_LANES), F32)],
        compiler_params=_cparams(("arbitrary",)),
        name="ffn_norm_router",
    )(h, g.reshape(1, d), wr, br)


def routing_tables(top_e, rank, counts, n_tokens):
    tk = n_tokens * TOP_K
    tm = EXPERT_TM
    n_blk = -(-(tk + N_EXPERTS * (tm - 1)) // tm)
    padded = (counts + tm - 1) // tm * tm
    pend = jnp.cumsum(padded)
    pstart = pend - padded
    pos = (pstart[top_e] + rank).reshape(tk).astype(I32)
    blk_start = jnp.arange(n_blk, dtype=I32) * tm
    blk_e = jnp.minimum(jnp.sum(blk_start[:, None] >= pend[None, :], axis=1), N_EXPERTS - 1).astype(I32)
    n_used = (pend[-1] // tm).astype(I32).reshape(1)
    is_last = jnp.any((blk_start[:, None] + tm == pend[None, :]) & (padded[None, :] > 0), axis=1)
    zero_blk = (is_last | (blk_start >= pend[-1])).astype(I32)
    return pos, zero_blk, (blk_e, n_used[0], pstart // tm, padded // tm, n_blk, counts)


SCHED_ACTIVE = 1
SCHED_FIRST = 2
SCHED_PREFETCH = 4
SCHED_HALF = 8


def resident_schedule(layout, n_col):
    blk_e, n_used, first_blk, n_blks, n_blk, counts = layout
    s = jnp.arange(n_blk * n_col, dtype=I32)
    n_active = n_col * n_used
    active = s < n_active
    sc = jnp.minimum(s, jnp.maximum(n_active - 1, 0))
    e = blk_e[jnp.minimum(sc // n_col, n_blk - 1)]
    b0 = first_blk[e]
    nb = jnp.maximum(n_blks[e], 1)
    local = sc - n_col * b0
    col = local // nb
    j = local % nb
    blk = b0 + j
    out_blk = jnp.where(active, blk, s // n_col)
    out_col = jnp.where(active, col, s % n_col)
    nxt = sc - j + nb
    has_next = nxt < n_active
    nxt = jnp.minimum(nxt, jnp.maximum(n_active - 1, 0))
    half_empty = counts[e] - j * EXPERT_TM <= EXPERT_TM // 2
    flags = jnp.where(active, SCHED_ACTIVE + SCHED_HALF * half_empty
                      + (j == 0) * (SCHED_FIRST + SCHED_PREFETCH * has_next), 0)
    as_i32 = lambda a: a.astype(I32)
    return tuple(map(as_i32, (blk, out_blk, out_col, e, col, flags, e[nxt], col[nxt])))


def _dispatch_kernel(pos_ref, zero_ref, f_ref, xs_hbm, zbuf, sem, zsem, *, rows, n_blk):
    i = pl.program_id(0)
    tm = f_ref.shape[0]

    def zero_copy(b):
        return pltpu.make_async_copy(zbuf, xs_hbm.at[pl.ds(b * rows, rows)], zsem)

    @pl.when(i == 0)
    def _():
        zbuf[...] = jnp.zeros_like(zbuf)

        def issue(b, c):
            @pl.when(zero_ref[b] != 0)
            def _():
                zero_copy(b).start()
            return c

        def drain(b, c):
            @pl.when(zero_ref[b] != 0)
            def _():
                zero_copy(b).wait()
            return c

        lax.fori_loop(0, n_blk, issue, 0)
        lax.fori_loop(0, n_blk, drain, 0)

    def row_copy(t, k):
        row = pos_ref[(i * tm + t) * TOP_K + k]
        return pltpu.make_async_copy(f_ref.at[pl.ds(t, 1)], xs_hbm.at[pl.ds(row, 1)], sem)

    def issue_rows(t, c):
        for k in range(TOP_K):
            row_copy(t, k).start()
        return c

    def drain_rows(t, c):
        for k in range(TOP_K):
            row_copy(t, k).wait()
        return c

    lax.fori_loop(0, tm, issue_rows, 0, unroll=DMA_ISSUE_UNROLL)
    lax.fori_loop(0, tm, drain_rows, 0, unroll=DMA_ISSUE_UNROLL)


def dispatch(fpk, pos, zero_blk, *, rows=EXPERT_TM, tm=NORM_TM):
    t, half = fpk.shape
    n_blk = zero_blk.shape[0]
    grid_spec = pltpu.PrefetchScalarGridSpec(
        num_scalar_prefetch=2,
        grid=(t // tm,),
        in_specs=[pl.BlockSpec((tm, half), lambda i, p, z: (i, 0))],
        out_specs=pl.BlockSpec(memory_space=pl.ANY),
        scratch_shapes=[pltpu.VMEM((rows, half), fpk.dtype), pltpu.SemaphoreType.DMA(()),
                        pltpu.SemaphoreType.DMA(())],
    )
    return pl.pallas_call(
        functools.partial(_dispatch_kernel, rows=rows, n_blk=n_blk),
        grid_spec=grid_spec,
        out_shape=jax.ShapeDtypeStruct((n_blk * rows, half), fpk.dtype),
        compiler_params=_cparams(("arbitrary",), has_side_effects=True),
        name="dispatch",
    )(pos, zero_blk, fpk)


def _weight_tile_refresh(s, flag, tile_copies, cast):
    @pl.when((flag & SCHED_FIRST) != 0)
    def _():
        @pl.when(s == 0)
        def _():
            for cp in tile_copies(False):
                cp.start()

        for cp in tile_copies(False):
            cp.wait()
        cast()

        @pl.when((flag & SCHED_PREFETCH) != 0)
        def _():
            for cp in tile_copies(True):
                cp.start()


def _expert_up_kernel(xb_ref, ob_ref, oc_ref, se_ref, sc_ref, flag_ref, ne_ref, nc_ref, xs_ref, w_hbm, bg_ref,
                      bu_ref, act_ref, wstage, wgb, wub, sem, *, tf, nf):
    s = pl.program_id(0)
    half = xs_ref.shape[1]
    flag = flag_ref[s]

    def tile_copies(following):
        e = (ne_ref if following else se_ref)[s]
        c = (nc_ref if following else sc_ref)[s]
        gate_cols = pl.ds(pl.multiple_of(c * tf, tf), tf)
        up_cols = pl.ds(pl.multiple_of((nf + c) * tf, tf), tf)
        return (pltpu.make_async_copy(w_hbm.at[e, :, gate_cols], wstage.at[0], sem.at[0]),
                pltpu.make_async_copy(w_hbm.at[e, :, up_cols], wstage.at[1], sem.at[1]))

    def cast():
        wgb[...] = wstage[0].astype(BF16)
        wub[...] = wstage[1].astype(BF16)

    _weight_tile_refresh(s, flag, tile_copies, cast)

    @pl.when((flag & SCHED_ACTIVE) != 0)
    def _():
        def rows_compute(rows):
            lo, hi = _unpack_bf16_pairs(xs_ref[0:rows, :])
            glu = (jnp.dot(lo, wgb[0:half, :], preferred_element_type=F32)
                   + jnp.dot(hi, wgb[half:, :], preferred_element_type=F32) + bg_ref[...])
            up = (jnp.dot(lo, wub[0:half, :], preferred_element_type=F32)
                  + jnp.dot(hi, wub[half:, :], preferred_element_type=F32) + bu_ref[...])
            glu = jnp.minimum(glu, SWIGLU_LIMIT)
            up = jnp.clip(up, -SWIGLU_LIMIT, SWIGLU_LIMIT)
            act = (up + 1.0) * (glu * jax.nn.sigmoid(SWIGLU_ALPHA * glu))
            act_ref[0:rows, :] = act.astype(act_ref.dtype)

        tm = xs_ref.shape[0]

        @pl.when((flag & SCHED_HALF) == 0)
        def _():
            rows_compute(tm)

        @pl.when((flag & SCHED_HALF) != 0)
        def _():
            rows_compute(tm // 2)
            act_ref[tm // 2:, :] = jnp.zeros((tm - tm // 2, act_ref.shape[1]), act_ref.dtype)

    @pl.when((flag & SCHED_ACTIVE) == 0)
    def _():
        act_ref[...] = jnp.zeros_like(act_ref)


def expert_up(xs, w_gate_up, b_gate_up, sched, *, tm=EXPERT_TM, tf=EXPERT_TF):
    n_rows, half = xs.shape
    d = 2 * half
    ff = EXPERT_FF
    nf = ff // tf
    n_steps = sched[0].shape[0]
    grid_spec = pltpu.PrefetchScalarGridSpec(
        num_scalar_prefetch=len(sched),
        grid=(n_steps,),
        in_specs=[pl.BlockSpec((tm, half), lambda s, *t: (t[0][s], 0)),
                  pl.BlockSpec(memory_space=pl.ANY),
                  pl.BlockSpec((None, 1, tf), lambda s, *t: (t[3][s], 0, t[4][s])),
                  pl.BlockSpec((None, 1, tf), lambda s, *t: (t[3][s], 0, nf + t[4][s]))],
        out_specs=pl.BlockSpec((tm, tf), lambda s, *t: (t[1][s], t[2][s])),
        scratch_shapes=[pltpu.VMEM((2, d, tf), F32), pltpu.VMEM((d, tf), BF16), pltpu.VMEM((d, tf), BF16),
                        pltpu.SemaphoreType.DMA((2,))],
    )
    return pl.pallas_call(
        functools.partial(_expert_up_kernel, tf=tf, nf=nf),
        grid_spec=grid_spec,
        out_shape=jax.ShapeDtypeStruct((n_rows, ff), BF16),
        compiler_params=_cparams(("arbitrary",)),
        name="expert_up",
    )(*sched, xs, w_gate_up, b_gate_up, b_gate_up)


def _expert_down_kernel(xb_ref, ob_ref, oc_ref, se_ref, sc_ref, flag_ref, ne_ref, nc_ref, act_ref, w_hbm, bd_ref,
                        y_ref, wstage, wdb, sem, *, tn):
    s = pl.program_id(0)
    flag = flag_ref[s]

    def tile_copies(following):
        e = (ne_ref if following else se_ref)[s]
        c = (nc_ref if following else sc_ref)[s]
        cols = pl.ds(pl.multiple_of(c * tn, tn), tn)
        return (pltpu.make_async_copy(w_hbm.at[e, :, cols], wstage, sem),)

    def cast():
        wdb[...] = wstage[...].astype(BF16)

    _weight_tile_refresh(s, flag, tile_copies, cast)

    @pl.when((flag & SCHED_ACTIVE) != 0)
    def _():
        def rows_compute(rows):
            y = jnp.dot(act_ref[0:rows, :], wdb[...], preferred_element_type=F32) + bd_ref[...]
            y_ref[0:rows, :] = _pack_bf16_pairs(y)

        tm = act_ref.shape[0]

        @pl.when((flag & SCHED_HALF) == 0)
        def _():
            rows_compute(tm)

        @pl.when((flag & SCHED_HALF) != 0)
        def _():
            rows_compute(tm // 2)
            y_ref[tm // 2:, :] = jnp.zeros((tm - tm // 2, y_ref.shape[1]), y_ref.dtype)

    @pl.when((flag & SCHED_ACTIVE) == 0)
    def _():
        y_ref[...] = jnp.zeros_like(y_ref)


def expert_down(act, w_down, b_down, sched, *, tm=EXPERT_TM, tn=EXPERT_TN):
    n_rows, ff = act.shape
    d = w_down.shape[2]
    n_steps = sched[0].shape[0]
    grid_spec = pltpu.PrefetchScalarGridSpec(
        num_scalar_prefetch=len(sched),
        grid=(n_steps,),
        in_specs=[pl.BlockSpec((tm, ff), lambda s, *t: (t[0][s], 0)),
                  pl.BlockSpec(memory_space=pl.ANY),
                  pl.BlockSpec((None, 1, tn), lambda s, *t: (t[3][s], 0, t[4][s]))],
        out_specs=pl.BlockSpec((tm, tn // 2), lambda s, *t: (t[1][s], t[2][s])),
        scratch_shapes=[pltpu.VMEM((ff, tn), F32), pltpu.VMEM((ff, tn), BF16), pltpu.SemaphoreType.DMA(())],
    )
    return pl.pallas_call(
        functools.partial(_expert_down_kernel, tn=tn),
        grid_spec=grid_spec,
        out_shape=jax.ShapeDtypeStruct((n_rows, d // 2), U32),
        compiler_params=_cparams(("arbitrary",)),
        name="expert_down",
    )(*sched, act, w_down, b_down)


def _combine_kernel(pos_ref, y_hbm, h_ref, gate_ref, g_ref, o_ref, a_ref, buf, sem, *, tq, n_steps, pair):
    i = pl.program_id(0)
    slot = i % 2

    def row_copy(step, sl, t, k):
        row = pos_ref[(step * tq + t) * TOP_K + k]
        return pltpu.make_async_copy(y_hbm.at[pl.ds(row, 1)], buf.at[sl, k, pl.ds(t, 1)], sem.at[sl])

    def issue(step, sl):
        def body(t, carry):
            for k in range(TOP_K):
                row_copy(step, sl, t, k).start()
            return carry

        lax.fori_loop(0, tq, body, 0, unroll=DMA_ISSUE_UNROLL)

    @pl.when(i == 0)
    def _():
        issue(0, 0)

    @pl.when(i + 1 < n_steps)
    def _():
        issue(i + 1, 1 - slot)

    def drain(t, carry):
        for k in range(TOP_K):
            row_copy(i, slot, t, k).wait()
        return carry

    lax.fori_loop(0, tq, drain, 0, unroll=DMA_ISSUE_UNROLL)
    gates = gate_ref[...]
    words = [buf[slot, k] for k in range(TOP_K)]
    hi_mask = jnp.uint32(0xFFFF0000)
    slabs = []
    for c in range(0, words[0].shape[1], pair // 2):
        for unpack in (lambda w: w << 16, lambda w: w & hi_mask):
            parts = [gates[:, k:k + 1] * pltpu.bitcast(unpack(words[k][:, c:c + pair // 2]), F32)
                     for k in range(TOP_K)]
            slabs.append((parts[0] + parts[1]) + (parts[2] + parts[3]))
    out = h_ref[...] + jnp.concatenate(slabs, axis=-1)
    o_ref[...] = out
    a_ref[...] = _rmsnorm_body(out, g_ref[...]).astype(a_ref.dtype)


def combine(y, pos, h, gates, g_next, *, pair, tq=COMBINE_TQ):
    t, d = h.shape
    rows = lambda i, p: (i, 0)
    grid_spec = pltpu.PrefetchScalarGridSpec(
        num_scalar_prefetch=1,
        grid=(t // tq,),
        in_specs=[pl.BlockSpec(memory_space=pl.ANY),
                  pl.BlockSpec((tq, d), rows),
                  pl.BlockSpec((tq, V7X_LANES), rows),
                  pl.BlockSpec((1, d), lambda i, p: (0, 0))],
        out_specs=[pl.BlockSpec((tq, d), rows), pl.BlockSpec((tq, d), rows)],
        scratch_shapes=[pltpu.VMEM((2, TOP_K, tq, d // 2), U32), pltpu.SemaphoreType.DMA((2,))],
    )
    return pl.pallas_call(
        functools.partial(_combine_kernel, tq=tq, n_steps=t // tq, pair=pair),
        grid_spec=grid_spec,
        out_shape=[jax.ShapeDtypeStruct((t, d), F32), jax.ShapeDtypeStruct((t, d), BF16)],
        compiler_params=_cparams(("arbitrary",)),
        name="combine",
    )(pos, y, h, gates, g_next.reshape(1, d))


def _ple_kernel(a_ref, w_ref, p_ref, wp_ref, h_ref, o_ref):
    gate = jax.nn.sigmoid(jnp.dot(a_ref[...], w_ref[...], preferred_element_type=F32))
    emb = jnp.dot(p_ref[...], wp_ref[...], preferred_element_type=F32)
    o_ref[...] = h_ref[...] + gate * emb


def _ple_final_kernel(a_ref, w_ref, p_ref, wp_ref, h_ref, g_ref, o_ref, *, tn):
    j = pl.program_id(1)
    gate = jax.nn.sigmoid(jnp.dot(a_ref[...], w_ref[...], preferred_element_type=F32))
    emb = jnp.dot(p_ref[...], wp_ref[...], preferred_element_type=F32)
    o_ref[:, pl.ds(pl.multiple_of(j * tn, tn), tn)] = h_ref[...] + gate * emb

    @pl.when(j == pl.num_programs(1) - 1)
    def _():
        o_ref[...] = _rmsnorm_body(o_ref[...], g_ref[...])


def ple(a, w_gate, p, w_proj, h, final_g=None, *, tm=MM_TM // 2, tn=MM_TN):
    t, k = a.shape
    n = w_gate.shape[1]
    pk = p.shape[1]
    if final_g is not None:
        tn = tn // 2
    tm, tn = min(tm, t), min(tn, n)
    in_specs = [pl.BlockSpec((tm, k), lambda i, j: (i, 0)),
                pl.BlockSpec((k, tn), lambda i, j: (0, j)),
                pl.BlockSpec((tm, pk), lambda i, j: (i, 0)),
                pl.BlockSpec((pk, tn), lambda i, j: (0, j)),
                pl.BlockSpec((tm, tn), lambda i, j: (i, j))]
    args = (a, w_gate, p, w_proj, h)
    if final_g is None:
        body = _ple_kernel
        out_spec = pl.BlockSpec((tm, tn), lambda i, j: (i, j))
    else:
        body = functools.partial(_ple_final_kernel, tn=tn)
        in_specs.append(pl.BlockSpec((1, n), lambda i, j: (0, 0)))
        args += (final_g.reshape(1, n),)
        out_spec = pl.BlockSpec((tm, n), lambda i, j: (i, 0))
    return pl.pallas_call(
        body,
        grid=(t // tm, n // tn),
        in_specs=in_specs,
        out_specs=out_spec,
        out_shape=jax.ShapeDtypeStruct((t, n), F32),
        compiler_params=_cparams(("parallel", "arbitrary")),
        name="ple",
    )(*args)


def kernel(x, p, positions, attn_norm_g, w_in, conv_w, conv_b, b_igate, b_fgate, mh_norm_g, w_out,
           ffn_norm_g, w_router, b_router, w_gate_up, b_gate_up, w_down, b_down, ple_norm_g, w_ple_gate,
           w_ple_proj, final_norm_g):
    batch, seq, d = x.shape
    depth = p.shape[0]
    t = batch * seq
    n_main = 3 * ATTN_WIDTH + 2 * MLSTM_QK_WIDTH + 2 * MLSTM_V_WIDTH
    nh = MLSTM_HEADS
    h = x.reshape(t, d)
    cos, sin = rope_tables(positions.reshape(t, 1))
    for i in range(depth):
        a = rmsnorm(h, attn_norm_g[i], BF16)
        w_gates = jnp.zeros((d, V7X_LANES), BF16).at[:, :2 * nh].set(w_in[i][:, n_main:].astype(BF16))
        z, gates = in_projection(a, w_in[i].astype(BF16), n_main, w_gates, cos, sin)
        attn = dilated_attention(z, batch, seq)
        gate_bias = jnp.zeros((1, V7X_LANES), F32).at[0, :nh].set(b_igate[i]).at[0, nh:2 * nh].set(b_fgate[i])
        ml = mlstm(z, gates, gate_bias, conv_w[i], conv_b[i], mh_norm_g[i], batch, seq)
        h = out_projection(attn, ml, w_out[i].astype(BF16), h)
        fpk, top_e, top_g, rank, counts = ffn_norm_router(h, ffn_norm_g[i], w_router[i], b_router[i])
        pos, zero_blk, layout = routing_tables(top_e[:, :TOP_K], rank[:, :TOP_K],
                                               counts[0, :N_EXPERTS].astype(I32), t)
        xs = dispatch(fpk, pos, zero_blk)
        act = expert_up(xs, w_gate_up[i], b_gate_up[i].reshape(N_EXPERTS, 1, 2 * EXPERT_FF),
                        resident_schedule(layout, EXPERT_FF // EXPERT_TF))
        tn = min(EXPERT_TN, d)
        y = expert_down(act, w_down[i], b_down[i].reshape(N_EXPERTS, 1, d),
                        resident_schedule(layout, d // tn), tn=tn)
        h, a2 = combine(y, pos, h, top_g, ple_norm_g[i], pair=tn)
        h = ple(a2, w_ple_gate[i].astype(BF16), p[i].reshape(t, -1).astype(BF16), w_ple_proj[i].astype(BF16), h,
                final_norm_g if i == depth - 1 else None)
    return h.astype(x.dtype).reshape(batch, seq, d)
```

```python
import functools

import jax
import jax.numpy as jnp
from jax import lax
from jax.experimental import pallas as pl
from jax.experimental.pallas import tpu as pltpu

F32 = jnp.float32
BF16 = jnp.bfloat16
U32 = jnp.uint32
I32 = jnp.int32

V7X_VMEM_BYTES = 64 * 1024 * 1024
V7X_LANES = 128
VMEM_LIMIT = V7X_VMEM_BYTES - 8 * 1024 * 1024

ATTN_HEADS = 16
HEAD_DIM = 128
ATTN_WIDTH = ATTN_HEADS * HEAD_DIM
ATTN_BLOCK = 128
DILATIONS = (1, 4, 16)
ATTN_SUPER = ATTN_BLOCK * max(DILATIONS)
ROPE_THETA = 10000.0
MLSTM_HEADS = 4
MLSTM_QK_DIM = 256
MLSTM_V_DIM = 512
MLSTM_QK_WIDTH = MLSTM_HEADS * MLSTM_QK_DIM
MLSTM_V_WIDTH = MLSTM_HEADS * MLSTM_V_DIM
MLSTM_CHUNK = 256
CONV_WIDTH = 4
CONV_PAD = 8
N_EXPERTS = 32
TOP_K = 4
EXPERT_FF = 1536
SWIGLU_LIMIT = 7.0
SWIGLU_ALPHA = 1.702
NORM_EPS = 1e-6
NEG_BIG = -1e30

MM_TM = 1024
MM_TN = 1024
NORM_TM = 256
EXPERT_TM = 512
EXPERT_TF = 512
EXPERT_TN = 2048
COMBINE_PLE_TM = 256
DMA_ISSUE_UNROLL = 4


def _cparams(sem, **kw):
    return pltpu.CompilerParams(dimension_semantics=sem, vmem_limit_bytes=VMEM_LIMIT, **kw)


def _rmsnorm_body(x, g):
    r = lax.rsqrt(jnp.mean(x * x, axis=-1, keepdims=True) + NORM_EPS)
    return x * r * g


def _rmsnorm_kernel(x_ref, g_ref, o_ref):
    o_ref[...] = _rmsnorm_body(x_ref[...], g_ref[...]).astype(o_ref.dtype)


def rmsnorm(x, g, out_dtype):
    t, d = x.shape
    return pl.pallas_call(
        _rmsnorm_kernel,
        grid=(t // NORM_TM,),
        in_specs=[pl.BlockSpec((NORM_TM, d), lambda i: (i, 0)),
                  pl.BlockSpec((1, d), lambda i: (0, 0))],
        out_specs=pl.BlockSpec((NORM_TM, d), lambda i: (i, 0)),
        out_shape=jax.ShapeDtypeStruct((t, d), out_dtype),
        compiler_params=_cparams(("parallel",)),
        name="rmsnorm",
    )(x, g.reshape(1, d))


def _rope_table_kernel(pos_ref, freq_ref, sign_ref, cos_ref, sin_ref):
    ang = pos_ref[...].astype(F32) * freq_ref[...]
    cos_ref[...] = jnp.cos(ang)
    sin_ref[...] = jnp.sin(ang) * sign_ref[...]


def rope_tables(positions_col):
    t = positions_col.shape[0]
    half = HEAD_DIM // 2
    inv_freq = jnp.power(ROPE_THETA, -jnp.arange(half, dtype=F32) / half)
    freq = jnp.concatenate([inv_freq, inv_freq]).reshape(1, HEAD_DIM)
    sign = jnp.concatenate([-jnp.ones((half,), F32), jnp.ones((half,), F32)]).reshape(1, HEAD_DIM)
    tm = 512
    return pl.pallas_call(
        _rope_table_kernel,
        grid=(t // tm,),
        in_specs=[pl.BlockSpec((tm, 1), lambda i: (i, 0)),
                  pl.BlockSpec((1, HEAD_DIM), lambda i: (0, 0)),
                  pl.BlockSpec((1, HEAD_DIM), lambda i: (0, 0))],
        out_specs=[pl.BlockSpec((tm, HEAD_DIM), lambda i: (i, 0))] * 2,
        out_shape=[jax.ShapeDtypeStruct((t, HEAD_DIM), F32)] * 2,
        compiler_params=_cparams(("parallel",)),
        name="rope_tables",
    )(positions_col, freq, sign)


def _inproj_kernel(a_ref, w_ref, wg_ref, cos_ref, sin_ref, z_ref, gate_ref, *, n_q_tiles, n_k_tiles,
                   q_scale):
    j = pl.program_id(1)

    @pl.when(j == 0)
    def _():
        gate_ref[...] = jnp.dot(a_ref[...], wg_ref[...], preferred_element_type=F32)

    def rope(scale):
        cos = cos_ref[...]
        sin = sin_ref[...]
        pair = 2 * HEAD_DIM
        for c0 in range(0, z_ref.shape[1], pair):
            acc = jnp.dot(a_ref[...], w_ref[:, c0:c0 + pair], preferred_element_type=F32)
            for c in range(c0, c0 + pair, HEAD_DIM):
                t = acc[:, c - c0:c - c0 + HEAD_DIM]
                r = pltpu.roll(t, HEAD_DIM // 2, axis=1)
                z_ref[:, c:c + HEAD_DIM] = ((t * cos + r * sin) * scale).astype(z_ref.dtype)

    @pl.when(j < n_q_tiles)
    def _():
        rope(q_scale)

    @pl.when((j >= n_q_tiles) & (j < n_q_tiles + n_k_tiles))
    def _():
        rope(1.0)

    @pl.when(j >= n_q_tiles + n_k_tiles)
    def _():
        z_ref[...] = jnp.dot(a_ref[...], w_ref[...], preferred_element_type=F32).astype(z_ref.dtype)


def in_projection(a, w, n, w_gates, cos, sin, *, tm=MM_TM, tn=MM_TN):
    t, k = a.shape
    tm, tn = min(tm, t), min(tn, n)
    kern = functools.partial(_inproj_kernel, n_q_tiles=ATTN_WIDTH // tn, n_k_tiles=ATTN_WIDTH // tn,
                             q_scale=HEAD_DIM ** -0.5)
    return pl.pallas_call(
        kern,
        grid=(t // tm, n // tn),
        in_specs=[pl.BlockSpec((tm, k), lambda i, j: (i, 0)),
                  pl.BlockSpec((k, tn), lambda i, j: (0, j)),
                  pl.BlockSpec((k, V7X_LANES), lambda i, j: (0, 0)),
                  pl.BlockSpec((tm, HEAD_DIM), lambda i, j: (i, 0)),
                  pl.BlockSpec((tm, HEAD_DIM), lambda i, j: (i, 0))],
        out_specs=[pl.BlockSpec((tm, tn), lambda i, j: (i, j)),
                   pl.BlockSpec((tm, V7X_LANES), lambda i, j: (i, 0))],
        out_shape=[jax.ShapeDtypeStruct((t, n), BF16),
                   jax.ShapeDtypeStruct((t, V7X_LANES), F32)],
        compiler_params=_cparams(("parallel", "arbitrary")),
        name="in_projection",
    )(a, w, w_gates, cos, sin)


def _band_block(qb, kb, vb, bias):
    s = lax.dot_general(qb, kb, (((1,), (1,)), ((), ())), preferred_element_type=F32) + bias
    m = jnp.max(s, axis=-1, keepdims=True)
    p = jnp.exp(s - m)
    l = jnp.sum(p, axis=-1, keepdims=True)
    o = jnp.dot(p.astype(BF16), vb, preferred_element_type=F32) / l
    return o, m + jnp.log(l)


def _attn_kernel(q_ref, kp_ref, kc_ref, vp_ref, vc_ref, o_ref, qf, kf, vf, ob, lb):
    sb = ATTN_SUPER
    blk = ATTN_BLOCK
    n = pl.program_id(2)
    qf[...] = q_ref[...].astype(F32)
    kf[0:sb, :] = kp_ref[...].astype(F32)
    kf[sb:2 * sb, :] = kc_ref[...].astype(F32)
    vf[0:sb, :] = vp_ref[...].astype(F32)
    vf[sb:2 * sb, :] = vc_ref[...].astype(F32)

    qi = lax.broadcasted_iota(I32, (blk, 2 * blk), 0)
    kj = lax.broadcasted_iota(I32, (blk, 2 * blk), 1)
    dist = blk + qi - kj
    band = (dist >= 0) & (dist <= blk)
    bias_in = jnp.where(band, 0.0, NEG_BIG).astype(F32)
    prev_ok = jnp.where(n > 0, 0.0, NEG_BIG).astype(F32)
    bias_edge = bias_in + jnp.where(kj < blk, prev_ok, 0.0)

    for p, d in enumerate(DILATIONS):
        per_class = sb // (d * blk)
        for r in range(d):
            for nb in range(per_class):
                q0 = r + d * blk * nb
                k0 = sb + q0 - d * blk
                if d == 1:
                    qb = qf[pl.ds(q0, blk), :]
                    kb = kf[pl.ds(k0, 2 * blk), :]
                    vb = vf[pl.ds(k0, 2 * blk), :]
                else:
                    qb = qf[pl.ds(q0, blk, stride=d), :]
                    kb = kf[pl.ds(k0, 2 * blk, stride=d), :]
                    vb = vf[pl.ds(k0, 2 * blk, stride=d), :]
                bias = bias_edge if nb == 0 else bias_in
                o, lse = _band_block(qb.astype(BF16), kb.astype(BF16), vb.astype(BF16), bias)
                lse_b = jnp.broadcast_to(lse, (blk, HEAD_DIM))
                if d == 1:
                    ob[p, pl.ds(q0, blk), :] = o
                    lb[p, pl.ds(q0, blk), :] = lse_b
                else:
                    ob[p, pl.ds(q0, blk, stride=d), :] = o
                    lb[p, pl.ds(q0, blk, stride=d), :] = lse_b

    l0, l1, l2 = lb[0], lb[1], lb[2]
    mx = jnp.maximum(jnp.maximum(l0, l1), l2)
    w0, w1, w2 = jnp.exp(l0 - mx), jnp.exp(l1 - mx), jnp.exp(l2 - mx)
    out = (w0 * ob[0] + w1 * ob[1] + w2 * ob[2]) / (w0 + w1 + w2)
    o_ref[...] = out.astype(o_ref.dtype)


def dilated_attention(z, batch, seq):
    t = z.shape[0]
    sb = ATTN_SUPER
    nsb = seq // sb
    h = ATTN_HEADS
    cur = lambda off: (lambda b, hh, n: (b * nsb + n, off + hh))
    prev = lambda off: (lambda b, hh, n: (b * nsb + jnp.maximum(n - 1, 0), off + hh))
    spec = lambda im: pl.BlockSpec((sb, HEAD_DIM), im)
    return pl.pallas_call(
        _attn_kernel,
        grid=(batch, h, nsb),
        in_specs=[spec(cur(0)), spec(prev(h)), spec(cur(h)), spec(prev(2 * h)), spec(cur(2 * h))],
        out_specs=spec(cur(0)),
        out_shape=jax.ShapeDtypeStruct((t, ATTN_WIDTH), BF16),
        scratch_shapes=[pltpu.VMEM((sb, HEAD_DIM), F32),
                        pltpu.VMEM((2 * sb, HEAD_DIM), F32),
                        pltpu.VMEM((2 * sb, HEAD_DIM), F32),
                        pltpu.VMEM((len(DILATIONS), sb, HEAD_DIM), F32),
                        pltpu.VMEM((len(DILATIONS), sb, HEAD_DIM), F32)],
        compiler_params=_cparams(("parallel", "parallel", "arbitrary")),
        name="dilated_attention",
    )(z, z, z, z, z)


def _log_sigmoid(x):
    return jnp.minimum(x, 0.0) - jnp.log(1.0 + jnp.exp(-jnp.abs(x)))


def _mlstm_kernel(q_ref, k_ref, v_ref, om_ref, g_ref, gb_ref, cw_ref, cb_ref, mhg_ref, o_ref,
                  xbuf, c_st, n_st, m_st):
    L = MLSTM_CHUNK
    dqk, dv, nh = MLSTM_QK_DIM, MLSTM_V_DIM, MLSTM_HEADS
    qw = nh * dqk
    c = pl.program_id(1)

    @pl.when(c == 0)
    def _():
        xbuf[0:CONV_PAD, :] = jnp.zeros((CONV_PAD, 2 * qw), F32)
        c_st[...] = jnp.zeros_like(c_st)
        n_st[...] = jnp.zeros_like(n_st)
        m_st[...] = jnp.zeros_like(m_st)

    @pl.when(c > 0)
    def _():
        xbuf[0:CONV_PAD, :] = xbuf[L:L + CONV_PAD, :]

    xbuf[CONV_PAD:CONV_PAD + L, 0:qw] = q_ref[...].astype(F32)
    xbuf[CONV_PAD:CONV_PAD + L, qw:2 * qw] = k_ref[...].astype(F32)
    y = cb_ref[...]
    for j in range(CONV_WIDTH):
        y = y + cw_ref[j:j + 1, :] * xbuf[pl.ds(CONV_PAD - CONV_WIDTH + 1 + j, L), :]
    qk = y * jax.nn.sigmoid(y)

    pre = g_ref[...] + gb_ref[...]
    lf = _log_sigmoid(pre)
    row = lax.broadcasted_iota(I32, (L, L), 0)
    col = lax.broadcasted_iota(I32, (L, L), 1)
    causal = col <= row
    tril = jnp.where(causal, 1.0, 0.0).astype(F32)
    bcum = jnp.dot(tril, lf, preferred_element_type=F32, precision=lax.Precision.HIGHEST)
    pre_t = pre.T
    bcum_t = bcum.T

    for h in range(nh):
        i_c = pre[:, h:h + 1]
        b_c = bcum[:, nh + h:nh + h + 1]
        i_r = pre_t[h:h + 1, :]
        b_r = bcum_t[nh + h:nh + h + 1, :]
        g = bcum[L - 1:L, nh + h:nh + h + 1]
        m = m_st[h, 0:1, 0:1]
        qq = (qk[:, h * dqk:(h + 1) * dqk] * (dqk ** -0.5))
        kk = qk[:, qw + h * dqk:qw + (h + 1) * dqk]
        vv = v_ref[:, h * dv:(h + 1) * dv]
        qb = qq.astype(BF16)

        log_d = jnp.where(causal, b_c - b_r + i_r, NEG_BIG)
        inter = b_c + m
        m_t = jnp.maximum(inter, jnp.max(log_d, axis=-1, keepdims=True))
        w_inter = jnp.exp(inter - m_t)
        s = lax.dot_general(qb, kk.astype(BF16), (((1,), (1,)), ((), ())),
                            preferred_element_type=F32) * jnp.exp(log_d - m_t)
        num = (w_inter * jnp.dot(qb, c_st[h].astype(BF16), preferred_element_type=F32)
               + jnp.dot(s.astype(BF16), vv, preferred_element_type=F32))
        den = (w_inter * jnp.sum(qq * n_st[h], axis=-1, keepdims=True)
               + jnp.sum(s, axis=-1, keepdims=True))
        hh = num / jnp.maximum(jnp.abs(den), jnp.exp(-m_t))

        log_w = g - b_c + i_c
        m_new = jnp.maximum(g + m, jnp.max(log_w, axis=0, keepdims=True))
        decay = jnp.exp(g + m - m_new)
        wk = kk * jnp.exp(log_w - m_new)
        c_st[h] = decay * c_st[h] + jnp.dot(wk.T.astype(BF16), vv, preferred_element_type=F32)
        n_st[h] = decay * n_st[h] + jnp.sum(wk, axis=0, keepdims=True)
        m_st[h] = jnp.broadcast_to(m_new, m_st.shape[1:])

        hn = hh * lax.rsqrt(jnp.mean(hh * hh, axis=-1, keepdims=True) + NORM_EPS)
        hn = hn * mhg_ref[:, h * dv:(h + 1) * dv]
        og = jax.nn.sigmoid(om_ref[:, h * dv:(h + 1) * dv].astype(F32))
        o_ref[:, h * dv:(h + 1) * dv] = (og * hn).astype(o_ref.dtype)


def mlstm(z, gates, gate_bias, conv_w, conv_b, mh_norm_g, batch, seq):
    t = z.shape[0]
    L = MLSTM_CHUNK
    nc = seq // L
    qw, vw = MLSTM_QK_WIDTH, MLSTM_V_WIDTH
    q_off = 3 * ATTN_WIDTH
    rows = lambda blk: (lambda b, c: (b * nc + c, blk))
    const = lambda b, c: (0, 0)
    return pl.pallas_call(
        _mlstm_kernel,
        grid=(batch, nc),
        in_specs=[pl.BlockSpec((L, qw), rows(q_off // qw)),
                  pl.BlockSpec((L, qw), rows(q_off // qw + 1)),
                  pl.BlockSpec((L, vw), rows((q_off + 2 * qw) // vw)),
                  pl.BlockSpec((L, vw), rows((q_off + 2 * qw) // vw + 1)),
                  pl.BlockSpec((L, V7X_LANES), rows(0)),
                  pl.BlockSpec((1, V7X_LANES), const),
                  pl.BlockSpec((CONV_WIDTH, 2 * qw), const),
                  pl.BlockSpec((1, 2 * qw), const),
                  pl.BlockSpec((1, vw), const)],
        out_specs=pl.BlockSpec((L, vw), rows(0)),
        out_shape=jax.ShapeDtypeStruct((t, vw), BF16),
        scratch_shapes=[pltpu.VMEM((L + CONV_PAD, 2 * qw), F32),
                        pltpu.VMEM((MLSTM_HEADS, MLSTM_QK_DIM, MLSTM_V_DIM), F32),
                        pltpu.VMEM((MLSTM_HEADS, 1, MLSTM_QK_DIM), F32),
                        pltpu.VMEM((MLSTM_HEADS, 8, V7X_LANES), F32)],
        compiler_params=_cparams(("parallel", "arbitrary")),
        name="mlstm",
    )(z, z, z, z, gates, gate_bias, conv_w, conv_b.reshape(1, 2 * qw), mh_norm_g.reshape(1, vw))


def _outproj_kernel(a1_ref, a2_ref, w_ref, res_ref, o_ref):
    k1 = a1_ref.shape[1]
    acc = jnp.dot(a1_ref[...], w_ref[0:k1, :], preferred_element_type=F32)
    acc = acc + jnp.dot(a2_ref[...], w_ref[k1:, :], preferred_element_type=F32)
    o_ref[...] = res_ref[...] + acc


def out_projection(a1, a2, w, res, *, tm=MM_TM, tn=MM_TN):
    t, k1 = a1.shape
    k2 = a2.shape[1]
    n = w.shape[1]
    tm, tn = min(tm, t), min(tn, n)
    return pl.pallas_call(
        _outproj_kernel,
        grid=(t // tm, n // tn),
        in_specs=[pl.BlockSpec((tm, k1), lambda i, j: (i, 0)),
                  pl.BlockSpec((tm, k2), lambda i, j: (i, 0)),
                  pl.BlockSpec((k1 + k2, tn), lambda i, j: (0, j)),
                  pl.BlockSpec((tm, tn), lambda i, j: (i, j))],
        out_specs=pl.BlockSpec((tm, tn), lambda i, j: (i, j)),
        out_shape=jax.ShapeDtypeStruct((t, n), F32),
        compiler_params=_cparams(("parallel", "arbitrary")),
        name="out_projection",
    )(a1, a2, w, res)


def _pack_bf16_pairs(f):
    half = f.shape[1] // 2
    fb = f.astype(BF16).astype(F32)
    lo = pltpu.bitcast(fb[:, :half], U32) >> 16
    hi = pltpu.bitcast(fb[:, half:], U32) & jnp.uint32(0xFFFF0000)
    return lo | hi


def _unpack_bf16_pairs(w):
    lo = pltpu.bitcast(w << 16, F32).astype(BF16)
    hi = pltpu.bitcast(w & jnp.uint32(0xFFFF0000), F32).astype(BF16)
    return lo, hi


def _router_kernel(h_ref, g_ref, wr_ref, br_ref, fpk_ref, e_ref, gate_ref, rank_ref, cnt_ref, carry):
    i = pl.program_id(0)

    @pl.when(i == 0)
    def _():
        carry[...] = jnp.zeros_like(carry)

    f = _rmsnorm_body(h_ref[...], g_ref[...])
    fpk_ref[...] = _pack_bf16_pairs(f)
    f_hi = f.astype(BF16)
    f_lo = (f - f_hi.astype(F32)).astype(BF16)
    logits = (jnp.dot(f_hi, wr_ref[0], preferred_element_type=F32)
              + (jnp.dot(f_lo, wr_ref[0], preferred_element_type=F32)
                 + jnp.dot(f_hi, wr_ref[1], preferred_element_type=F32))) + br_ref[...]
    lane = lax.broadcasted_iota(I32, logits.shape, 1)
    cur = logits
    vals, idxs = [], []
    for _ in range(TOP_K):
        mx = jnp.max(cur, axis=-1, keepdims=True)
        idx = jnp.min(jnp.where(cur == mx, lane, V7X_LANES), axis=-1, keepdims=True)
        vals.append(mx)
        idxs.append(idx)
        cur = jnp.where(lane == idx, -jnp.inf, cur)
    exps = [jnp.exp(v - vals[0]) for v in vals]
    tot = exps[0] + exps[1] + exps[2] + exps[3]
    e_out = jnp.zeros(logits.shape, I32)
    g_out = jnp.zeros(logits.shape, F32)
    for k in range(TOP_K):
        e_out = jnp.where(lane == k, idxs[k], e_out)
        g_out = jnp.where(lane == k, exps[k] / tot, g_out)
    e_ref[...] = e_out
    gate_ref[...] = g_out

    tm = logits.shape[0]
    onehots = [lane == idxs[k] for k in range(TOP_K)]
    member = jnp.zeros(logits.shape, F32)
    for k in range(TOP_K):
        member = jnp.where(onehots[k], 1.0, member)
    row = lax.broadcasted_iota(I32, (tm, tm), 0)
    col = lax.broadcasted_iota(I32, (tm, tm), 1)
    stril = jnp.where(col < row, 1.0, 0.0).astype(BF16)
    before = carry[0:1, :] + jnp.dot(stril, member.astype(BF16), preferred_element_type=F32)
    r_out = jnp.zeros(logits.shape, I32)
    for k in range(TOP_K):
        rk = jnp.sum(jnp.where(onehots[k], before, 0.0), axis=-1, keepdims=True)
        r_out = jnp.where(lane == k, rk.astype(I32), r_out)
    rank_ref[...] = r_out
    total = carry[0:1, :] + jnp.sum(member, axis=0, keepdims=True)
    carry[...] = jnp.broadcast_to(total, carry.shape)
    cnt_ref[...] = jnp.broadcast_to(total, cnt_ref.shape)


def ffn_norm_router(h, g, w_router, b_router, *, tm=NORM_TM):
    t, d = h.shape
    nt = t // tm
    wr = jnp.zeros((d, V7X_LANES), F32).at[:, :N_EXPERTS].set(w_router)
    wr_hi = wr.astype(BF16)
    wr = jnp.stack([wr_hi, (wr - wr_hi.astype(F32)).astype(BF16)])
    br = jnp.full((1, V7X_LANES), NEG_BIG, F32).at[0, :N_EXPERTS].set(b_router)
    rows = lambda i: (i, 0)
    const = lambda i: (0, 0)
    return pl.pallas_call(
        _router_kernel,
        grid=(nt,),
        in_specs=[pl.BlockSpec((tm, d), rows),
                  pl.BlockSpec((1, d), const),
                  pl.BlockSpec((2, d, V7X_LANES), lambda i: (0, 0, 0)),
                  pl.BlockSpec((1, V7X_LANES), const)],
        out_specs=[pl.BlockSpec((tm, d // 2), rows),
                   pl.BlockSpec((tm, V7X_LANES), rows),
                   pl.BlockSpec((tm, V7X_LANES), rows),
                   pl.BlockSpec((tm, V7X_LANES), rows),
                   pl.BlockSpec((8, V7X_LANES), const)],
        out_shape=[jax.ShapeDtypeStruct((t, d // 2), U32),
                   jax.ShapeDtypeStruct((t, V7X_LANES), I32),
                   jax.ShapeDtypeStruct((t, V7X_LANES), F32),
                   jax.ShapeDtypeStruct((t, V7X_LANES), I32),
                   jax.ShapeDtypeStruct((8, V7X_LANES), F32)],
        scratch_shapes=[pltpu.VMEM((8, V7X_LANES), F32)],
        compiler_params=_cparams(("arbitrary",)),
        name="ffn_norm_router",
    )(h, g.reshape(1, d), wr, br)


def routing_tables(top_e, rank, counts, n_tokens):
    tk = n_tokens * TOP_K
    tm = EXPERT_TM
    n_blk = -(-(tk + N_EXPERTS * (tm - 1)) // tm)
    padded = (counts + tm - 1) // tm * tm
    pend = jnp.cumsum(padded)
    pstart = pend - padded
    pos = (pstart[top_e] + rank).reshape(tk).astype(I32)
    blk_start = jnp.arange(n_blk, dtype=I32) * tm
    blk_e = jnp.minimum(jnp.sum(blk_start[:, None] >= pend[None, :], axis=1), N_EXPERTS - 1).astype(I32)
    n_used = (pend[-1] // tm).astype(I32).reshape(1)
    is_last = jnp.any((blk_start[:, None] + tm == pend[None, :]) & (padded[None, :] > 0), axis=1)
    zero_blk = (is_last | (blk_start >= pend[-1])).astype(I32)
    return pos, zero_blk, (blk_e, n_used[0], pstart // tm, padded // tm, n_blk, counts)


SCHED_ACTIVE = 1
SCHED_FIRST = 2
SCHED_PREFETCH = 4
SCHED_HALF = 8


def resident_schedule(layout, n_col):
    blk_e, n_used, first_blk, n_blks, n_blk, counts = layout
    s = jnp.arange(n_blk * n_col, dtype=I32)
    n_active = n_col * n_used
    active = s < n_active
    sc = jnp.minimum(s, jnp.maximum(n_active - 1, 0))
    e = blk_e[jnp.minimum(sc // n_col, n_blk - 1)]
    b0 = first_blk[e]
    nb = jnp.maximum(n_blks[e], 1)
    local = sc - n_col * b0
    col = local // nb
    j = local % nb
    blk = b0 + j
    out_blk = jnp.where(active, blk, s // n_col)
    out_col = jnp.where(active, col, s % n_col)
    nxt = sc - j + nb
    has_next = nxt < n_active
    nxt = jnp.minimum(nxt, jnp.maximum(n_active - 1, 0))
    half_empty = counts[e] - j * EXPERT_TM <= EXPERT_TM // 2
    flags = jnp.where(active, SCHED_ACTIVE + SCHED_HALF * half_empty
                      + (j == 0) * (SCHED_FIRST + SCHED_PREFETCH * has_next), 0)
    as_i32 = lambda a: a.astype(I32)
    return tuple(map(as_i32, (blk, out_blk, out_col, e, col, flags, e[nxt], col[nxt])))


def _dispatch_kernel(pos_ref, zero_ref, f_ref, xs_hbm, zbuf, sem, zsem, *, rows, n_blk):
    i = pl.program_id(0)
    tm = f_ref.shape[0]

    def zero_copy(b):
        return pltpu.make_async_copy(zbuf, xs_hbm.at[pl.ds(b * rows, rows)], zsem)

    @pl.when(i == 0)
    def _():
        zbuf[...] = jnp.zeros_like(zbuf)

        def issue(b, c):
            @pl.when(zero_ref[b] != 0)
            def _():
                zero_copy(b).start()
            return c

        def drain(b, c):
            @pl.when(zero_ref[b] != 0)
            def _():
                zero_copy(b).wait()
            return c

        lax.fori_loop(0, n_blk, issue, 0)
        lax.fori_loop(0, n_blk, drain, 0)

    def row_copy(t, k):
        row = pos_ref[(i * tm + t) * TOP_K + k]
        return pltpu.make_async_copy(f_ref.at[pl.ds(t, 1)], xs_hbm.at[pl.ds(row, 1)], sem)

    def issue_rows(t, c):
        for k in range(TOP_K):
            row_copy(t, k).start()
        return c

    def drain_rows(t, c):
        for k in range(TOP_K):
            row_copy(t, k).wait()
        return c

    lax.fori_loop(0, tm, issue_rows, 0, unroll=DMA_ISSUE_UNROLL)
    lax.fori_loop(0, tm, drain_rows, 0, unroll=DMA_ISSUE_UNROLL)


def dispatch(fpk, pos, zero_blk, *, rows=EXPERT_TM, tm=NORM_TM):
    t, half = fpk.shape
    n_blk = zero_blk.shape[0]
    grid_spec = pltpu.PrefetchScalarGridSpec(
        num_scalar_prefetch=2,
        grid=(t // tm,),
        in_specs=[pl.BlockSpec((tm, half), lambda i, p, z: (i, 0))],
        out_specs=pl.BlockSpec(memory_space=pl.ANY),
        scratch_shapes=[pltpu.VMEM((rows, half), fpk.dtype), pltpu.SemaphoreType.DMA(()),
                        pltpu.SemaphoreType.DMA(())],
    )
    return pl.pallas_call(
        functools.partial(_dispatch_kernel, rows=rows, n_blk=n_blk),
        grid_spec=grid_spec,
        out_shape=jax.ShapeDtypeStruct((n_blk * rows, half), fpk.dtype),
        compiler_params=_cparams(("arbitrary",), has_side_effects=True),
        name="dispatch",
    )(pos, zero_blk, fpk)


def _weight_tile_refresh(s, flag, tile_copies, cast):
    @pl.when((flag & SCHED_FIRST) != 0)
    def _():
        @pl.when(s == 0)
        def _():
            for cp in tile_copies(False):
                cp.start()

        for cp in tile_copies(False):
            cp.wait()
        cast()

        @pl.when((flag & SCHED_PREFETCH) != 0)
        def _():
            for cp in tile_copies(True):
                cp.start()


def _expert_up_kernel(xb_ref, ob_ref, oc_ref, se_ref, sc_ref, flag_ref, ne_ref, nc_ref, xs_ref, w_hbm, bg_ref,
                      bu_ref, act_ref, wstage, wgb, wub, sem, *, tf, nf):
    s = pl.program_id(0)
    half = xs_ref.shape[1]
    flag = flag_ref[s]

    def tile_copies(following):
        e = (ne_ref if following else se_ref)[s]
        c = (nc_ref if following else sc_ref)[s]
        gate_cols = pl.ds(pl.multiple_of(c * tf, tf), tf)
        up_cols = pl.ds(pl.multiple_of((nf + c) * tf, tf), tf)
        return (pltpu.make_async_copy(w_hbm.at[e, :, gate_cols], wstage.at[0], sem.at[0]),
                pltpu.make_async_copy(w_hbm.at[e, :, up_cols], wstage.at[1], sem.at[1]))

    def cast():
        wgb[...] = wstage[0].astype(BF16)
        wub[...] = wstage[1].astype(BF16)

    _weight_tile_refresh(s, flag, tile_copies, cast)

    @pl.when((flag & SCHED_ACTIVE) != 0)
    def _():
        def rows_compute(rows):
            lo, hi = _unpack_bf16_pairs(xs_ref[0:rows, :])
            glu = (jnp.dot(lo, wgb[0:half, :], preferred_element_type=F32)
                   + jnp.dot(hi, wgb[half:, :], preferred_element_type=F32) + bg_ref[...])
            up = (jnp.dot(lo, wub[0:half, :], preferred_element_type=F32)
                  + jnp.dot(hi, wub[half:, :], preferred_element_type=F32) + bu_ref[...])
            glu = jnp.minimum(glu, SWIGLU_LIMIT)
            up = jnp.clip(up, -SWIGLU_LIMIT, SWIGLU_LIMIT)
            act = (up + 1.0) * (glu * jax.nn.sigmoid(SWIGLU_ALPHA * glu))
            act_ref[0:rows, :] = act.astype(act_ref.dtype)

        tm = xs_ref.shape[0]

        @pl.when((flag & SCHED_HALF) == 0)
        def _():
            rows_compute(tm)

        @pl.when((flag & SCHED_HALF) != 0)
        def _():
            rows_compute(tm // 2)
            act_ref[tm // 2:, :] = jnp.zeros((tm - tm // 2, act_ref.shape[1]), act_ref.dtype)

    @pl.when((flag & SCHED_ACTIVE) == 0)
    def _():
        act_ref[...] = jnp.zeros_like(act_ref)


def expert_up(xs, w_gate_up, b_gate_up, sched, *, tm=EXPERT_TM, tf=EXPERT_TF):
    n_rows, half = xs.shape
    d = 2 * half
    ff = EXPERT_FF
    nf = ff // tf
    n_steps = sched[0].shape[0]
    grid_spec = pltpu.PrefetchScalarGridSpec(
        num_scalar_prefetch=len(sched),
        grid=(n_steps,),
        in_specs=[pl.BlockSpec((tm, half), lambda s, *t: (t[0][s], 0)),
                  pl.BlockSpec(memory_space=pl.ANY),
                  pl.BlockSpec((None, 1, tf), lambda s, *t: (t[3][s], 0, t[4][s])),
                  pl.BlockSpec((None, 1, tf), lambda s, *t: (t[3][s], 0, nf + t[4][s]))],
        out_specs=pl.BlockSpec((tm, tf), lambda s, *t: (t[1][s], t[2][s])),
        scratch_shapes=[pltpu.VMEM((2, d, tf), F32), pltpu.VMEM((d, tf), BF16), pltpu.VMEM((d, tf), BF16),
                        pltpu.SemaphoreType.DMA((2,))],
    )
    return pl.pallas_call(
        functools.partial(_expert_up_kernel, tf=tf, nf=nf),
        grid_spec=grid_spec,
        out_shape=jax.ShapeDtypeStruct((n_rows, ff), BF16),
        compiler_params=_cparams(("arbitrary",)),
        name="expert_up",
    )(*sched, xs, w_gate_up, b_gate_up, b_gate_up)


def _expert_down_kernel(xb_ref, ob_ref, oc_ref, se_ref, sc_ref, flag_ref, ne_ref, nc_ref, act_ref, w_hbm, bd_ref,
                        y_ref, wstage, wdb, sem, *, tn):
    s = pl.program_id(0)
    flag = flag_ref[s]

    def tile_copies(following):
        e = (ne_ref if following else se_ref)[s]
        c = (nc_ref if following else sc_ref)[s]
        cols = pl.ds(pl.multiple_of(c * tn, tn), tn)
        return (pltpu.make_async_copy(w_hbm.at[e, :, cols], wstage, sem),)

    def cast():
        wdb[...] = wstage[...].astype(BF16)

    _weight_tile_refresh(s, flag, tile_copies, cast)

    @pl.when((flag & SCHED_ACTIVE) != 0)
    def _():
        def rows_compute(rows):
            y = jnp.dot(act_ref[0:rows, :], wdb[...], preferred_element_type=F32) + bd_ref[...]
            y_ref[0:rows, :] = _pack_bf16_pairs(y)

        tm = act_ref.shape[0]

        @pl.when((flag & SCHED_HALF) == 0)
        def _():
            rows_compute(tm)

        @pl.when((flag & SCHED_HALF) != 0)
        def _():
            rows_compute(tm // 2)
            y_ref[tm // 2:, :] = jnp.zeros((tm - tm // 2, y_ref.shape[1]), y_ref.dtype)

    @pl.when((flag & SCHED_ACTIVE) == 0)
    def _():
        y_ref[...] = jnp.zeros_like(y_ref)


def expert_down(act, w_down, b_down, sched, *, tm=EXPERT_TM, tn=EXPERT_TN):
    n_rows, ff = act.shape
    d = w_down.shape[2]
    n_steps = sched[0].shape[0]
    grid_spec = pltpu.PrefetchScalarGridSpec(
        num_scalar_prefetch=len(sched),
        grid=(n_steps,),
        in_specs=[pl.BlockSpec((tm, ff), lambda s, *t: (t[0][s], 0)),
                  pl.BlockSpec(memory_space=pl.ANY),
                  pl.BlockSpec((None, 1, tn), lambda s, *t: (t[3][s], 0, t[4][s]))],
        out_specs=pl.BlockSpec((tm, tn // 2), lambda s, *t: (t[1][s], t[2][s])),
        scratch_shapes=[pltpu.VMEM((ff, tn), F32), pltpu.VMEM((ff, tn), BF16), pltpu.SemaphoreType.DMA(())],
    )
    return pl.pallas_call(
        functools.partial(_expert_down_kernel, tn=tn),
        grid_spec=grid_spec,
        out_shape=jax.ShapeDtypeStruct((n_rows, d // 2), U32),
        compiler_params=_cparams(("arbitrary",)),
        name="expert_down",
    )(*sched, act, w_down, b_down)


def _combine_ple_kernel(pos_ref, y_hbm, h_ref, gate_ref, gn_ref, w_ref, p_ref, wp_ref, gf_ref, o_ref, gbuf, a2, sem,
                        *, tm, tn, pair, final):
    i = pl.program_id(0)
    j = pl.program_id(1)
    d = o_ref.shape[1]

    def row_copy(blk, t, k):
        row = pos_ref[(blk * tm + t) * TOP_K + k]
        return pltpu.make_async_copy(y_hbm.at[pl.ds(row, 1)], gbuf.at[k, pl.ds(t, 1)], sem)

    def issue(blk):
        def body(t, carry):
            for k in range(TOP_K):
                row_copy(blk, t, k).start()
            return carry

        lax.fori_loop(0, tm, body, 0, unroll=DMA_ISSUE_UNROLL)

    @pl.when(j == 0)
    def _():
        @pl.when(i == 0)
        def _():
            issue(0)

        def drain(t, carry):
            for k in range(TOP_K):
                row_copy(i, t, k).wait()
            return carry

        lax.fori_loop(0, tm, drain, 0, unroll=DMA_ISSUE_UNROLL)
        gates = gate_ref[...]
        hi_mask = jnp.uint32(0xFFFF0000)
        half = pair // 2
        ssq = jnp.zeros((tm, 1), F32)
        for c in range(d // pair):
            for part, unpack in enumerate((lambda w: w << 16, lambda w: w & hi_mask)):
                terms = [gates[:, k:k + 1] * pltpu.bitcast(unpack(gbuf[k, :, c * half:(c + 1) * half]), F32)
                         for k in range(TOP_K)]
                cols = slice(c * pair + part * half, c * pair + (part + 1) * half)
                val = h_ref[:, cols] + ((terms[0] + terms[1]) + (terms[2] + terms[3]))
                o_ref[:, cols] = val
                ssq = ssq + jnp.sum(val * val, axis=-1, keepdims=True)
        r = lax.rsqrt(ssq / d + NORM_EPS)
        for c0 in range(0, d, half):
            a2[:, c0:c0 + half] = (o_ref[:, c0:c0 + half] * r * gn_ref[:, c0:c0 + half]).astype(a2.dtype)

        @pl.when(i + 1 < pl.num_programs(0))
        def _():
            issue(i + 1)

    gate = jax.nn.sigmoid(jnp.dot(a2[...], w_ref[...], preferred_element_type=F32))
    emb = jnp.dot(p_ref[...], wp_ref[...], preferred_element_type=F32)
    cols = pl.ds(pl.multiple_of(j * tn, tn), tn)
    o_ref[:, cols] = o_ref[:, cols] + gate * emb

    if final:
        @pl.when(j == pl.num_programs(1) - 1)
        def _():
            o_ref[...] = _rmsnorm_body(o_ref[...], gf_ref[...])


def combine_ple(y, pos, h, gates, g_ple, w_gate, p, w_proj, g_final, *, pair, tm=COMBINE_PLE_TM, tn=MM_TN):
    t, d = h.shape
    pk = p.shape[1]
    tm, tn = min(tm, t), min(tn, d)
    final = g_final is not None
    g_last = (g_final if final else jnp.ones((d,), F32)).reshape(1, d)
    rows = lambda i, j, pos: (i, 0)
    const = lambda i, j, pos: (0, 0)
    colt = lambda i, j, pos: (0, j)
    grid_spec = pltpu.PrefetchScalarGridSpec(
        num_scalar_prefetch=1,
        grid=(t // tm, d // tn),
        in_specs=[pl.BlockSpec(memory_space=pl.ANY),
                  pl.BlockSpec((tm, d), rows),
                  pl.BlockSpec((tm, V7X_LANES), rows),
                  pl.BlockSpec((1, d), const),
                  pl.BlockSpec((d, tn), colt),
                  pl.BlockSpec((tm, pk), rows),
                  pl.BlockSpec((pk, tn), colt),
                  pl.BlockSpec((1, d), const)],
        out_specs=pl.BlockSpec((tm, d), rows),
        scratch_shapes=[pltpu.VMEM((TOP_K, tm, d // 2), U32), pltpu.VMEM((tm, d), BF16), pltpu.SemaphoreType.DMA(())],
    )
    return pl.pallas_call(
        functools.partial(_combine_ple_kernel, tm=tm, tn=tn, pair=pair, final=final),
        grid_spec=grid_spec,
        out_shape=jax.ShapeDtypeStruct((t, d), F32),
        compiler_params=_cparams(("arbitrary", "arbitrary")),
        name="combine_ple",
    )(pos, y, h, gates, g_ple.reshape(1, d), w_gate, p, w_proj, g_last)


def kernel(x, p, positions, attn_norm_g, w_in, conv_w, conv_b, b_igate, b_fgate, mh_norm_g, w_out,
           ffn_norm_g, w_router, b_router, w_gate_up, b_gate_up, w_down, b_down, ple_norm_g, w_ple_gate,
           w_ple_proj, final_norm_g):
    batch, seq, d = x.shape
    depth = p.shape[0]
    t = batch * seq
    n_main = 3 * ATTN_WIDTH + 2 * MLSTM_QK_WIDTH + 2 * MLSTM_V_WIDTH
    nh = MLSTM_HEADS
    h = x.reshape(t, d)
    cos, sin = rope_tables(positions.reshape(t, 1))
    for i in range(depth):
        a = rmsnorm(h, attn_norm_g[i], BF16)
        w_gates = jnp.zeros((d, V7X_LANES), BF16).at[:, :2 * nh].set(w_in[i][:, n_main:].astype(BF16))
        z, gates = in_projection(a, w_in[i].astype(BF16), n_main, w_gates, cos, sin)
        attn = dilated_attention(z, batch, seq)
        gate_bias = jnp.zeros((1, V7X_LANES), F32).at[0, :nh].set(b_igate[i]).at[0, nh:2 * nh].set(b_fgate[i])
        ml = mlstm(z, gates, gate_bias, conv_w[i], conv_b[i], mh_norm_g[i], batch, seq)
        h = out_projection(attn, ml, w_out[i].astype(BF16), h)
        fpk, top_e, top_g, rank, counts = ffn_norm_router(h, ffn_norm_g[i], w_router[i], b_router[i])
        pos, zero_blk, layout = routing_tables(top_e[:, :TOP_K], rank[:, :TOP_K],
                                               counts[0, :N_EXPERTS].astype(I32), t)
        xs = dispatch(fpk, pos, zero_blk)
        act = expert_up(xs, w_gate_up[i], b_gate_up[i].reshape(N_EXPERTS, 1, 2 * EXPERT_FF),
                        resident_schedule(layout, EXPERT_FF // EXPERT_TF))
        tn = min(EXPERT_TN, d)
        y = expert_down(act, w_down[i], b_down[i].reshape(N_EXPERTS, 1, d),
                        resident_schedule(layout, d // tn), tn=tn)
        h = combine_ple(y, pos, h, top_g, ple_norm_g[i], w_ple_gate[i].astype(BF16),
                        p[i].reshape(t, -1).astype(BF16), w_ple_proj[i].astype(BF16),
                        final_norm_g if i == depth - 1 else None, pair=tn)
    return h.astype(x.dtype).reshape(batch, seq, d)
```

```python
import functools

import jax
import jax.numpy as jnp
from jax import lax
from jax.experimental import pallas as pl
from jax.experimental.pallas import tpu as pltpu

F32 = jnp.float32
BF16 = jnp.bfloat16
U32 = jnp.uint32
I32 = jnp.int32

V7X_VMEM_BYTES = 64 * 1024 * 1024
V7X_LANES = 128
VMEM_LIMIT = V7X_VMEM_BYTES - 8 * 1024 * 1024

ATTN_HEADS = 16
HEAD_DIM = 128
ATTN_WIDTH = ATTN_HEADS * HEAD_DIM
ATTN_BLOCK = 128
DILATIONS = (1, 4, 16)
ATTN_SUPER = ATTN_BLOCK * max(DILATIONS)
ROPE_THETA = 10000.0
MLSTM_HEADS = 4
MLSTM_QK_DIM = 256
MLSTM_V_DIM = 512
MLSTM_QK_WIDTH = MLSTM_HEADS * MLSTM_QK_DIM
MLSTM_V_WIDTH = MLSTM_HEADS * MLSTM_V_DIM
MLSTM_CHUNK = 256
CONV_WIDTH = 4
CONV_PAD = 8
N_EXPERTS = 32
TOP_K = 4
EXPERT_FF = 1536
SWIGLU_LIMIT = 7.0
SWIGLU_ALPHA = 1.702
NORM_EPS = 1e-6
NEG_BIG = -1e30

MM_TM = 1024
MM_TN = 1024
NORM_TM = 256
EXPERT_TM = 512
EXPERT_TF = 512
EXPERT_TN = 2048
COMBINE_PLE_TM = 256
DMA_ISSUE_UNROLL = 4


def _cparams(sem, **kw):
    return pltpu.CompilerParams(dimension_semantics=sem, vmem_limit_bytes=VMEM_LIMIT, **kw)


def _rmsnorm_body(x, g):
    r = lax.rsqrt(jnp.mean(x * x, axis=-1, keepdims=True) + NORM_EPS)
    return x * r * g


def _rmsnorm_kernel(x_ref, g_ref, o_ref):
    o_ref[...] = _rmsnorm_body(x_ref[...], g_ref[...]).astype(o_ref.dtype)


def rmsnorm(x, g, out_dtype):
    t, d = x.shape
    return pl.pallas_call(
        _rmsnorm_kernel,
        grid=(t // NORM_TM,),
        in_specs=[pl.BlockSpec((NORM_TM, d), lambda i: (i, 0)),
                  pl.BlockSpec((1, d), lambda i: (0, 0))],
        out_specs=pl.BlockSpec((NORM_TM, d), lambda i: (i, 0)),
        out_shape=jax.ShapeDtypeStruct((t, d), out_dtype),
        compiler_params=_cparams(("parallel",)),
        name="rmsnorm",
    )(x, g.reshape(1, d))


def _rope_table_kernel(pos_ref, freq_ref, sign_ref, cos_ref, sin_ref):
    ang = pos_ref[...].astype(F32) * freq_ref[...]
    cos_ref[...] = jnp.cos(ang)
    sin_ref[...] = jnp.sin(ang) * sign_ref[...]


def rope_tables(positions_col):
    t = positions_col.shape[0]
    half = HEAD_DIM // 2
    inv_freq = jnp.power(ROPE_THETA, -jnp.arange(half, dtype=F32) / half)
    freq = jnp.concatenate([inv_freq, inv_freq]).reshape(1, HEAD_DIM)
    sign = jnp.concatenate([-jnp.ones((half,), F32), jnp.ones((half,), F32)]).reshape(1, HEAD_DIM)
    tm = 512
    return pl.pallas_call(
        _rope_table_kernel,
        grid=(t // tm,),
        in_specs=[pl.BlockSpec((tm, 1), lambda i: (i, 0)),
                  pl.BlockSpec((1, HEAD_DIM), lambda i: (0, 0)),
                  pl.BlockSpec((1, HEAD_DIM), lambda i: (0, 0))],
        out_specs=[pl.BlockSpec((tm, HEAD_DIM), lambda i: (i, 0))] * 2,
        out_shape=[jax.ShapeDtypeStruct((t, HEAD_DIM), F32)] * 2,
        compiler_params=_cparams(("parallel",)),
        name="rope_tables",
    )(positions_col, freq, sign)


def _inproj_kernel(a_ref, w_ref, wg_ref, cos_ref, sin_ref, z_ref, gate_ref, *, n_q_tiles, n_k_tiles,
                   q_scale):
    j = pl.program_id(1)

    @pl.when(j == 0)
    def _():
        gate_ref[...] = jnp.dot(a_ref[...], wg_ref[...], preferred_element_type=F32)

    def rope(scale):
        cos = cos_ref[...]
        sin = sin_ref[...]
        pair = 2 * HEAD_DIM
        for c0 in range(0, z_ref.shape[1], pair):
            acc = jnp.dot(a_ref[...], w_ref[:, c0:c0 + pair], preferred_element_type=F32)
            for c in range(c0, c0 + pair, HEAD_DIM):
                t = acc[:, c - c0:c - c0 + HEAD_DIM]
                r = pltpu.roll(t, HEAD_DIM // 2, axis=1)
                z_ref[:, c:c + HEAD_DIM] = ((t * cos + r * sin) * scale).astype(z_ref.dtype)

    @pl.when(j < n_q_tiles)
    def _():
        rope(q_scale)

    @pl.when((j >= n_q_tiles) & (j < n_q_tiles + n_k_tiles))
    def _():
        rope(1.0)

    @pl.when(j >= n_q_tiles + n_k_tiles)
    def _():
        z_ref[...] = jnp.dot(a_ref[...], w_ref[...], preferred_element_type=F32).astype(z_ref.dtype)


def in_projection(a, w, n, w_gates, cos, sin, *, tm=MM_TM, tn=MM_TN):
    t, k = a.shape
    tm, tn = min(tm, t), min(tn, n)
    kern = functools.partial(_inproj_kernel, n_q_tiles=ATTN_WIDTH // tn, n_k_tiles=ATTN_WIDTH // tn,
                             q_scale=HEAD_DIM ** -0.5)
    return pl.pallas_call(
        kern,
        grid=(t // tm, n // tn),
        in_specs=[pl.BlockSpec((tm, k), lambda i, j: (i, 0)),
                  pl.BlockSpec((k, tn), lambda i, j: (0, j)),
                  pl.BlockSpec((k, V7X_LANES), lambda i, j: (0, 0)),
                  pl.BlockSpec((tm, HEAD_DIM), lambda i, j: (i, 0)),
                  pl.BlockSpec((tm, HEAD_DIM), lambda i, j: (i, 0))],
        out_specs=[pl.BlockSpec((tm, tn), lambda i, j: (i, j)),
                   pl.BlockSpec((tm, V7X_LANES), lambda i, j: (i, 0))],
        out_shape=[jax.ShapeDtypeStruct((t, n), BF16),
                   jax.ShapeDtypeStruct((t, V7X_LANES), F32)],
        compiler_params=_cparams(("parallel", "arbitrary")),
        name="in_projection",
    )(a, w, w_gates, cos, sin)


def _band_block(qb, kb, vb, bias):
    s = lax.dot_general(qb, kb, (((1,), (1,)), ((), ())), preferred_element_type=F32) + bias
    m = jnp.max(s, axis=-1, keepdims=True)
    p = jnp.exp(s - m)
    l = jnp.sum(p, axis=-1, keepdims=True)
    o = jnp.dot(p.astype(BF16), vb, preferred_element_type=F32) / l
    return o, m + jnp.log(l)


def _attn_kernel(q_ref, kp_ref, kc_ref, vp_ref, vc_ref, o_ref, qf, kf, vf, ob, lb):
    sb = ATTN_SUPER
    blk = ATTN_BLOCK
    n = pl.program_id(2)
    qf[...] = q_ref[...].astype(F32)
    kf[0:sb, :] = kp_ref[...].astype(F32)
    kf[sb:2 * sb, :] = kc_ref[...].astype(F32)
    vf[0:sb, :] = vp_ref[...].astype(F32)
    vf[sb:2 * sb, :] = vc_ref[...].astype(F32)

    qi = lax.broadcasted_iota(I32, (blk, 2 * blk), 0)
    kj = lax.broadcasted_iota(I32, (blk, 2 * blk), 1)
    dist = blk + qi - kj
    band = (dist >= 0) & (dist <= blk)
    bias_in = jnp.where(band, 0.0, NEG_BIG).astype(F32)
    prev_ok = jnp.where(n > 0, 0.0, NEG_BIG).astype(F32)
    bias_edge = bias_in + jnp.where(kj < blk, prev_ok, 0.0)

    for p, d in enumerate(DILATIONS):
        per_class = sb // (d * blk)
        for r in range(d):
            for nb in range(per_class):
                q0 = r + d * blk * nb
                k0 = sb + q0 - d * blk
                if d == 1:
                    qb = qf[pl.ds(q0, blk), :]
                    kb = kf[pl.ds(k0, 2 * blk), :]
                    vb = vf[pl.ds(k0, 2 * blk), :]
                else:
                    qb = qf[pl.ds(q0, blk, stride=d), :]
                    kb = kf[pl.ds(k0, 2 * blk, stride=d), :]
                    vb = vf[pl.ds(k0, 2 * blk, stride=d), :]
                bias = bias_edge if nb == 0 else bias_in
                o, lse = _band_block(qb.astype(BF16), kb.astype(BF16), vb.astype(BF16), bias)
                lse_b = jnp.broadcast_to(lse, (blk, HEAD_DIM))
                if d == 1:
                    ob[p, pl.ds(q0, blk), :] = o
                    lb[p, pl.ds(q0, blk), :] = lse_b
                else:
                    ob[p, pl.ds(q0, blk, stride=d), :] = o
                    lb[p, pl.ds(q0, blk, stride=d), :] = lse_b

    l0, l1, l2 = lb[0], lb[1], lb[2]
    mx = jnp.maximum(jnp.maximum(l0, l1), l2)
    w0, w1, w2 = jnp.exp(l0 - mx), jnp.exp(l1 - mx), jnp.exp(l2 - mx)
    out = (w0 * ob[0] + w1 * ob[1] + w2 * ob[2]) / (w0 + w1 + w2)
    o_ref[...] = out.astype(o_ref.dtype)


def dilated_attention(z, batch, seq):
    t = z.shape[0]
    sb = ATTN_SUPER
    nsb = seq // sb
    h = ATTN_HEADS
    cur = lambda off: (lambda b, hh, n: (b * nsb + n, off + hh))
    prev = lambda off: (lambda b, hh, n: (b * nsb + jnp.maximum(n - 1, 0), off + hh))
    spec = lambda im: pl.BlockSpec((sb, HEAD_DIM), im)
    return pl.pallas_call(
        _attn_kernel,
        grid=(batch, h, nsb),
        in_specs=[spec(cur(0)), spec(prev(h)), spec(cur(h)), spec(prev(2 * h)), spec(cur(2 * h))],
        out_specs=spec(cur(0)),
        out_shape=jax.ShapeDtypeStruct((t, ATTN_WIDTH), BF16),
        scratch_shapes=[pltpu.VMEM((sb, HEAD_DIM), F32),
                        pltpu.VMEM((2 * sb, HEAD_DIM), F32),
                        pltpu.VMEM((2 * sb, HEAD_DIM), F32),
                        pltpu.VMEM((len(DILATIONS), sb, HEAD_DIM), F32),
                        pltpu.VMEM((len(DILATIONS), sb, HEAD_DIM), F32)],
        compiler_params=_cparams(("parallel", "parallel", "arbitrary")),
        name="dilated_attention",
    )(z, z, z, z, z)


def _log_sigmoid(x):
    return jnp.minimum(x, 0.0) - jnp.log(1.0 + jnp.exp(-jnp.abs(x)))


def _mlstm_kernel(q_ref, k_ref, v_ref, om_ref, g_ref, gb_ref, cw_ref, cb_ref, mhg_ref, o_ref,
                  xbuf, c_st, n_st, m_st):
    L = MLSTM_CHUNK
    dqk, dv, nh = MLSTM_QK_DIM, MLSTM_V_DIM, MLSTM_HEADS
    qw = nh * dqk
    c = pl.program_id(1)

    @pl.when(c == 0)
    def _():
        xbuf[0:CONV_PAD, :] = jnp.zeros((CONV_PAD, 2 * qw), F32)
        c_st[...] = jnp.zeros_like(c_st)
        n_st[...] = jnp.zeros_like(n_st)
        m_st[...] = jnp.zeros_like(m_st)

    @pl.when(c > 0)
    def _():
        xbuf[0:CONV_PAD, :] = xbuf[L:L + CONV_PAD, :]

    xbuf[CONV_PAD:CONV_PAD + L, 0:qw] = q_ref[...].astype(F32)
    xbuf[CONV_PAD:CONV_PAD + L, qw:2 * qw] = k_ref[...].astype(F32)
    y = cb_ref[...]
    for j in range(CONV_WIDTH):
        y = y + cw_ref[j:j + 1, :] * xbuf[pl.ds(CONV_PAD - CONV_WIDTH + 1 + j, L), :]
    qk = y * jax.nn.sigmoid(y)

    pre = g_ref[...] + gb_ref[...]
    lf = _log_sigmoid(pre)
    row = lax.broadcasted_iota(I32, (L, L), 0)
    col = lax.broadcasted_iota(I32, (L, L), 1)
    causal = col <= row
    tril = jnp.where(causal, 1.0, 0.0).astype(F32)
    bcum = jnp.dot(tril, lf, preferred_element_type=F32, precision=lax.Precision.HIGHEST)
    pre_t = pre.T
    bcum_t = bcum.T

    for h in range(nh):
        i_c = pre[:, h:h + 1]
        b_c = bcum[:, nh + h:nh + h + 1]
        i_r = pre_t[h:h + 1, :]
        b_r = bcum_t[nh + h:nh + h + 1, :]
        g = bcum[L - 1:L, nh + h:nh + h + 1]
        m = m_st[h, 0:1, 0:1]
        qq = (qk[:, h * dqk:(h + 1) * dqk] * (dqk ** -0.5))
        kk = qk[:, qw + h * dqk:qw + (h + 1) * dqk]
        vv = v_ref[:, h * dv:(h + 1) * dv]
        qb = qq.astype(BF16)

        log_d = jnp.where(causal, b_c - b_r + i_r, NEG_BIG)
        inter = b_c + m
        m_t = jnp.maximum(inter, jnp.max(log_d, axis=-1, keepdims=True))
        w_inter = jnp.exp(inter - m_t)
        s = lax.dot_general(qb, kk.astype(BF16), (((1,), (1,)), ((), ())),
                            preferred_element_type=F32) * jnp.exp(log_d - m_t)
        num = (w_inter * jnp.dot(qb, c_st[h].astype(BF16), preferred_element_type=F32)
               + jnp.dot(s.astype(BF16), vv, preferred_element_type=F32))
        den = (w_inter * jnp.sum(qq * n_st[h], axis=-1, keepdims=True)
               + jnp.sum(s, axis=-1, keepdims=True))
        hh = num / jnp.maximum(jnp.abs(den), jnp.exp(-m_t))

        log_w = g - b_c + i_c
        m_new = jnp.maximum(g + m, jnp.max(log_w, axis=0, keepdims=True))
        decay = jnp.exp(g + m - m_new)
        wk = kk * jnp.exp(log_w - m_new)
        c_st[h] = decay * c_st[h] + jnp.dot(wk.T.astype(BF16), vv, preferred_element_type=F32)
        n_st[h] = decay * n_st[h] + jnp.sum(wk, axis=0, keepdims=True)
        m_st[h] = jnp.broadcast_to(m_new, m_st.shape[1:])

        hn = hh * lax.rsqrt(jnp.mean(hh * hh, axis=-1, keepdims=True) + NORM_EPS)
        hn = hn * mhg_ref[:, h * dv:(h + 1) * dv]
        og = jax.nn.sigmoid(om_ref[:, h * dv:(h + 1) * dv].astype(F32))
        o_ref[:, h * dv:(h + 1) * dv] = (og * hn).astype(o_ref.dtype)


def mlstm(z, gates, gate_bias, conv_w, conv_b, mh_norm_g, batch, seq):
    t = z.shape[0]
    L = MLSTM_CHUNK
    nc = seq // L
    qw, vw = MLSTM_QK_WIDTH, MLSTM_V_WIDTH
    q_off = 3 * ATTN_WIDTH
    rows = lambda blk: (lambda b, c: (b * nc + c, blk))
    const = lambda b, c: (0, 0)
    return pl.pallas_call(
        _mlstm_kernel,
        grid=(batch, nc),
        in_specs=[pl.BlockSpec((L, qw), rows(q_off // qw)),
                  pl.BlockSpec((L, qw), rows(q_off // qw + 1)),
                  pl.BlockSpec((L, vw), rows((q_off + 2 * qw) // vw)),
                  pl.BlockSpec((L, vw), rows((q_off + 2 * qw) // vw + 1)),
                  pl.BlockSpec((L, V7X_LANES), rows(0)),
                  pl.BlockSpec((1, V7X_LANES), const),
                  pl.BlockSpec((CONV_WIDTH, 2 * qw), const),
                  pl.BlockSpec((1, 2 * qw), const),
                  pl.BlockSpec((1, vw), const)],
        out_specs=pl.BlockSpec((L, vw), rows(0)),
        out_shape=jax.ShapeDtypeStruct((t, vw), BF16),
        scratch_shapes=[pltpu.VMEM((L + CONV_PAD, 2 * qw), F32),
                        pltpu.VMEM((MLSTM_HEADS, MLSTM_QK_DIM, MLSTM_V_DIM), F32),
                        pltpu.VMEM((MLSTM_HEADS, 1, MLSTM_QK_DIM), F32),
                        pltpu.VMEM((MLSTM_HEADS, 8, V7X_LANES), F32)],
        compiler_params=_cparams(("parallel", "arbitrary")),
        name="mlstm",
    )(z, z, z, z, gates, gate_bias, conv_w, conv_b.reshape(1, 2 * qw), mh_norm_g.reshape(1, vw))


def _outproj_kernel(a1_ref, a2_ref, w_ref, res_ref, o_ref):
    k1 = a1_ref.shape[1]
    acc = jnp.dot(a1_ref[...], w_ref[0:k1, :], preferred_element_type=F32)
    acc = acc + jnp.dot(a2_ref[...], w_ref[k1:, :], preferred_element_type=F32)
    o_ref[...] = res_ref[...] + acc


def out_projection(a1, a2, w, res, *, tm=MM_TM, tn=MM_TN):
    t, k1 = a1.shape
    k2 = a2.shape[1]
    n = w.shape[1]
    tm, tn = min(tm, t), min(tn, n)
    return pl.pallas_call(
        _outproj_kernel,
        grid=(t // tm, n // tn),
        in_specs=[pl.BlockSpec((tm, k1), lambda i, j: (i, 0)),
                  pl.BlockSpec((tm, k2), lambda i, j: (i, 0)),
                  pl.BlockSpec((k1 + k2, tn), lambda i, j: (0, j)),
                  pl.BlockSpec((tm, tn), lambda i, j: (i, j))],
        out_specs=pl.BlockSpec((tm, tn), lambda i, j: (i, j)),
        out_shape=jax.ShapeDtypeStruct((t, n), F32),
        compiler_params=_cparams(("parallel", "arbitrary")),
        name="out_projection",
    )(a1, a2, w, res)


def _pack_bf16_pairs(f):
    half = f.shape[1] // 2
    fb = f.astype(BF16).astype(F32)
    lo = pltpu.bitcast(fb[:, :half], U32) >> 16
    hi = pltpu.bitcast(fb[:, half:], U32) & jnp.uint32(0xFFFF0000)
    return lo | hi


def _unpack_bf16_pairs(w):
    lo = pltpu.bitcast(w << 16, F32).astype(BF16)
    hi = pltpu.bitcast(w & jnp.uint32(0xFFFF0000), F32).astype(BF16)
    return lo, hi


def _router_kernel(h_ref, g_ref, wr_ref, br_ref, fpk_ref, e_ref, gate_ref, rank_ref, cnt_ref, carry):
    i = pl.program_id(0)

    @pl.when(i == 0)
    def _():
        carry[...] = jnp.zeros_like(carry)

    f = _rmsnorm_body(h_ref[...], g_ref[...])
    fpk_ref[...] = _pack_bf16_pairs(f)
    f_hi = f.astype(BF16)
    f_lo = (f - f_hi.astype(F32)).astype(BF16)
    logits = (jnp.dot(f_hi, wr_ref[0], preferred_element_type=F32)
              + (jnp.dot(f_lo, wr_ref[0], preferred_element_type=F32)
                 + jnp.dot(f_hi, wr_ref[1], preferred_element_type=F32))) + br_ref[...]
    lane = lax.broadcasted_iota(I32, logits.shape, 1)
    cur = logits
    vals, idxs = [], []
    for _ in range(TOP_K):
        mx = jnp.max(cur, axis=-1, keepdims=True)
        idx = jnp.min(jnp.where(cur == mx, lane, V7X_LANES), axis=-1, keepdims=True)
        vals.append(mx)
        idxs.append(idx)
        cur = jnp.where(lane == idx, -jnp.inf, cur)
    exps = [jnp.exp(v - vals[0]) for v in vals]
    tot = exps[0] + exps[1] + exps[2] + exps[3]
    e_out = jnp.zeros(logits.shape, I32)
    g_out = jnp.zeros(logits.shape, F32)
    for k in range(TOP_K):
        e_out = jnp.where(lane == k, idxs[k], e_out)
        g_out = jnp.where(lane == k, exps[k] / tot, g_out)
    e_ref[...] = e_out
    gate_ref[...] = g_out

    tm = logits.shape[0]
    onehots = [lane == idxs[k] for k in range(TOP_K)]
    member = jnp.zeros(logits.shape, F32)
    for k in range(TOP_K):
        member = jnp.where(onehots[k], 1.0, member)
    row = lax.broadcasted_iota(I32, (tm, tm), 0)
    col = lax.broadcasted_iota(I32, (tm, tm), 1)
    stril = jnp.where(col < row, 1.0, 0.0).astype(BF16)
    before = carry[0:1, :] + jnp.dot(stril, member.astype(BF16), preferred_element_type=F32)
    r_out = jnp.zeros(logits.shape, I32)
    for k in range(TOP_K):
        rk = jnp.sum(jnp.where(onehots[k], before, 0.0), axis=-1, keepdims=True)
        r_out = jnp.where(lane == k, rk.astype(I32), r_out)
    rank_ref[...] = r_out
    total = carry[0:1, :] + jnp.sum(member, axis=0, keepdims=True)
    carry[...] = jnp.broadcast_to(total, carry.shape)
    cnt_ref[...] = jnp.broadcast_to(total, cnt_ref.shape)


def ffn_norm_router(h, g, w_router, b_router, *, tm=NORM_TM):
    t, d = h.shape
    nt = t // tm
    wr = jnp.zeros((d, V7X_LANES), F32).at[:, :N_EXPERTS].set(w_router)
    wr_hi = wr.astype(BF16)
    wr = jnp.stack([wr_hi, (wr - wr_hi.astype(F32)).astype(BF16)])
    br = jnp.full((1, V7X_LANES), NEG_BIG, F32).at[0, :N_EXPERTS].set(b_router)
    rows = lambda i: (i, 0)
    const = lambda i: (0, 0)
    return pl.pallas_call(
        _router_kernel,
        grid=(nt,),
        in_specs=[pl.BlockSpec((tm, d), rows),
                  pl.BlockSpec((1, d), const),
                  pl.BlockSpec((2, d, V7X_LANES), lambda i: (0, 0, 0)),
                  pl.BlockSpec((1, V7X_LANES), const)],
        out_specs=[pl.BlockSpec((tm, d // 2), rows),
                   pl.BlockSpec((tm, V7X_LANES), rows),
                   pl.BlockSpec((tm, V7X_LANES), rows),
                   pl.BlockSpec((tm, V7X_LANES), rows),
                   pl.BlockSpec((8, V7X_LANES), const)],
        out_shape=[jax.ShapeDtypeStruct((t, d // 2), U32),
                   jax.ShapeDtypeStruct((t, V7X_LANES), I32),
                   jax.ShapeDtypeStruct((t, V7X_LANES), F32),
                   jax.ShapeDtypeStruct((t, V7X_LANES), I32),
                   jax.ShapeDtypeStruct((8, V7X_LANES), F32)],
        scratch_shapes=[pltpu.VMEM((8, V7X_LANES), F32)],
        compiler_params=_cparams(("arbitrary",)),
        name="ffn_norm_router",
    )(h, g.reshape(1, d), wr, br)


def routing_tables(top_e, rank, counts, n_tokens):
    tk = n_tokens * TOP_K
    tm = EXPERT_TM
    n_blk = -(-(tk + N_EXPERTS * (tm - 1)) // tm)
    padded = (counts + tm - 1) // tm * tm
    pend = jnp.cumsum(padded)
    pstart = pend - padded
    pos = (pstart[top_e] + rank).reshape(tk).astype(I32)
    blk_start = jnp.arange(n_blk, dtype=I32) * tm
    blk_e = jnp.minimum(jnp.sum(blk_start[:, None] >= pend[None, :], axis=1), N_EXPERTS - 1).astype(I32)
    n_used = (pend[-1] // tm).astype(I32).reshape(1)
    is_last = jnp.any((blk_start[:, None] + tm == pend[None, :]) & (padded[None, :] > 0), axis=1)
    zero_blk = (is_last | (blk_start >= pend[-1])).astype(I32)
    return pos, zero_blk, (blk_e, n_used[0], pstart // tm, padded // tm, n_blk, counts)


SCHED_ACTIVE = 1
SCHED_FIRST = 2
SCHED_PREFETCH = 4
SCHED_HALF = 8


def resident_schedule(layout, n_col):
    blk_e, n_used, first_blk, n_blks, n_blk, counts = layout
    s = jnp.arange(n_blk * n_col, dtype=I32)
    n_active = n_col * n_used
    active = s < n_active
    sc = jnp.minimum(s, jnp.maximum(n_active - 1, 0))
    e = blk_e[jnp.minimum(sc // n_col, n_blk - 1)]
    b0 = first_blk[e]
    nb = jnp.maximum(n_blks[e], 1)
    local = sc - n_col * b0
    col = local // nb
    j = local % nb
    blk = b0 + j
    out_blk = jnp.where(active, blk, s // n_col)
    out_col = jnp.where(active, col, s % n_col)
    nxt = sc - j + nb
    has_next = nxt < n_active
    nxt = jnp.minimum(nxt, jnp.maximum(n_active - 1, 0))
    half_empty = counts[e] - j * EXPERT_TM <= EXPERT_TM // 2
    flags = jnp.where(active, SCHED_ACTIVE + SCHED_HALF * half_empty
                      + (j == 0) * (SCHED_FIRST + SCHED_PREFETCH * has_next), 0)
    as_i32 = lambda a: a.astype(I32)
    return tuple(map(as_i32, (blk, out_blk, out_col, e, col, flags, e[nxt], col[nxt])))


def _dispatch_kernel(pos_ref, zero_ref, f_ref, xs_hbm, zbuf, sem, zsem, *, rows, n_blk):
    i = pl.program_id(0)
    tm = f_ref.shape[0]

    def zero_copy(b):
        return pltpu.make_async_copy(zbuf, xs_hbm.at[pl.ds(b * rows, rows)], zsem)

    @pl.when(i == 0)
    def _():
        zbuf[...] = jnp.zeros_like(zbuf)

        def issue(b, c):
            @pl.when(zero_ref[b] != 0)
            def _():
                zero_copy(b).start()
            return c

        def drain(b, c):
            @pl.when(zero_ref[b] != 0)
            def _():
                zero_copy(b).wait()
            return c

        lax.fori_loop(0, n_blk, issue, 0)
        lax.fori_loop(0, n_blk, drain, 0)

    def row_copy(t, k):
        row = pos_ref[(i * tm + t) * TOP_K + k]
        return pltpu.make_async_copy(f_ref.at[pl.ds(t, 1)], xs_hbm.at[pl.ds(row, 1)], sem)

    def issue_rows(t, c):
        for k in range(TOP_K):
            row_copy(t, k).start()
        return c

    def drain_rows(t, c):
        for k in range(TOP_K):
            row_copy(t, k).wait()
        return c

    lax.fori_loop(0, tm, issue_rows, 0, unroll=DMA_ISSUE_UNROLL)
    lax.fori_loop(0, tm, drain_rows, 0, unroll=DMA_ISSUE_UNROLL)


def dispatch(fpk, pos, zero_blk, *, rows=EXPERT_TM, tm=NORM_TM):
    t, half = fpk.shape
    n_blk = zero_blk.shape[0]
    grid_spec = pltpu.PrefetchScalarGridSpec(
        num_scalar_prefetch=2,
        grid=(t // tm,),
        in_specs=[pl.BlockSpec((tm, half), lambda i, p, z: (i, 0))],
        out_specs=pl.BlockSpec(memory_space=pl.ANY),
        scratch_shapes=[pltpu.VMEM((rows, half), fpk.dtype), pltpu.SemaphoreType.DMA(()),
                        pltpu.SemaphoreType.DMA(())],
    )
    return pl.pallas_call(
        functools.partial(_dispatch_kernel, rows=rows, n_blk=n_blk),
        grid_spec=grid_spec,
        out_shape=jax.ShapeDtypeStruct((n_blk * rows, half), fpk.dtype),
        compiler_params=_cparams(("arbitrary",), has_side_effects=True),
        name="dispatch",
    )(pos, zero_blk, fpk)


def _weight_tile_refresh(s, flag, tile_copies, cast):
    @pl.when((flag & SCHED_FIRST) != 0)
    def _():
        @pl.when(s == 0)
        def _():
            for cp in tile_copies(False):
                cp.start()

        for cp in tile_copies(False):
            cp.wait()
        cast()

        @pl.when((flag & SCHED_PREFETCH) != 0)
        def _():
            for cp in tile_copies(True):
                cp.start(priority=1)


def _expert_up_kernel(xb_ref, ob_ref, oc_ref, se_ref, sc_ref, flag_ref, ne_ref, nc_ref, xs_ref, w_hbm, bg_ref,
                      bu_ref, act_ref, wstage, wgb, wub, sem, *, tf, nf):
    s = pl.program_id(0)
    half = xs_ref.shape[1]
    flag = flag_ref[s]

    def tile_copies(following):
        e = (ne_ref if following else se_ref)[s]
        c = (nc_ref if following else sc_ref)[s]
        gate_cols = pl.ds(pl.multiple_of(c * tf, tf), tf)
        up_cols = pl.ds(pl.multiple_of((nf + c) * tf, tf), tf)
        return (pltpu.make_async_copy(w_hbm.at[e, :, gate_cols], wstage.at[0], sem.at[0]),
                pltpu.make_async_copy(w_hbm.at[e, :, up_cols], wstage.at[1], sem.at[1]))

    def cast():
        wgb[...] = wstage[0].astype(BF16)
        wub[...] = wstage[1].astype(BF16)

    _weight_tile_refresh(s, flag, tile_copies, cast)

    @pl.when((flag & SCHED_ACTIVE) != 0)
    def _():
        def rows_compute(rows):
            lo, hi = _unpack_bf16_pairs(xs_ref[0:rows, :])
            glu = (jnp.dot(lo, wgb[0:half, :], preferred_element_type=F32)
                   + jnp.dot(hi, wgb[half:, :], preferred_element_type=F32) + bg_ref[...])
            up = (jnp.dot(lo, wub[0:half, :], preferred_element_type=F32)
                  + jnp.dot(hi, wub[half:, :], preferred_element_type=F32) + bu_ref[...])
            glu = jnp.minimum(glu, SWIGLU_LIMIT)
            up = jnp.clip(up, -SWIGLU_LIMIT, SWIGLU_LIMIT)
            act = (up + 1.0) * (glu * jax.nn.sigmoid(SWIGLU_ALPHA * glu))
            act_ref[0:rows, :] = act.astype(act_ref.dtype)

        tm = xs_ref.shape[0]

        @pl.when((flag & SCHED_HALF) == 0)
        def _():
            rows_compute(tm)

        @pl.when((flag & SCHED_HALF) != 0)
        def _():
            rows_compute(tm // 2)
            act_ref[tm // 2:, :] = jnp.zeros((tm - tm // 2, act_ref.shape[1]), act_ref.dtype)

    @pl.when((flag & SCHED_ACTIVE) == 0)
    def _():
        act_ref[...] = jnp.zeros_like(act_ref)


def expert_up(xs, w_gate_up, b_gate_up, sched, *, tm=EXPERT_TM, tf=EXPERT_TF):
    n_rows, half = xs.shape
    d = 2 * half
    ff = EXPERT_FF
    nf = ff // tf
    n_steps = sched[0].shape[0]
    grid_spec = pltpu.PrefetchScalarGridSpec(
        num_scalar_prefetch=len(sched),
        grid=(n_steps,),
        in_specs=[pl.BlockSpec((tm, half), lambda s, *t: (t[0][s], 0)),
                  pl.BlockSpec(memory_space=pl.ANY),
                  pl.BlockSpec((None, 1, tf), lambda s, *t: (t[3][s], 0, t[4][s])),
                  pl.BlockSpec((None, 1, tf), lambda s, *t: (t[3][s], 0, nf + t[4][s]))],
        out_specs=pl.BlockSpec((tm, tf), lambda s, *t: (t[1][s], t[2][s])),
        scratch_shapes=[pltpu.VMEM((2, d, tf), F32), pltpu.VMEM((d, tf), BF16), pltpu.VMEM((d, tf), BF16),
                        pltpu.SemaphoreType.DMA((2,))],
    )
    return pl.pallas_call(
        functools.partial(_expert_up_kernel, tf=tf, nf=nf),
        grid_spec=grid_spec,
        out_shape=jax.ShapeDtypeStruct((n_rows, ff), BF16),
        compiler_params=_cparams(("arbitrary",)),
        name="expert_up",
    )(*sched, xs, w_gate_up, b_gate_up, b_gate_up)


def _expert_down_kernel(xb_ref, ob_ref, oc_ref, se_ref, sc_ref, flag_ref, ne_ref, nc_ref, act_ref, w_hbm, bd_ref,
                        y_ref, wstage, wdb, sem, *, tn):
    s = pl.program_id(0)
    flag = flag_ref[s]

    def tile_copies(following):
        e = (ne_ref if following else se_ref)[s]
        c = (nc_ref if following else sc_ref)[s]
        cols = pl.ds(pl.multiple_of(c * tn, tn), tn)
        return (pltpu.make_async_copy(w_hbm.at[e, :, cols], wstage, sem),)

    def cast():
        wdb[...] = wstage[...].astype(BF16)

    _weight_tile_refresh(s, flag, tile_copies, cast)

    @pl.when((flag & SCHED_ACTIVE) != 0)
    def _():
        def rows_compute(rows):
            y = jnp.dot(act_ref[0:rows, :], wdb[...], preferred_element_type=F32) + bd_ref[...]
            y_ref[0:rows, :] = _pack_bf16_pairs(y)

        tm = act_ref.shape[0]

        @pl.when((flag & SCHED_HALF) == 0)
        def _():
            rows_compute(tm)

        @pl.when((flag & SCHED_HALF) != 0)
        def _():
            rows_compute(tm // 2)
            y_ref[tm // 2:, :] = jnp.zeros((tm - tm // 2, y_ref.shape[1]), y_ref.dtype)

    @pl.when((flag & SCHED_ACTIVE) == 0)
    def _():
        y_ref[...] = jnp.zeros_like(y_ref)


def expert_down(act, w_down, b_down, sched, *, tm=EXPERT_TM, tn=EXPERT_TN):
    n_rows, ff = act.shape
    d = w_down.shape[2]
    n_steps = sched[0].shape[0]
    grid_spec = pltpu.PrefetchScalarGridSpec(
        num_scalar_prefetch=len(sched),
        grid=(n_steps,),
        in_specs=[pl.BlockSpec((tm, ff), lambda s, *t: (t[0][s], 0)),
                  pl.BlockSpec(memory_space=pl.ANY),
                  pl.BlockSpec((None, 1, tn), lambda s, *t: (t[3][s], 0, t[4][s]))],
        out_specs=pl.BlockSpec((tm, tn // 2), lambda s, *t: (t[1][s], t[2][s])),
        scratch_shapes=[pltpu.VMEM((ff, tn), F32), pltpu.VMEM((ff, tn), BF16), pltpu.SemaphoreType.DMA(())],
    )
    return pl.pallas_call(
        functools.partial(_expert_down_kernel, tn=tn),
        grid_spec=grid_spec,
        out_shape=jax.ShapeDtypeStruct((n_rows, d // 2), U32),
        compiler_params=_cparams(("arbitrary",)),
        name="expert_down",
    )(*sched, act, w_down, b_down)


def _combine_ple_kernel(pos_ref, y_hbm, h_ref, gate_ref, gn_ref, w_ref, p_ref, wp_ref, gf_ref, o_ref, gbuf, a2, sem,
                        *, tm, tn, pair, final):
    i = pl.program_id(0)
    j = pl.program_id(1)
    d = o_ref.shape[1]

    def row_copy(blk, t, k):
        row = pos_ref[(blk * tm + t) * TOP_K + k]
        return pltpu.make_async_copy(y_hbm.at[pl.ds(row, 1)], gbuf.at[k, pl.ds(t, 1)], sem)

    def issue(blk, t0, count):
        def body(t, carry):
            for k in range(TOP_K):
                row_copy(blk, t0 + t, k).start(priority=1)
            return carry

        lax.fori_loop(0, count, body, 0, unroll=DMA_ISSUE_UNROLL)

    per_step = tm // (d // tn)

    @pl.when(j == 0)
    def _():
        @pl.when(i == 0)
        def _():
            issue(0, 0, tm)

        def drain(t, carry):
            for k in range(TOP_K):
                row_copy(i, t, k).wait()
            return carry

        lax.fori_loop(0, tm, drain, 0, unroll=DMA_ISSUE_UNROLL)
        gates = gate_ref[...]
        hi_mask = jnp.uint32(0xFFFF0000)
        half = pair // 2
        ssq = jnp.zeros((tm, 1), F32)
        for c in range(d // pair):
            for part, unpack in enumerate((lambda w: w << 16, lambda w: w & hi_mask)):
                terms = [gates[:, k:k + 1] * pltpu.bitcast(unpack(gbuf[k, :, c * half:(c + 1) * half]), F32)
                         for k in range(TOP_K)]
                cols = slice(c * pair + part * half, c * pair + (part + 1) * half)
                val = h_ref[:, cols] + ((terms[0] + terms[1]) + (terms[2] + terms[3]))
                o_ref[:, cols] = val
                ssq = ssq + jnp.sum(val * val, axis=-1, keepdims=True)
        r = lax.rsqrt(ssq / d + NORM_EPS)
        for c0 in range(0, d, half):
            a2[:, c0:c0 + half] = (o_ref[:, c0:c0 + half] * r * gn_ref[:, c0:c0 + half]).astype(a2.dtype)

    @pl.when(i + 1 < pl.num_programs(0))
    def _():
        issue(i + 1, j * per_step, per_step)

    gate = jax.nn.sigmoid(jnp.dot(a2[...], w_ref[...], preferred_element_type=F32))
    emb = jnp.dot(p_ref[...], wp_ref[...], preferred_element_type=F32)
    cols = pl.ds(pl.multiple_of(j * tn, tn), tn)
    o_ref[:, cols] = o_ref[:, cols] + gate * emb

    if final:
        @pl.when(j == pl.num_programs(1) - 1)
        def _():
            o_ref[...] = _rmsnorm_body(o_ref[...], gf_ref[...])


def combine_ple(y, pos, h, gates, g_ple, w_gate, p, w_proj, g_final, *, pair, tm=COMBINE_PLE_TM, tn=MM_TN):
    t, d = h.shape
    pk = p.shape[1]
    tm, tn = min(tm, t), min(tn, d)
    final = g_final is not None
    g_last = (g_final if final else jnp.ones((d,), F32)).reshape(1, d)
    rows = lambda i, j, pos: (i, 0)
    const = lambda i, j, pos: (0, 0)
    colt = lambda i, j, pos: (0, j)
    grid_spec = pltpu.PrefetchScalarGridSpec(
        num_scalar_prefetch=1,
        grid=(t // tm, d // tn),
        in_specs=[pl.BlockSpec(memory_space=pl.ANY),
                  pl.BlockSpec((tm, d), rows),
                  pl.BlockSpec((tm, V7X_LANES), rows),
                  pl.BlockSpec((1, d), const),
                  pl.BlockSpec((d, tn), colt),
                  pl.BlockSpec((tm, pk), rows),
                  pl.BlockSpec((pk, tn), colt),
                  pl.BlockSpec((1, d), const)],
        out_specs=pl.BlockSpec((tm, d), rows),
        scratch_shapes=[pltpu.VMEM((TOP_K, tm, d // 2), U32), pltpu.VMEM((tm, d), BF16), pltpu.SemaphoreType.DMA(())],
    )
    return pl.pallas_call(
        functools.partial(_combine_ple_kernel, tm=tm, tn=tn, pair=pair, final=final),
        grid_spec=grid_spec,
        out_shape=jax.ShapeDtypeStruct((t, d), F32),
        compiler_params=_cparams(("arbitrary", "arbitrary")),
        name="combine_ple",
    )(pos, y, h, gates, g_ple.reshape(1, d), w_gate, p, w_proj, g_last)


def kernel(x, p, positions, attn_norm_g, w_in, conv_w, conv_b, b_igate, b_fgate, mh_norm_g, w_out,
           ffn_norm_g, w_router, b_router, w_gate_up, b_gate_up, w_down, b_down, ple_norm_g, w_ple_gate,
           w_ple_proj, final_norm_g):
    batch, seq, d = x.shape
    depth = p.shape[0]
    t = batch * seq
    n_main = 3 * ATTN_WIDTH + 2 * MLSTM_QK_WIDTH + 2 * MLSTM_V_WIDTH
    nh = MLSTM_HEADS
    h = x.reshape(t, d)
    cos, sin = rope_tables(positions.reshape(t, 1))
    for i in range(depth):
        a = rmsnorm(h, attn_norm_g[i], BF16)
        w_gates = jnp.zeros((d, V7X_LANES), BF16).at[:, :2 * nh].set(w_in[i][:, n_main:].astype(BF16))
        z, gates = in_projection(a, w_in[i].astype(BF16), n_main, w_gates, cos, sin)
        attn = dilated_attention(z, batch, seq)
        gate_bias = jnp.zeros((1, V7X_LANES), F32).at[0, :nh].set(b_igate[i]).at[0, nh:2 * nh].set(b_fgate[i])
        ml = mlstm(z, gates, gate_bias, conv_w[i], conv_b[i], mh_norm_g[i], batch, seq)
        h = out_projection(attn, ml, w_out[i].astype(BF16), h)
        fpk, top_e, top_g, rank, counts = ffn_norm_router(h, ffn_norm_g[i], w_router[i], b_router[i])
        pos, zero_blk, layout = routing_tables(top_e[:, :TOP_K], rank[:, :TOP_K],
                                               counts[0, :N_EXPERTS].astype(I32), t)
        xs = dispatch(fpk, pos, zero_blk)
        act = expert_up(xs, w_gate_up[i], b_gate_up[i].reshape(N_EXPERTS, 1, 2 * EXPERT_FF),
                        resident_schedule(layout, EXPERT_FF // EXPERT_TF))
        tn = min(EXPERT_TN, d)
        y = expert_down(act, w_down[i], b_down[i].reshape(N_EXPERTS, 1, d),
                        resident_schedule(layout, d // tn), tn=tn)
        h = combine_ple(y, pos, h, top_g, ple_norm_g[i], w_ple_gate[i].astype(BF16),
                        p[i].reshape(t, -1).astype(BF16), w_ple_proj[i].astype(BF16),
                        final_norm_g if i == depth - 1 else None, pair=tn)
    return h.astype(x.dtype).reshape(batch, seq, d)
```

```python
import functools

import jax
import jax.numpy as jnp
from jax import lax
from jax.experimental import pallas as pl
from jax.experimental.pallas import tpu as pltpu

F32 = jnp.float32
BF16 = jnp.bfloat16
U32 = jnp.uint32
I32 = jnp.int32

V7X_VMEM_BYTES = 64 * 1024 * 1024
V7X_LANES = 128
VMEM_LIMIT = V7X_VMEM_BYTES - 8 * 1024 * 1024

ATTN_HEADS = 16
HEAD_DIM = 128
ATTN_WIDTH = ATTN_HEADS * HEAD_DIM
ATTN_BLOCK = 128
DILATIONS = (1, 4, 16)
ATTN_SUPER = ATTN_BLOCK * max(DILATIONS)
ROPE_THETA = 10000.0
MLSTM_HEADS = 4
MLSTM_QK_DIM = 256
MLSTM_V_DIM = 512
MLSTM_QK_WIDTH = MLSTM_HEADS * MLSTM_QK_DIM
MLSTM_V_WIDTH = MLSTM_HEADS * MLSTM_V_DIM
MLSTM_CHUNK = 256
CONV_WIDTH = 4
CONV_PAD = 8
N_EXPERTS = 32
TOP_K = 4
EXPERT_FF = 1536
SWIGLU_LIMIT = 7.0
SWIGLU_ALPHA = 1.702
NORM_EPS = 1e-6
NEG_BIG = -1e30

MM_TM = 1024
MM_TN = 1024
NORM_TM = 256
EXPERT_TM = 512
EXPERT_TF = 512
EXPERT_TN = 2048
COMBINE_PLE_TM = 256
DMA_ISSUE_UNROLL = 4


def _cparams(sem, **kw):
    return pltpu.CompilerParams(dimension_semantics=sem, vmem_limit_bytes=VMEM_LIMIT, **kw)


def _rmsnorm_body(x, g):
    r = lax.rsqrt(jnp.mean(x * x, axis=-1, keepdims=True) + NORM_EPS)
    return x * r * g


def _rmsnorm_kernel(x_ref, g_ref, o_ref):
    o_ref[...] = _rmsnorm_body(x_ref[...], g_ref[...]).astype(o_ref.dtype)


def rmsnorm(x, g, out_dtype):
    t, d = x.shape
    return pl.pallas_call(
        _rmsnorm_kernel,
        grid=(t // NORM_TM,),
        in_specs=[pl.BlockSpec((NORM_TM, d), lambda i: (i, 0)),
                  pl.BlockSpec((1, d), lambda i: (0, 0))],
        out_specs=pl.BlockSpec((NORM_TM, d), lambda i: (i, 0)),
        out_shape=jax.ShapeDtypeStruct((t, d), out_dtype),
        compiler_params=_cparams(("parallel",)),
        name="rmsnorm",
    )(x, g.reshape(1, d))


def _rope_table_kernel(pos_ref, freq_ref, sign_ref, cos_ref, sin_ref):
    ang = pos_ref[...].astype(F32) * freq_ref[...]
    cos_ref[...] = jnp.cos(ang)
    sin_ref[...] = jnp.sin(ang) * sign_ref[...]


def rope_tables(positions_col):
    t = positions_col.shape[0]
    half = HEAD_DIM // 2
    inv_freq = jnp.power(ROPE_THETA, -jnp.arange(half, dtype=F32) / half)
    freq = jnp.concatenate([inv_freq, inv_freq]).reshape(1, HEAD_DIM)
    sign = jnp.concatenate([-jnp.ones((half,), F32), jnp.ones((half,), F32)]).reshape(1, HEAD_DIM)
    tm = 512
    return pl.pallas_call(
        _rope_table_kernel,
        grid=(t // tm,),
        in_specs=[pl.BlockSpec((tm, 1), lambda i: (i, 0)),
                  pl.BlockSpec((1, HEAD_DIM), lambda i: (0, 0)),
                  pl.BlockSpec((1, HEAD_DIM), lambda i: (0, 0))],
        out_specs=[pl.BlockSpec((tm, HEAD_DIM), lambda i: (i, 0))] * 2,
        out_shape=[jax.ShapeDtypeStruct((t, HEAD_DIM), F32)] * 2,
        compiler_params=_cparams(("parallel",)),
        name="rope_tables",
    )(positions_col, freq, sign)


def _inproj_kernel(a_ref, w_ref, wg_ref, cos_ref, sin_ref, z_ref, gate_ref, *, n_q_tiles, n_k_tiles,
                   q_scale):
    j = pl.program_id(1)

    @pl.when(j == 0)
    def _():
        gate_ref[...] = jnp.dot(a_ref[...], wg_ref[...], preferred_element_type=F32)

    def rope(scale):
        cos = cos_ref[...]
        sin = sin_ref[...]
        pair = 2 * HEAD_DIM
        for c0 in range(0, z_ref.shape[1], pair):
            acc = jnp.dot(a_ref[...], w_ref[:, c0:c0 + pair], preferred_element_type=F32)
            for c in range(c0, c0 + pair, HEAD_DIM):
                t = acc[:, c - c0:c - c0 + HEAD_DIM]
                r = pltpu.roll(t, HEAD_DIM // 2, axis=1)
                z_ref[:, c:c + HEAD_DIM] = ((t * cos + r * sin) * scale).astype(z_ref.dtype)

    @pl.when(j < n_q_tiles)
    def _():
        rope(q_scale)

    @pl.when((j >= n_q_tiles) & (j < n_q_tiles + n_k_tiles))
    def _():
        rope(1.0)

    @pl.when(j >= n_q_tiles + n_k_tiles)
    def _():
        z_ref[...] = jnp.dot(a_ref[...], w_ref[...], preferred_element_type=F32).astype(z_ref.dtype)


def in_projection(a, w, n, w_gates, cos, sin, *, tm=MM_TM, tn=MM_TN):
    t, k = a.shape
    tm, tn = min(tm, t), min(tn, n)
    kern = functools.partial(_inproj_kernel, n_q_tiles=ATTN_WIDTH // tn, n_k_tiles=ATTN_WIDTH // tn,
                             q_scale=HEAD_DIM ** -0.5)
    return pl.pallas_call(
        kern,
        grid=(t // tm, n // tn),
        in_specs=[pl.BlockSpec((tm, k), lambda i, j: (i, 0)),
                  pl.BlockSpec((k, tn), lambda i, j: (0, j)),
                  pl.BlockSpec((k, V7X_LANES), lambda i, j: (0, 0)),
                  pl.BlockSpec((tm, HEAD_DIM), lambda i, j: (i, 0)),
                  pl.BlockSpec((tm, HEAD_DIM), lambda i, j: (i, 0))],
        out_specs=[pl.BlockSpec((tm, tn), lambda i, j: (i, j)),
                   pl.BlockSpec((tm, V7X_LANES), lambda i, j: (i, 0))],
        out_shape=[jax.ShapeDtypeStruct((t, n), BF16),
                   jax.ShapeDtypeStruct((t, V7X_LANES), F32)],
        compiler_params=_cparams(("parallel", "arbitrary")),
        name="in_projection",
    )(a, w, w_gates, cos, sin)


def _band_block(qb, kb, vb, bias):
    s = lax.dot_general(qb, kb, (((1,), (1,)), ((), ())), preferred_element_type=F32) + bias
    m = jnp.max(s, axis=-1, keepdims=True)
    p = jnp.exp(s - m)
    l = jnp.sum(p, axis=-1, keepdims=True)
    o = jnp.dot(p.astype(BF16), vb, preferred_element_type=F32) / l
    return o, m + jnp.log(l)


def _attn_kernel(q_ref, kp_ref, kc_ref, vp_ref, vc_ref, o_ref, qf, kf, vf, ob, lb):
    sb = ATTN_SUPER
    blk = ATTN_BLOCK
    n = pl.program_id(2)
    qf[...] = q_ref[...].astype(F32)
    kf[0:sb, :] = kp_ref[...].astype(F32)
    kf[sb:2 * sb, :] = kc_ref[...].astype(F32)
    vf[0:sb, :] = vp_ref[...].astype(F32)
    vf[sb:2 * sb, :] = vc_ref[...].astype(F32)

    qi = lax.broadcasted_iota(I32, (blk, 2 * blk), 0)
    kj = lax.broadcasted_iota(I32, (blk, 2 * blk), 1)
    dist = blk + qi - kj
    band = (dist >= 0) & (dist <= blk)
    bias_in = jnp.where(band, 0.0, NEG_BIG).astype(F32)
    prev_ok = jnp.where(n > 0, 0.0, NEG_BIG).astype(F32)
    bias_edge = bias_in + jnp.where(kj < blk, prev_ok, 0.0)

    for p, d in enumerate(DILATIONS):
        per_class = sb // (d * blk)
        for r in range(d):
            for nb in range(per_class):
                q0 = r + d * blk * nb
                k0 = sb + q0 - d * blk
                if d == 1:
                    qb = qf[pl.ds(q0, blk), :]
                    kb = kf[pl.ds(k0, 2 * blk), :]
                    vb = vf[pl.ds(k0, 2 * blk), :]
                else:
                    qb = qf[pl.ds(q0, blk, stride=d), :]
                    kb = kf[pl.ds(k0, 2 * blk, stride=d), :]
                    vb = vf[pl.ds(k0, 2 * blk, stride=d), :]
                bias = bias_edge if nb == 0 else bias_in
                o, lse = _band_block(qb.astype(BF16), kb.astype(BF16), vb.astype(BF16), bias)
                lse_b = jnp.broadcast_to(lse, (blk, HEAD_DIM))
                if d == 1:
                    ob[p, pl.ds(q0, blk), :] = o
                    lb[p, pl.ds(q0, blk), :] = lse_b
                else:
                    ob[p, pl.ds(q0, blk, stride=d), :] = o
                    lb[p, pl.ds(q0, blk, stride=d), :] = lse_b

    l0, l1, l2 = lb[0], lb[1], lb[2]
    mx = jnp.maximum(jnp.maximum(l0, l1), l2)
    w0, w1, w2 = jnp.exp(l0 - mx), jnp.exp(l1 - mx), jnp.exp(l2 - mx)
    out = (w0 * ob[0] + w1 * ob[1] + w2 * ob[2]) / (w0 + w1 + w2)
    o_ref[...] = out.astype(o_ref.dtype)


def dilated_attention(z, batch, seq):
    t = z.shape[0]
    sb = ATTN_SUPER
    nsb = seq // sb
    h = ATTN_HEADS
    cur = lambda off: (lambda b, hh, n: (b * nsb + n, off + hh))
    prev = lambda off: (lambda b, hh, n: (b * nsb + jnp.maximum(n - 1, 0), off + hh))
    spec = lambda im: pl.BlockSpec((sb, HEAD_DIM), im)
    return pl.pallas_call(
        _attn_kernel,
        grid=(batch, h, nsb),
        in_specs=[spec(cur(0)), spec(prev(h)), spec(cur(h)), spec(prev(2 * h)), spec(cur(2 * h))],
        out_specs=spec(cur(0)),
        out_shape=jax.ShapeDtypeStruct((t, ATTN_WIDTH), BF16),
        scratch_shapes=[pltpu.VMEM((sb, HEAD_DIM), F32),
                        pltpu.VMEM((2 * sb, HEAD_DIM), F32),
                        pltpu.VMEM((2 * sb, HEAD_DIM), F32),
                        pltpu.VMEM((len(DILATIONS), sb, HEAD_DIM), F32),
                        pltpu.VMEM((len(DILATIONS), sb, HEAD_DIM), F32)],
        compiler_params=_cparams(("parallel", "parallel", "arbitrary")),
        name="dilated_attention",
    )(z, z, z, z, z)


def _log_sigmoid(x):
    return jnp.minimum(x, 0.0) - jnp.log(1.0 + jnp.exp(-jnp.abs(x)))


def _mlstm_kernel(q_ref, k_ref, v_ref, om_ref, g_ref, gb_ref, cw_ref, cb_ref, mhg_ref, o_ref,
                  xbuf, c_st, n_st, m_st):
    L = MLSTM_CHUNK
    dqk, dv, nh = MLSTM_QK_DIM, MLSTM_V_DIM, MLSTM_HEADS
    qw = nh * dqk
    c = pl.program_id(1)

    @pl.when(c == 0)
    def _():
        xbuf[0:CONV_PAD, :] = jnp.zeros((CONV_PAD, 2 * qw), F32)
        c_st[...] = jnp.zeros_like(c_st)
        n_st[...] = jnp.zeros_like(n_st)
        m_st[...] = jnp.zeros_like(m_st)

    @pl.when(c > 0)
    def _():
        xbuf[0:CONV_PAD, :] = xbuf[L:L + CONV_PAD, :]

    xbuf[CONV_PAD:CONV_PAD + L, 0:qw] = q_ref[...].astype(F32)
    xbuf[CONV_PAD:CONV_PAD + L, qw:2 * qw] = k_ref[...].astype(F32)
    y = cb_ref[...]
    for j in range(CONV_WIDTH):
        y = y + cw_ref[j:j + 1, :] * xbuf[pl.ds(CONV_PAD - CONV_WIDTH + 1 + j, L), :]
    qk = y * jax.nn.sigmoid(y)

    pre = g_ref[...] + gb_ref[...]
    lf = _log_sigmoid(pre)
    row = lax.broadcasted_iota(I32, (L, L), 0)
    col = lax.broadcasted_iota(I32, (L, L), 1)
    causal = col <= row
    tril = jnp.where(causal, 1.0, 0.0).astype(F32)
    bcum = jnp.dot(tril, lf, preferred_element_type=F32, precision=lax.Precision.HIGHEST)
    pre_t = pre.T
    bcum_t = bcum.T

    for h in range(nh):
        i_c = pre[:, h:h + 1]
        b_c = bcum[:, nh + h:nh + h + 1]
        i_r = pre_t[h:h + 1, :]
        b_r = bcum_t[nh + h:nh + h + 1, :]
        g = bcum[L - 1:L, nh + h:nh + h + 1]
        m = m_st[h, 0:1, 0:1]
        qq = (qk[:, h * dqk:(h + 1) * dqk] * (dqk ** -0.5))
        kk = qk[:, qw + h * dqk:qw + (h + 1) * dqk]
        vv = v_ref[:, h * dv:(h + 1) * dv]
        qb = qq.astype(BF16)

        log_d = jnp.where(causal, b_c - b_r + i_r, NEG_BIG)
        inter = b_c + m
        m_t = jnp.maximum(inter, jnp.max(log_d, axis=-1, keepdims=True))
        w_inter = jnp.exp(inter - m_t)
        s = lax.dot_general(qb, kk.astype(BF16), (((1,), (1,)), ((), ())),
                            preferred_element_type=F32) * jnp.exp(log_d - m_t)
        num = (w_inter * jnp.dot(qb, c_st[h].astype(BF16), preferred_element_type=F32)
               + jnp.dot(s.astype(BF16), vv, preferred_element_type=F32))
        den = (w_inter * jnp.sum(qq * n_st[h], axis=-1, keepdims=True)
               + jnp.sum(s, axis=-1, keepdims=True))
        hh = num / jnp.maximum(jnp.abs(den), jnp.exp(-m_t))

        log_w = g - b_c + i_c
        m_new = jnp.maximum(g + m, jnp.max(log_w, axis=0, keepdims=True))
        decay = jnp.exp(g + m - m_new)
        wk = kk * jnp.exp(log_w - m_new)
        c_st[h] = decay * c_st[h] + jnp.dot(wk.T.astype(BF16), vv, preferred_element_type=F32)
        n_st[h] = decay * n_st[h] + jnp.sum(wk, axis=0, keepdims=True)
        m_st[h] = jnp.broadcast_to(m_new, m_st.shape[1:])

        hn = hh * lax.rsqrt(jnp.mean(hh * hh, axis=-1, keepdims=True) + NORM_EPS)
        hn = hn * mhg_ref[:, h * dv:(h + 1) * dv]
        og = jax.nn.sigmoid(om_ref[:, h * dv:(h + 1) * dv].astype(F32))
        o_ref[:, h * dv:(h + 1) * dv] = (og * hn).astype(o_ref.dtype)


def mlstm(z, gates, gate_bias, conv_w, conv_b, mh_norm_g, batch, seq):
    t = z.shape[0]
    L = MLSTM_CHUNK
    nc = seq // L
    qw, vw = MLSTM_QK_WIDTH, MLSTM_V_WIDTH
    q_off = 3 * ATTN_WIDTH
    rows = lambda blk: (lambda b, c: (b * nc + c, blk))
    const = lambda b, c: (0, 0)
    return pl.pallas_call(
        _mlstm_kernel,
        grid=(batch, nc),
        in_specs=[pl.BlockSpec((L, qw), rows(q_off // qw)),
                  pl.BlockSpec((L, qw), rows(q_off // qw + 1)),
                  pl.BlockSpec((L, vw), rows((q_off + 2 * qw) // vw)),
                  pl.BlockSpec((L, vw), rows((q_off + 2 * qw) // vw + 1)),
                  pl.BlockSpec((L, V7X_LANES), rows(0)),
                  pl.BlockSpec((1, V7X_LANES), const),
                  pl.BlockSpec((CONV_WIDTH, 2 * qw), const),
                  pl.BlockSpec((1, 2 * qw), const),
                  pl.BlockSpec((1, vw), const)],
        out_specs=pl.BlockSpec((L, vw), rows(0)),
        out_shape=jax.ShapeDtypeStruct((t, vw), BF16),
        scratch_shapes=[pltpu.VMEM((L + CONV_PAD, 2 * qw), F32),
                        pltpu.VMEM((MLSTM_HEADS, MLSTM_QK_DIM, MLSTM_V_DIM), F32),
                        pltpu.VMEM((MLSTM_HEADS, 1, MLSTM_QK_DIM), F32),
                        pltpu.VMEM((MLSTM_HEADS, 8, V7X_LANES), F32)],
        compiler_params=_cparams(("parallel", "arbitrary")),
        name="mlstm",
    )(z, z, z, z, gates, gate_bias, conv_w, conv_b.reshape(1, 2 * qw), mh_norm_g.reshape(1, vw))


def _outproj_kernel(a1_ref, a2_ref, w_ref, res_ref, o_ref):
    k1 = a1_ref.shape[1]
    acc = jnp.dot(a1_ref[...], w_ref[0:k1, :], preferred_element_type=F32)
    acc = acc + jnp.dot(a2_ref[...], w_ref[k1:, :], preferred_element_type=F32)
    o_ref[...] = res_ref[...] + acc


def out_projection(a1, a2, w, res, *, tm=MM_TM, tn=MM_TN):
    t, k1 = a1.shape
    k2 = a2.shape[1]
    n = w.shape[1]
    tm, tn = min(tm, t), min(tn, n)
    return pl.pallas_call(
        _outproj_kernel,
        grid=(t // tm, n // tn),
        in_specs=[pl.BlockSpec((tm, k1), lambda i, j: (i, 0)),
                  pl.BlockSpec((tm, k2), lambda i, j: (i, 0)),
                  pl.BlockSpec((k1 + k2, tn), lambda i, j: (0, j)),
                  pl.BlockSpec((tm, tn), lambda i, j: (i, j))],
        out_specs=pl.BlockSpec((tm, tn), lambda i, j: (i, j)),
        out_shape=jax.ShapeDtypeStruct((t, n), F32),
        compiler_params=_cparams(("parallel", "arbitrary")),
        name="out_projection",
    )(a1, a2, w, res)


def _pack_bf16_pairs(f):
    half = f.shape[1] // 2
    fb = f.astype(BF16).astype(F32)
    lo = pltpu.bitcast(fb[:, :half], U32) >> 16
    hi = pltpu.bitcast(fb[:, half:], U32) & jnp.uint32(0xFFFF0000)
    return lo | hi


def _unpack_bf16_pairs(w):
    lo = pltpu.bitcast(w << 16, F32).astype(BF16)
    hi = pltpu.bitcast(w & jnp.uint32(0xFFFF0000), F32).astype(BF16)
    return lo, hi


def _router_kernel(h_ref, g_ref, wr_ref, br_ref, fpk_ref, e_ref, gate_ref, rank_ref, cnt_ref, carry):
    i = pl.program_id(0)

    @pl.when(i == 0)
    def _():
        carry[...] = jnp.zeros_like(carry)

    f = _rmsnorm_body(h_ref[...], g_ref[...])
    fpk_ref[...] = _pack_bf16_pairs(f)
    f_hi = f.astype(BF16)
    f_lo = (f - f_hi.astype(F32)).astype(BF16)
    logits = (jnp.dot(f_hi, wr_ref[0], preferred_element_type=F32)
              + (jnp.dot(f_lo, wr_ref[0], preferred_element_type=F32)
                 + jnp.dot(f_hi, wr_ref[1], preferred_element_type=F32))) + br_ref[...]
    lane = lax.broadcasted_iota(I32, logits.shape, 1)
    cur = logits
    vals, idxs = [], []
    for _ in range(TOP_K):
        mx = jnp.max(cur, axis=-1, keepdims=True)
        idx = jnp.min(jnp.where(cur == mx, lane, V7X_LANES), axis=-1, keepdims=True)
        vals.append(mx)
        idxs.append(idx)
        cur = jnp.where(lane == idx, -jnp.inf, cur)
    exps = [jnp.exp(v - vals[0]) for v in vals]
    tot = exps[0] + exps[1] + exps[2] + exps[3]
    e_out = jnp.zeros(logits.shape, I32)
    g_out = jnp.zeros(logits.shape, F32)
    for k in range(TOP_K):
        e_out = jnp.where(lane == k, idxs[k], e_out)
        g_out = jnp.where(lane == k, exps[k] / tot, g_out)
    e_ref[...] = e_out
    gate_ref[...] = g_out

    tm = logits.shape[0]
    onehots = [lane == idxs[k] for k in range(TOP_K)]
    member = jnp.zeros(logits.shape, F32)
    for k in range(TOP_K):
        member = jnp.where(onehots[k], 1.0, member)
    row = lax.broadcasted_iota(I32, (tm, tm), 0)
    col = lax.broadcasted_iota(I32, (tm, tm), 1)
    stril = jnp.where(col < row, 1.0, 0.0).astype(BF16)
    before = carry[0:1, :] + jnp.dot(stril, member.astype(BF16), preferred_element_type=F32)
    r_out = jnp.zeros(logits.shape, I32)
    for k in range(TOP_K):
        rk = jnp.sum(jnp.where(onehots[k], before, 0.0), axis=-1, keepdims=True)
        r_out = jnp.where(lane == k, rk.astype(I32), r_out)
    rank_ref[...] = r_out
    total = carry[0:1, :] + jnp.sum(member, axis=0, keepdims=True)
    carry[...] = jnp.broadcast_to(total, carry.shape)
    cnt_ref[...] = jnp.broadcast_to(total, cnt_ref.shape)


def ffn_norm_router(h, g, w_router, b_router, *, tm=NORM_TM):
    t, d = h.shape
    nt = t // tm
    wr = jnp.zeros((d, V7X_LANES), F32).at[:, :N_EXPERTS].set(w_router)
    wr_hi = wr.astype(BF16)
    wr = jnp.stack([wr_hi, (wr - wr_hi.astype(F32)).astype(BF16)])
    br = jnp.full((1, V7X_LANES), NEG_BIG, F32).at[0, :N_EXPERTS].set(b_router)
    rows = lambda i: (i, 0)
    const = lambda i: (0, 0)
    return pl.pallas_call(
        _router_kernel,
        grid=(nt,),
        in_specs=[pl.BlockSpec((tm, d), rows),
                  pl.BlockSpec((1, d), const),
                  pl.BlockSpec((2, d, V7X_LANES), lambda i: (0, 0, 0)),
                  pl.BlockSpec((1, V7X_LANES), const)],
        out_specs=[pl.BlockSpec((tm, d // 2), rows),
                   pl.BlockSpec((tm, V7X_LANES), rows),
                   pl.BlockSpec((tm, V7X_LANES), rows),
                   pl.BlockSpec((tm, V7X_LANES), rows),
                   pl.BlockSpec((8, V7X_LANES), const)],
        out_shape=[jax.ShapeDtypeStruct((t, d // 2), U32),
                   jax.ShapeDtypeStruct((t, V7X_LANES), I32),
                   jax.ShapeDtypeStruct((t, V7X_LANES), F32),
                   jax.ShapeDtypeStruct((t, V7X_LANES), I32),
                   jax.ShapeDtypeStruct((8, V7X_LANES), F32)],
        scratch_shapes=[pltpu.VMEM((8, V7X_LANES), F32)],
        compiler_params=_cparams(("arbitrary",)),
        name="ffn_norm_router",
    )(h, g.reshape(1, d), wr, br)


def routing_tables(top_e, rank, counts, n_tokens):
    tk = n_tokens * TOP_K
    tm = EXPERT_TM
    n_blk = -(-(tk + N_EXPERTS * (tm - 1)) // tm)
    padded = (counts + tm - 1) // tm * tm
    pend = jnp.cumsum(padded)
    pstart = pend - padded
    pos = (pstart[top_e] + rank).reshape(tk).astype(I32)
    blk_start = jnp.arange(n_blk, dtype=I32) * tm
    blk_e = jnp.minimum(jnp.sum(blk_start[:, None] >= pend[None, :], axis=1), N_EXPERTS - 1).astype(I32)
    n_used = (pend[-1] // tm).astype(I32).reshape(1)
    is_last = jnp.any((blk_start[:, None] + tm == pend[None, :]) & (padded[None, :] > 0), axis=1)
    zero_blk = (is_last | (blk_start >= pend[-1])).astype(I32)
    return pos, zero_blk, (blk_e, n_used[0], pstart // tm, padded // tm, n_blk, counts)


SCHED_ACTIVE = 1
SCHED_FIRST = 2
SCHED_PREFETCH = 4
SCHED_HALF = 8


def resident_schedule(layout, n_col):
    blk_e, n_used, first_blk, n_blks, n_blk, counts = layout
    s = jnp.arange(n_blk * n_col, dtype=I32)
    n_active = n_col * n_used
    active = s < n_active
    sc = jnp.minimum(s, jnp.maximum(n_active - 1, 0))
    e = blk_e[jnp.minimum(sc // n_col, n_blk - 1)]
    b0 = first_blk[e]
    nb = jnp.maximum(n_blks[e], 1)
    local = sc - n_col * b0
    col = local // nb
    j = local % nb
    blk = b0 + j
    out_blk = jnp.where(active, blk, s // n_col)
    out_col = jnp.where(active, col, s % n_col)
    nxt = sc - j + nb
    has_next = nxt < n_active
    nxt = jnp.minimum(nxt, jnp.maximum(n_active - 1, 0))
    half_empty = counts[e] - j * EXPERT_TM <= EXPERT_TM // 2
    flags = jnp.where(active, SCHED_ACTIVE + SCHED_HALF * half_empty
                      + (j == 0) * (SCHED_FIRST + SCHED_PREFETCH * has_next), 0)
    as_i32 = lambda a: a.astype(I32)
    return tuple(map(as_i32, (blk, out_blk, out_col, e, col, flags, e[nxt], col[nxt])))


def _dispatch_kernel(pos_ref, zero_ref, f_ref, xs_hbm, zbuf, sem, zsem, *, rows, n_blk):
    i = pl.program_id(0)
    tm = f_ref.shape[0]

    def zero_copy(b):
        return pltpu.make_async_copy(zbuf, xs_hbm.at[pl.ds(b * rows, rows)], zsem)

    @pl.when(i == 0)
    def _():
        zbuf[...] = jnp.zeros_like(zbuf)

        def issue(b, c):
            @pl.when(zero_ref[b] != 0)
            def _():
                zero_copy(b).start()
            return c

        def drain(b, c):
            @pl.when(zero_ref[b] != 0)
            def _():
                zero_copy(b).wait()
            return c

        lax.fori_loop(0, n_blk, issue, 0)
        lax.fori_loop(0, n_blk, drain, 0)

    def row_copy(t, k):
        row = pos_ref[(i * tm + t) * TOP_K + k]
        return pltpu.make_async_copy(f_ref.at[pl.ds(t, 1)], xs_hbm.at[pl.ds(row, 1)], sem)

    def drain_rows(t, c):
        for k in range(TOP_K):
            row_copy(t, k).wait()
        return c

    for t in range(tm):
        for k in range(TOP_K):
            row_copy(t, k).start()
    lax.fori_loop(0, tm, drain_rows, 0, unroll=DMA_ISSUE_UNROLL)


def dispatch(fpk, pos, zero_blk, *, rows=EXPERT_TM, tm=NORM_TM):
    t, half = fpk.shape
    n_blk = zero_blk.shape[0]
    grid_spec = pltpu.PrefetchScalarGridSpec(
        num_scalar_prefetch=2,
        grid=(t // tm,),
        in_specs=[pl.BlockSpec((tm, half), lambda i, p, z: (i, 0))],
        out_specs=pl.BlockSpec(memory_space=pl.ANY),
        scratch_shapes=[pltpu.VMEM((rows, half), fpk.dtype), pltpu.SemaphoreType.DMA(()),
                        pltpu.SemaphoreType.DMA(())],
    )
    return pl.pallas_call(
        functools.partial(_dispatch_kernel, rows=rows, n_blk=n_blk),
        grid_spec=grid_spec,
        out_shape=jax.ShapeDtypeStruct((n_blk * rows, half), fpk.dtype),
        compiler_params=_cparams(("arbitrary",), has_side_effects=True),
        name="dispatch",
    )(pos, zero_blk, fpk)


def _weight_tile_refresh(s, flag, tile_copies, cast):
    @pl.when((flag & SCHED_FIRST) != 0)
    def _():
        @pl.when(s == 0)
        def _():
            for cp in tile_copies(False):
                cp.start()

        for cp in tile_copies(False):
            cp.wait()
        cast()

        @pl.when((flag & SCHED_PREFETCH) != 0)
        def _():
            for cp in tile_copies(True):
                cp.start(priority=1)


def _expert_up_kernel(xb_ref, ob_ref, oc_ref, se_ref, sc_ref, flag_ref, ne_ref, nc_ref, xs_ref, w_hbm, bg_ref,
                      bu_ref, act_ref, wstage, wgb, wub, sem, *, tf, nf):
    s = pl.program_id(0)
    half = xs_ref.shape[1]
    flag = flag_ref[s]

    def tile_copies(following):
        e = (ne_ref if following else se_ref)[s]
        c = (nc_ref if following else sc_ref)[s]
        gate_cols = pl.ds(pl.multiple_of(c * tf, tf), tf)
        up_cols = pl.ds(pl.multiple_of((nf + c) * tf, tf), tf)
        return (pltpu.make_async_copy(w_hbm.at[e, :, gate_cols], wstage.at[0], sem.at[0]),
                pltpu.make_async_copy(w_hbm.at[e, :, up_cols], wstage.at[1], sem.at[1]))

    def cast():
        wgb[...] = wstage[0].astype(BF16)
        wub[...] = wstage[1].astype(BF16)

    _weight_tile_refresh(s, flag, tile_copies, cast)

    @pl.when((flag & SCHED_ACTIVE) != 0)
    def _():
        def rows_compute(rows):
            lo, hi = _unpack_bf16_pairs(xs_ref[0:rows, :])
            glu = (jnp.dot(lo, wgb[0:half, :], preferred_element_type=F32)
                   + jnp.dot(hi, wgb[half:, :], preferred_element_type=F32) + bg_ref[...])
            up = (jnp.dot(lo, wub[0:half, :], preferred_element_type=F32)
                  + jnp.dot(hi, wub[half:, :], preferred_element_type=F32) + bu_ref[...])
            glu = jnp.minimum(glu, SWIGLU_LIMIT)
            up = jnp.clip(up, -SWIGLU_LIMIT, SWIGLU_LIMIT)
            act = (up + 1.0) * (glu * jax.nn.sigmoid(SWIGLU_ALPHA * glu))
            act_ref[0:rows, :] = act.astype(act_ref.dtype)

        tm = xs_ref.shape[0]

        @pl.when((flag & SCHED_HALF) == 0)
        def _():
            rows_compute(tm)

        @pl.when((flag & SCHED_HALF) != 0)
        def _():
            rows_compute(tm // 2)
            act_ref[tm // 2:, :] = jnp.zeros((tm - tm // 2, act_ref.shape[1]), act_ref.dtype)

    @pl.when((flag & SCHED_ACTIVE) == 0)
    def _():
        act_ref[...] = jnp.zeros_like(act_ref)


def expert_up(xs, w_gate_up, b_gate_up, sched, *, tm=EXPERT_TM, tf=EXPERT_TF):
    n_rows, half = xs.shape
    d = 2 * half
    ff = EXPERT_FF
    nf = ff // tf
    n_steps = sched[0].shape[0]
    grid_spec = pltpu.PrefetchScalarGridSpec(
        num_scalar_prefetch=len(sched),
        grid=(n_steps,),
        in_specs=[pl.BlockSpec((tm, half), lambda s, *t: (t[0][s], 0)),
                  pl.BlockSpec(memory_space=pl.ANY),
                  pl.BlockSpec((None, 1, tf), lambda s, *t: (t[3][s], 0, t[4][s])),
                  pl.BlockSpec((None, 1, tf), lambda s, *t: (t[3][s], 0, nf + t[4][s]))],
        out_specs=pl.BlockSpec((tm, tf), lambda s, *t: (t[1][s], t[2][s])),
        scratch_shapes=[pltpu.VMEM((2, d, tf), F32), pltpu.VMEM((d, tf), BF16), pltpu.VMEM((d, tf), BF16),
                        pltpu.SemaphoreType.DMA((2,))],
    )
    return pl.pallas_call(
        functools.partial(_expert_up_kernel, tf=tf, nf=nf),
        grid_spec=grid_spec,
        out_shape=jax.ShapeDtypeStruct((n_rows, ff), BF16),
        compiler_params=_cparams(("arbitrary",)),
        name="expert_up",
    )(*sched, xs, w_gate_up, b_gate_up, b_gate_up)


def _expert_down_kernel(xb_ref, ob_ref, oc_ref, se_ref, sc_ref, flag_ref, ne_ref, nc_ref, act_ref, w_hbm, bd_ref,
                        y_ref, wstage, wdb, sem, *, tn):
    s = pl.program_id(0)
    flag = flag_ref[s]

    def tile_copies(following):
        e = (ne_ref if following else se_ref)[s]
        c = (nc_ref if following else sc_ref)[s]
        cols = pl.ds(pl.multiple_of(c * tn, tn), tn)
        return (pltpu.make_async_copy(w_hbm.at[e, :, cols], wstage, sem),)

    def cast():
        wdb[...] = wstage[...].astype(BF16)

    _weight_tile_refresh(s, flag, tile_copies, cast)

    @pl.when((flag & SCHED_ACTIVE) != 0)
    def _():
        def rows_compute(rows):
            y = jnp.dot(act_ref[0:rows, :], wdb[...], preferred_element_type=F32) + bd_ref[...]
            y_ref[0:rows, :] = _pack_bf16_pairs(y)

        tm = act_ref.shape[0]

        @pl.when((flag & SCHED_HALF) == 0)
        def _():
            rows_compute(tm)

        @pl.when((flag & SCHED_HALF) != 0)
        def _():
            rows_compute(tm // 2)
            y_ref[tm // 2:, :] = jnp.zeros((tm - tm // 2, y_ref.shape[1]), y_ref.dtype)

    @pl.when((flag & SCHED_ACTIVE) == 0)
    def _():
        y_ref[...] = jnp.zeros_like(y_ref)


def expert_down(act, w_down, b_down, sched, *, tm=EXPERT_TM, tn=EXPERT_TN):
    n_rows, ff = act.shape
    d = w_down.shape[2]
    n_steps = sched[0].shape[0]
    grid_spec = pltpu.PrefetchScalarGridSpec(
        num_scalar_prefetch=len(sched),
        grid=(n_steps,),
        in_specs=[pl.BlockSpec((tm, ff), lambda s, *t: (t[0][s], 0)),
                  pl.BlockSpec(memory_space=pl.ANY),
                  pl.BlockSpec((None, 1, tn), lambda s, *t: (t[3][s], 0, t[4][s]))],
        out_specs=pl.BlockSpec((tm, tn // 2), lambda s, *t: (t[1][s], t[2][s])),
        scratch_shapes=[pltpu.VMEM((ff, tn), F32), pltpu.VMEM((ff, tn), BF16), pltpu.SemaphoreType.DMA(())],
    )
    return pl.pallas_call(
        functools.partial(_expert_down_kernel, tn=tn),
        grid_spec=grid_spec,
        out_shape=jax.ShapeDtypeStruct((n_rows, d // 2), U32),
        compiler_params=_cparams(("arbitrary",)),
        name="expert_down",
    )(*sched, act, w_down, b_down)


def _combine_ple_kernel(pos_ref, y_hbm, h_ref, gate_ref, gn_ref, w_ref, p_ref, wp_ref, gf_ref, o_ref, gbuf, a2, sem,
                        *, tm, tn, pair, final):
    i = pl.program_id(0)
    j = pl.program_id(1)
    d = o_ref.shape[1]

    def row_copy(blk, t, k):
        row = pos_ref[(blk * tm + t) * TOP_K + k]
        return pltpu.make_async_copy(y_hbm.at[pl.ds(row, 1)], gbuf.at[k, pl.ds(t, 1)], sem)

    def issue(blk, t0, count):
        def body(t, carry):
            for k in range(TOP_K):
                row_copy(blk, t0 + t, k).start(priority=1)
            return carry

        lax.fori_loop(0, count, body, 0, unroll=DMA_ISSUE_UNROLL)

    def drain(t, carry):
        for k in range(TOP_K):
            row_copy(i, t, k).wait()
        return carry

    per_step = tm // (d // tn)

    @pl.when(j == 0)
    def _():
        @pl.when(i == 0)
        def _():
            issue(0, 0, tm)

        lax.fori_loop(0, tm, drain, 0, unroll=DMA_ISSUE_UNROLL)
        gates = gate_ref[...]
        hi_mask = jnp.uint32(0xFFFF0000)
        half = pair // 2
        ssq = jnp.zeros((tm, 1), F32)
        for c in range(d // pair):
            for part, unpack in enumerate((lambda w: w << 16, lambda w: w & hi_mask)):
                terms = [gates[:, k:k + 1] * pltpu.bitcast(unpack(gbuf[k, :, c * half:(c + 1) * half]), F32)
                         for k in range(TOP_K)]
                cols = slice(c * pair + part * half, c * pair + (part + 1) * half)
                val = h_ref[:, cols] + ((terms[0] + terms[1]) + (terms[2] + terms[3]))
                o_ref[:, cols] = val
                ssq = ssq + jnp.sum(val * val, axis=-1, keepdims=True)
        r = lax.rsqrt(ssq / d + NORM_EPS)
        for c0 in range(0, d, half):
            a2[:, c0:c0 + half] = (o_ref[:, c0:c0 + half] * r * gn_ref[:, c0:c0 + half]).astype(a2.dtype)

    last_i = pl.num_programs(0) - 1
    nxt = jnp.minimum(i + 1, last_i)
    for t in range(per_step):
        for k in range(TOP_K):
            row_copy(nxt, j * per_step + t, k).start(priority=1)

    gate = jax.nn.sigmoid(jnp.dot(a2[...], w_ref[...], preferred_element_type=F32))
    emb = jnp.dot(p_ref[...], wp_ref[...], preferred_element_type=F32)
    cols = pl.ds(pl.multiple_of(j * tn, tn), tn)
    o_ref[:, cols] = o_ref[:, cols] + gate * emb

    @pl.when((i == last_i) & (j == pl.num_programs(1) - 1))
    def _():
        lax.fori_loop(0, tm, drain, 0, unroll=DMA_ISSUE_UNROLL)

    if final:
        @pl.when(j == pl.num_programs(1) - 1)
        def _():
            o_ref[...] = _rmsnorm_body(o_ref[...], gf_ref[...])


def combine_ple(y, pos, h, gates, g_ple, w_gate, p, w_proj, g_final, *, pair, tm=COMBINE_PLE_TM, tn=MM_TN):
    t, d = h.shape
    pk = p.shape[1]
    tm, tn = min(tm, t), min(tn, d)
    final = g_final is not None
    g_last = (g_final if final else jnp.ones((d,), F32)).reshape(1, d)
    rows = lambda i, j, pos: (i, 0)
    const = lambda i, j, pos: (0, 0)
    colt = lambda i, j, pos: (0, j)
    grid_spec = pltpu.PrefetchScalarGridSpec(
        num_scalar_prefetch=1,
        grid=(t // tm, d // tn),
        in_specs=[pl.BlockSpec(memory_space=pl.ANY),
                  pl.BlockSpec((tm, d), rows),
                  pl.BlockSpec((tm, V7X_LANES), rows),
                  pl.BlockSpec((1, d), const),
                  pl.BlockSpec((d, tn), colt),
                  pl.BlockSpec((tm, pk), rows),
                  pl.BlockSpec((pk, tn), colt),
                  pl.BlockSpec((1, d), const)],
        out_specs=pl.BlockSpec((tm, d), rows),
        scratch_shapes=[pltpu.VMEM((TOP_K, tm, d // 2), U32), pltpu.VMEM((tm, d), BF16), pltpu.SemaphoreType.DMA(())],
    )
    return pl.pallas_call(
        functools.partial(_combine_ple_kernel, tm=tm, tn=tn, pair=pair, final=final),
        grid_spec=grid_spec,
        out_shape=jax.ShapeDtypeStruct((t, d), F32),
        compiler_params=_cparams(("arbitrary", "arbitrary")),
        name="combine_ple",
    )(pos, y, h, gates, g_ple.reshape(1, d), w_gate, p, w_proj, g_last)


def kernel(x, p, positions, attn_norm_g, w_in, conv_w, conv_b, b_igate, b_fgate, mh_norm_g, w_out,
           ffn_norm_g, w_router, b_router, w_gate_up, b_gate_up, w_down, b_down, ple_norm_g, w_ple_gate,
           w_ple_proj, final_norm_g):
    batch, seq, d = x.shape
    depth = p.shape[0]
    t = batch * seq
    n_main = 3 * ATTN_WIDTH + 2 * MLSTM_QK_WIDTH + 2 * MLSTM_V_WIDTH
    nh = MLSTM_HEADS
    h = x.reshape(t, d)
    cos, sin = rope_tables(positions.reshape(t, 1))
    for i in range(depth):
        a = rmsnorm(h, attn_norm_g[i], BF16)
        w_gates = jnp.zeros((d, V7X_LANES), BF16).at[:, :2 * nh].set(w_in[i][:, n_main:].astype(BF16))
        z, gates = in_projection(a, w_in[i].astype(BF16), n_main, w_gates, cos, sin)
        attn = dilated_attention(z, batch, seq)
        gate_bias = jnp.zeros((1, V7X_LANES), F32).at[0, :nh].set(b_igate[i]).at[0, nh:2 * nh].set(b_fgate[i])
        ml = mlstm(z, gates, gate_bias, conv_w[i], conv_b[i], mh_norm_g[i], batch, seq)
        h = out_projection(attn, ml, w_out[i].astype(BF16), h)
        fpk, top_e, top_g, rank, counts = ffn_norm_router(h, ffn_norm_g[i], w_router[i], b_router[i])
        pos, zero_blk, layout = routing_tables(top_e[:, :TOP_K], rank[:, :TOP_K],
                                               counts[0, :N_EXPERTS].astype(I32), t)
        xs = dispatch(fpk, pos, zero_blk)
        act = expert_up(xs, w_gate_up[i], b_gate_up[i].reshape(N_EXPERTS, 1, 2 * EXPERT_FF),
                        resident_schedule(layout, EXPERT_FF // EXPERT_TF))
        tn = min(EXPERT_TN, d)
        y = expert_down(act, w_down[i], b_down[i].reshape(N_EXPERTS, 1, d),
                        resident_schedule(layout, d // tn), tn=tn)
        h = combine_ple(y, pos, h, top_g, ple_norm_g[i], w_ple_gate[i].astype(BF16),
                        p[i].reshape(t, -1).astype(BF16), w_ple_proj[i].astype(BF16),
                        final_norm_g if i == depth - 1 else None, pair=tn)
    return h.astype(x.dtype).reshape(batch, seq, d)
```

```python
import functools

import jax
import jax.numpy as jnp
from jax import lax
from jax.experimental import pallas as pl
from jax.experimental.pallas import tpu as pltpu

F32 = jnp.float32
BF16 = jnp.bfloat16
U32 = jnp.uint32
I32 = jnp.int32

V7X_VMEM_BYTES = 64 * 1024 * 1024
V7X_LANES = 128
VMEM_LIMIT = V7X_VMEM_BYTES - 8 * 1024 * 1024

ATTN_HEADS = 16
HEAD_DIM = 128
ATTN_WIDTH = ATTN_HEADS * HEAD_DIM
ATTN_BLOCK = 128
DILATIONS = (1, 4, 16)
ATTN_SUPER = ATTN_BLOCK * max(DILATIONS)
ROPE_THETA = 10000.0
MLSTM_HEADS = 4
MLSTM_QK_DIM = 256
MLSTM_V_DIM = 512
MLSTM_QK_WIDTH = MLSTM_HEADS * MLSTM_QK_DIM
MLSTM_V_WIDTH = MLSTM_HEADS * MLSTM_V_DIM
MLSTM_CHUNK = 256
CONV_WIDTH = 4
CONV_PAD = 8
N_EXPERTS = 32
TOP_K = 4
EXPERT_FF = 1536
SWIGLU_LIMIT = 7.0
SWIGLU_ALPHA = 1.702
NORM_EPS = 1e-6
NEG_BIG = -1e30

MM_TM = 1024
MM_TN = 1024
NORM_TM = 256
EXPERT_TM = 512
EXPERT_TF = 512
EXPERT_TN = 2048
COMBINE_PLE_TM = 256
DMA_ISSUE_UNROLL = 4


def _cparams(sem, **kw):
    return pltpu.CompilerParams(dimension_semantics=sem, vmem_limit_bytes=VMEM_LIMIT, **kw)


def _rmsnorm_body(x, g):
    r = lax.rsqrt(jnp.mean(x * x, axis=-1, keepdims=True) + NORM_EPS)
    return x * r * g


def _rmsnorm_kernel(x_ref, g_ref, o_ref):
    o_ref[...] = _rmsnorm_body(x_ref[...], g_ref[...]).astype(o_ref.dtype)


def rmsnorm(x, g, out_dtype):
    t, d = x.shape
    return pl.pallas_call(
        _rmsnorm_kernel,
        grid=(t // NORM_TM,),
        in_specs=[pl.BlockSpec((NORM_TM, d), lambda i: (i, 0)),
                  pl.BlockSpec((1, d), lambda i: (0, 0))],
        out_specs=pl.BlockSpec((NORM_TM, d), lambda i: (i, 0)),
        out_shape=jax.ShapeDtypeStruct((t, d), out_dtype),
        compiler_params=_cparams(("parallel",)),
        name="rmsnorm",
    )(x, g.reshape(1, d))


def _rope_table_kernel(pos_ref, freq_ref, sign_ref, cos_ref, sin_ref):
    ang = pos_ref[...].astype(F32) * freq_ref[...]
    cos_ref[...] = jnp.cos(ang)
    sin_ref[...] = jnp.sin(ang) * sign_ref[...]


def rope_tables(positions_col):
    t = positions_col.shape[0]
    half = HEAD_DIM // 2
    inv_freq = jnp.power(ROPE_THETA, -jnp.arange(half, dtype=F32) / half)
    freq = jnp.concatenate([inv_freq, inv_freq]).reshape(1, HEAD_DIM)
    sign = jnp.concatenate([-jnp.ones((half,), F32), jnp.ones((half,), F32)]).reshape(1, HEAD_DIM)
    tm = 512
    return pl.pallas_call(
        _rope_table_kernel,
        grid=(t // tm,),
        in_specs=[pl.BlockSpec((tm, 1), lambda i: (i, 0)),
                  pl.BlockSpec((1, HEAD_DIM), lambda i: (0, 0)),
                  pl.BlockSpec((1, HEAD_DIM), lambda i: (0, 0))],
        out_specs=[pl.BlockSpec((tm, HEAD_DIM), lambda i: (i, 0))] * 2,
        out_shape=[jax.ShapeDtypeStruct((t, HEAD_DIM), F32)] * 2,
        compiler_params=_cparams(("parallel",)),
        name="rope_tables",
    )(positions_col, freq, sign)


def _inproj_kernel(a_ref, w_ref, wg_ref, cos_ref, sin_ref, z_ref, gate_ref, *, n_q_tiles, n_k_tiles,
                   q_scale):
    j = pl.program_id(1)

    @pl.when(j == 0)
    def _():
        gate_ref[...] = jnp.dot(a_ref[...], wg_ref[...], preferred_element_type=F32)

    def rope(scale):
        cos = cos_ref[...]
        sin = sin_ref[...]
        pair = 2 * HEAD_DIM
        for c0 in range(0, z_ref.shape[1], pair):
            acc = jnp.dot(a_ref[...], w_ref[:, c0:c0 + pair], preferred_element_type=F32)
            for c in range(c0, c0 + pair, HEAD_DIM):
                t = acc[:, c - c0:c - c0 + HEAD_DIM]
                r = pltpu.roll(t, HEAD_DIM // 2, axis=1)
                z_ref[:, c:c + HEAD_DIM] = ((t * cos + r * sin) * scale).astype(z_ref.dtype)

    @pl.when(j < n_q_tiles)
    def _():
        rope(q_scale)

    @pl.when((j >= n_q_tiles) & (j < n_q_tiles + n_k_tiles))
    def _():
        rope(1.0)

    @pl.when(j >= n_q_tiles + n_k_tiles)
    def _():
        z_ref[...] = jnp.dot(a_ref[...], w_ref[...], preferred_element_type=F32).astype(z_ref.dtype)


def in_projection(a, w, n, w_gates, cos, sin, *, tm=MM_TM, tn=MM_TN):
    t, k = a.shape
    tm, tn = min(tm, t), min(tn, n)
    kern = functools.partial(_inproj_kernel, n_q_tiles=ATTN_WIDTH // tn, n_k_tiles=ATTN_WIDTH // tn,
                             q_scale=HEAD_DIM ** -0.5)
    return pl.pallas_call(
        kern,
        grid=(t // tm, n // tn),
        in_specs=[pl.BlockSpec((tm, k), lambda i, j: (i, 0)),
                  pl.BlockSpec((k, tn), lambda i, j: (0, j)),
                  pl.BlockSpec((k, V7X_LANES), lambda i, j: (0, 0)),
                  pl.BlockSpec((tm, HEAD_DIM), lambda i, j: (i, 0)),
                  pl.BlockSpec((tm, HEAD_DIM), lambda i, j: (i, 0))],
        out_specs=[pl.BlockSpec((tm, tn), lambda i, j: (i, j)),
                   pl.BlockSpec((tm, V7X_LANES), lambda i, j: (i, 0))],
        out_shape=[jax.ShapeDtypeStruct((t, n), BF16),
                   jax.ShapeDtypeStruct((t, V7X_LANES), F32)],
        compiler_params=_cparams(("parallel", "arbitrary")),
        name="in_projection",
    )(a, w, w_gates, cos, sin)


def _band_block(qb, kb, vb, bias):
    s = lax.dot_general(qb, kb, (((1,), (1,)), ((), ())), preferred_element_type=F32) + bias
    m = jnp.max(s, axis=-1, keepdims=True)
    p = jnp.exp(s - m)
    l = jnp.sum(p, axis=-1, keepdims=True)
    o = jnp.dot(p.astype(BF16), vb, preferred_element_type=F32) / l
    return o, m + jnp.log(l)


def _attn_kernel(q_ref, kp_ref, kc_ref, vp_ref, vc_ref, o_ref, qf, kf, vf, ob, lb):
    sb = ATTN_SUPER
    blk = ATTN_BLOCK
    n = pl.program_id(2)
    qf[...] = q_ref[...].astype(F32)
    kf[0:sb, :] = kp_ref[...].astype(F32)
    kf[sb:2 * sb, :] = kc_ref[...].astype(F32)
    vf[0:sb, :] = vp_ref[...].astype(F32)
    vf[sb:2 * sb, :] = vc_ref[...].astype(F32)

    qi = lax.broadcasted_iota(I32, (blk, 2 * blk), 0)
    kj = lax.broadcasted_iota(I32, (blk, 2 * blk), 1)
    dist = blk + qi - kj
    band = (dist >= 0) & (dist <= blk)
    bias_in = jnp.where(band, 0.0, NEG_BIG).astype(F32)
    prev_ok = jnp.where(n > 0, 0.0, NEG_BIG).astype(F32)
    bias_edge = bias_in + jnp.where(kj < blk, prev_ok, 0.0)

    for p, d in enumerate(DILATIONS):
        per_class = sb // (d * blk)
        for r in range(d):
            for nb in range(per_class):
                q0 = r + d * blk * nb
                k0 = sb + q0 - d * blk
                if d == 1:
                    qb = qf[pl.ds(q0, blk), :]
                    kb = kf[pl.ds(k0, 2 * blk), :]
                    vb = vf[pl.ds(k0, 2 * blk), :]
                else:
                    qb = qf[pl.ds(q0, blk, stride=d), :]
                    kb = kf[pl.ds(k0, 2 * blk, stride=d), :]
                    vb = vf[pl.ds(k0, 2 * blk, stride=d), :]
                bias = bias_edge if nb == 0 else bias_in
                o, lse = _band_block(qb.astype(BF16), kb.astype(BF16), vb.astype(BF16), bias)
                lse_b = jnp.broadcast_to(lse, (blk, HEAD_DIM))
                if d == 1:
                    ob[p, pl.ds(q0, blk), :] = o
                    lb[p, pl.ds(q0, blk), :] = lse_b
                else:
                    ob[p, pl.ds(q0, blk, stride=d), :] = o
                    lb[p, pl.ds(q0, blk, stride=d), :] = lse_b

    l0, l1, l2 = lb[0], lb[1], lb[2]
    mx = jnp.maximum(jnp.maximum(l0, l1), l2)
    w0, w1, w2 = jnp.exp(l0 - mx), jnp.exp(l1 - mx), jnp.exp(l2 - mx)
    out = (w0 * ob[0] + w1 * ob[1] + w2 * ob[2]) / (w0 + w1 + w2)
    o_ref[...] = out.astype(o_ref.dtype)


def dilated_attention(z, batch, seq):
    t = z.shape[0]
    sb = ATTN_SUPER
    nsb = seq // sb
    h = ATTN_HEADS
    cur = lambda off: (lambda b, hh, n: (b * nsb + n, off + hh))
    prev = lambda off: (lambda b, hh, n: (b * nsb + jnp.maximum(n - 1, 0), off + hh))
    spec = lambda im: pl.BlockSpec((sb, HEAD_DIM), im)
    return pl.pallas_call(
        _attn_kernel,
        grid=(batch, h, nsb),
        in_specs=[spec(cur(0)), spec(prev(h)), spec(cur(h)), spec(prev(2 * h)), spec(cur(2 * h))],
        out_specs=spec(cur(0)),
        out_shape=jax.ShapeDtypeStruct((t, ATTN_WIDTH), BF16),
        scratch_shapes=[pltpu.VMEM((sb, HEAD_DIM), F32),
                        pltpu.VMEM((2 * sb, HEAD_DIM), F32),
                        pltpu.VMEM((2 * sb, HEAD_DIM), F32),
                        pltpu.VMEM((len(DILATIONS), sb, HEAD_DIM), F32),
                        pltpu.VMEM((len(DILATIONS), sb, HEAD_DIM), F32)],
        compiler_params=_cparams(("parallel", "parallel", "arbitrary")),
        name="dilated_attention",
    )(z, z, z, z, z)


def _log_sigmoid(x):
    return jnp.minimum(x, 0.0) - jnp.log(1.0 + jnp.exp(-jnp.abs(x)))


def _mlstm_kernel(q_ref, k_ref, v_ref, om_ref, g_ref, gb_ref, cw_ref, cb_ref, mhg_ref, o_ref,
                  xbuf, c_st, n_st, m_st):
    L = MLSTM_CHUNK
    dqk, dv, nh = MLSTM_QK_DIM, MLSTM_V_DIM, MLSTM_HEADS
    qw = nh * dqk
    c = pl.program_id(1)

    @pl.when(c == 0)
    def _():
        xbuf[0:CONV_PAD, :] = jnp.zeros((CONV_PAD, 2 * qw), F32)
        c_st[...] = jnp.zeros_like(c_st)
        n_st[...] = jnp.zeros_like(n_st)
        m_st[...] = jnp.zeros_like(m_st)

    @pl.when(c > 0)
    def _():
        xbuf[0:CONV_PAD, :] = xbuf[L:L + CONV_PAD, :]

    xbuf[CONV_PAD:CONV_PAD + L, 0:qw] = q_ref[...].astype(F32)
    xbuf[CONV_PAD:CONV_PAD + L, qw:2 * qw] = k_ref[...].astype(F32)
    y = cb_ref[...]
    for j in range(CONV_WIDTH):
        y = y + cw_ref[j:j + 1, :] * xbuf[pl.ds(CONV_PAD - CONV_WIDTH + 1 + j, L), :]
    qk = y * jax.nn.sigmoid(y)

    pre = g_ref[...] + gb_ref[...]
    lf = _log_sigmoid(pre)
    row = lax.broadcasted_iota(I32, (L, L), 0)
    col = lax.broadcasted_iota(I32, (L, L), 1)
    causal = col <= row
    tril = jnp.where(causal, 1.0, 0.0).astype(F32)
    bcum = jnp.dot(tril, lf, preferred_element_type=F32, precision=lax.Precision.HIGHEST)
    pre_t = pre.T
    bcum_t = bcum.T

    for h in range(nh):
        i_c = pre[:, h:h + 1]
        b_c = bcum[:, nh + h:nh + h + 1]
        i_r = pre_t[h:h + 1, :]
        b_r = bcum_t[nh + h:nh + h + 1, :]
        g = bcum[L - 1:L, nh + h:nh + h + 1]
        m = m_st[h, 0:1, 0:1]
        qq = (qk[:, h * dqk:(h + 1) * dqk] * (dqk ** -0.5))
        kk = qk[:, qw + h * dqk:qw + (h + 1) * dqk]
        vv = v_ref[:, h * dv:(h + 1) * dv]
        qb = qq.astype(BF16)

        log_d = jnp.where(causal, b_c - b_r + i_r, NEG_BIG)
        inter = b_c + m
        m_t = jnp.maximum(inter, jnp.max(log_d, axis=-1, keepdims=True))
        w_inter = jnp.exp(inter - m_t)
        s = lax.dot_general(qb, kk.astype(BF16), (((1,), (1,)), ((), ())),
                            preferred_element_type=F32) * jnp.exp(log_d - m_t)
        num = (w_inter * jnp.dot(qb, c_st[h].astype(BF16), preferred_element_type=F32)
               + jnp.dot(s.astype(BF16), vv, preferred_element_type=F32))
        den = (w_inter * jnp.sum(qq * n_st[h], axis=-1, keepdims=True)
               + jnp.sum(s, axis=-1, keepdims=True))
        hh = num / jnp.maximum(jnp.abs(den), jnp.exp(-m_t))

        log_w = g - b_c + i_c
        m_new = jnp.maximum(g + m, jnp.max(log_w, axis=0, keepdims=True))
        decay = jnp.exp(g + m - m_new)
        wk = kk * jnp.exp(log_w - m_new)
        c_st[h] = decay * c_st[h] + jnp.dot(wk.T.astype(BF16), vv, preferred_element_type=F32)
        n_st[h] = decay * n_st[h] + jnp.sum(wk, axis=0, keepdims=True)
        m_st[h] = jnp.broadcast_to(m_new, m_st.shape[1:])

        hn = hh * lax.rsqrt(jnp.mean(hh * hh, axis=-1, keepdims=True) + NORM_EPS)
        hn = hn * mhg_ref[:, h * dv:(h + 1) * dv]
        og = jax.nn.sigmoid(om_ref[:, h * dv:(h + 1) * dv].astype(F32))
        o_ref[:, h * dv:(h + 1) * dv] = (og * hn).astype(o_ref.dtype)


def mlstm(z, gates, gate_bias, conv_w, conv_b, mh_norm_g, batch, seq):
    t = z.shape[0]
    L = MLSTM_CHUNK
    nc = seq // L
    qw, vw = MLSTM_QK_WIDTH, MLSTM_V_WIDTH
    q_off = 3 * ATTN_WIDTH
    rows = lambda blk: (lambda b, c: (b * nc + c, blk))
    const = lambda b, c: (0, 0)
    return pl.pallas_call(
        _mlstm_kernel,
        grid=(batch, nc),
        in_specs=[pl.BlockSpec((L, qw), rows(q_off // qw)),
                  pl.BlockSpec((L, qw), rows(q_off // qw + 1)),
                  pl.BlockSpec((L, vw), rows((q_off + 2 * qw) // vw)),
                  pl.BlockSpec((L, vw), rows((q_off + 2 * qw) // vw + 1)),
                  pl.BlockSpec((L, V7X_LANES), rows(0)),
                  pl.BlockSpec((1, V7X_LANES), const),
                  pl.BlockSpec((CONV_WIDTH, 2 * qw), const),
                  pl.BlockSpec((1, 2 * qw), const),
                  pl.BlockSpec((1, vw), const)],
        out_specs=pl.BlockSpec((L, vw), rows(0)),
        out_shape=jax.ShapeDtypeStruct((t, vw), BF16),
        scratch_shapes=[pltpu.VMEM((L + CONV_PAD, 2 * qw), F32),
                        pltpu.VMEM((MLSTM_HEADS, MLSTM_QK_DIM, MLSTM_V_DIM), F32),
                        pltpu.VMEM((MLSTM_HEADS, 1, MLSTM_QK_DIM), F32),
                        pltpu.VMEM((MLSTM_HEADS, 8, V7X_LANES), F32)],
        compiler_params=_cparams(("parallel", "arbitrary")),
        name="mlstm",
    )(z, z, z, z, gates, gate_bias, conv_w, conv_b.reshape(1, 2 * qw), mh_norm_g.reshape(1, vw))


def _outproj_kernel(a1_ref, a2_ref, w_ref, res_ref, o_ref):
    k1 = a1_ref.shape[1]
    acc = jnp.dot(a1_ref[...], w_ref[0:k1, :], preferred_element_type=F32)
    acc = acc + jnp.dot(a2_ref[...], w_ref[k1:, :], preferred_element_type=F32)
    o_ref[...] = res_ref[...] + acc


def out_projection(a1, a2, w, res, *, tm=MM_TM, tn=MM_TN):
    t, k1 = a1.shape
    k2 = a2.shape[1]
    n = w.shape[1]
    tm, tn = min(tm, t), min(tn, n)
    return pl.pallas_call(
        _outproj_kernel,
        grid=(t // tm, n // tn),
        in_specs=[pl.BlockSpec((tm, k1), lambda i, j: (i, 0)),
                  pl.BlockSpec((tm, k2), lambda i, j: (i, 0)),
                  pl.BlockSpec((k1 + k2, tn), lambda i, j: (0, j)),
                  pl.BlockSpec((tm, tn), lambda i, j: (i, j))],
        out_specs=pl.BlockSpec((tm, tn), lambda i, j: (i, j)),
        out_shape=jax.ShapeDtypeStruct((t, n), F32),
        compiler_params=_cparams(("parallel", "arbitrary")),
        name="out_projection",
    )(a1, a2, w, res)


def _pack_bf16_pairs(f):
    half = f.shape[1] // 2
    fb = f.astype(BF16).astype(F32)
    lo = pltpu.bitcast(fb[:, :half], U32) >> 16
    hi = pltpu.bitcast(fb[:, half:], U32) & jnp.uint32(0xFFFF0000)
    return lo | hi


def _unpack_bf16_pairs(w):
    lo = pltpu.bitcast(w << 16, F32).astype(BF16)
    hi = pltpu.bitcast(w & jnp.uint32(0xFFFF0000), F32).astype(BF16)
    return lo, hi


def _router_kernel(h_ref, g_ref, wr_ref, br_ref, fpk_ref, e_ref, gate_ref, rank_ref, cnt_ref, carry):
    i = pl.program_id(0)

    @pl.when(i == 0)
    def _():
        carry[...] = jnp.zeros_like(carry)

    f = _rmsnorm_body(h_ref[...], g_ref[...])
    fpk_ref[...] = _pack_bf16_pairs(f)
    f_hi = f.astype(BF16)
    f_lo = (f - f_hi.astype(F32)).astype(BF16)
    logits = (jnp.dot(f_hi, wr_ref[0], preferred_element_type=F32)
              + (jnp.dot(f_lo, wr_ref[0], preferred_element_type=F32)
                 + jnp.dot(f_hi, wr_ref[1], preferred_element_type=F32))) + br_ref[...]
    lane = lax.broadcasted_iota(I32, logits.shape, 1)
    cur = logits
    vals, idxs = [], []
    for _ in range(TOP_K):
        mx = jnp.max(cur, axis=-1, keepdims=True)
        idx = jnp.min(jnp.where(cur == mx, lane, V7X_LANES), axis=-1, keepdims=True)
        vals.append(mx)
        idxs.append(idx)
        cur = jnp.where(lane == idx, -jnp.inf, cur)
    exps = [jnp.exp(v - vals[0]) for v in vals]
    tot = exps[0] + exps[1] + exps[2] + exps[3]
    e_out = jnp.zeros(logits.shape, I32)
    g_out = jnp.zeros(logits.shape, F32)
    for k in range(TOP_K):
        e_out = jnp.where(lane == k, idxs[k], e_out)
        g_out = jnp.where(lane == k, exps[k] / tot, g_out)
    e_ref[...] = e_out
    gate_ref[...] = g_out

    tm = logits.shape[0]
    onehots = [lane == idxs[k] for k in range(TOP_K)]
    member = jnp.zeros(logits.shape, F32)
    for k in range(TOP_K):
        member = jnp.where(onehots[k], 1.0, member)
    row = lax.broadcasted_iota(I32, (tm, tm), 0)
    col = lax.broadcasted_iota(I32, (tm, tm), 1)
    stril = jnp.where(col < row, 1.0, 0.0).astype(BF16)
    before = carry[0:1, :] + jnp.dot(stril, member.astype(BF16), preferred_element_type=F32)
    r_out = jnp.zeros(logits.shape, I32)
    for k in range(TOP_K):
        rk = jnp.sum(jnp.where(onehots[k], before, 0.0), axis=-1, keepdims=True)
        r_out = jnp.where(lane == k, rk.astype(I32), r_out)
    rank_ref[...] = r_out
    total = carry[0:1, :] + jnp.sum(member, axis=0, keepdims=True)
    carry[...] = jnp.broadcast_to(total, carry.shape)
    cnt_ref[...] = jnp.broadcast_to(total, cnt_ref.shape)


def ffn_norm_router(h, g, w_router, b_router, *, tm=NORM_TM):
    t, d = h.shape
    nt = t // tm
    wr = jnp.zeros((d, V7X_LANES), F32).at[:, :N_EXPERTS].set(w_router)
    wr_hi = wr.astype(BF16)
    wr = jnp.stack([wr_hi, (wr - wr_hi.astype(F32)).astype(BF16)])
    br = jnp.full((1, V7X_LANES), NEG_BIG, F32).at[0, :N_EXPERTS].set(b_router)
    rows = lambda i: (i, 0)
    const = lambda i: (0, 0)
    return pl.pallas_call(
        _router_kernel,
        grid=(nt,),
        in_specs=[pl.BlockSpec((tm, d), rows),
                  pl.BlockSpec((1, d), const),
                  pl.BlockSpec((2, d, V7X_LANES), lambda i: (0, 0, 0)),
                  pl.BlockSpec((1, V7X_LANES), const)],
        out_specs=[pl.BlockSpec((tm, d // 2), rows),
                   pl.BlockSpec((tm, V7X_LANES), rows),
                   pl.BlockSpec((tm, V7X_LANES), rows),
                   pl.BlockSpec((tm, V7X_LANES), rows),
                   pl.BlockSpec((8, V7X_LANES), const)],
        out_shape=[jax.ShapeDtypeStruct((t, d // 2), U32),
                   jax.ShapeDtypeStruct((t, V7X_LANES), I32),
                   jax.ShapeDtypeStruct((t, V7X_LANES), F32),
                   jax.ShapeDtypeStruct((t, V7X_LANES), I32),
                   jax.ShapeDtypeStruct((8, V7X_LANES), F32)],
        scratch_shapes=[pltpu.VMEM((8, V7X_LANES), F32)],
        compiler_params=_cparams(("arbitrary",)),
        name="ffn_norm_router",
    )(h, g.reshape(1, d), wr, br)


def routing_tables(top_e, rank, counts, n_tokens):
    tk = n_tokens * TOP_K
    tm = EXPERT_TM
    n_blk = -(-(tk + N_EXPERTS * (tm - 1)) // tm)
    padded = (counts + tm - 1) // tm * tm
    pend = jnp.cumsum(padded)
    pstart = pend - padded
    pos = (pstart[top_e] + rank).reshape(tk).astype(I32)
    blk_start = jnp.arange(n_blk, dtype=I32) * tm
    blk_e = jnp.minimum(jnp.sum(blk_start[:, None] >= pend[None, :], axis=1), N_EXPERTS - 1).astype(I32)
    n_used = (pend[-1] // tm).astype(I32).reshape(1)
    is_last = jnp.any((blk_start[:, None] + tm == pend[None, :]) & (padded[None, :] > 0), axis=1)
    zero_blk = (is_last | (blk_start >= pend[-1])).astype(I32)
    return pos, zero_blk, (blk_e, n_used[0], pstart // tm, padded // tm, n_blk, counts)


SCHED_ACTIVE = 1
SCHED_FIRST = 2
SCHED_PREFETCH = 4
SCHED_HALF = 8


def resident_schedule(layout, n_col):
    blk_e, n_used, first_blk, n_blks, n_blk, counts = layout
    s = jnp.arange(n_blk * n_col, dtype=I32)
    n_active = n_col * n_used
    active = s < n_active
    sc = jnp.minimum(s, jnp.maximum(n_active - 1, 0))
    e = blk_e[jnp.minimum(sc // n_col, n_blk - 1)]
    b0 = first_blk[e]
    nb = jnp.maximum(n_blks[e], 1)
    local = sc - n_col * b0
    col = local // nb
    j = local % nb
    blk = b0 + j
    out_blk = jnp.where(active, blk, s // n_col)
    out_col = jnp.where(active, col, s % n_col)
    nxt = sc - j + nb
    has_next = nxt < n_active
    nxt = jnp.minimum(nxt, jnp.maximum(n_active - 1, 0))
    half_empty = counts[e] - j * EXPERT_TM <= EXPERT_TM // 2
    flags = jnp.where(active, SCHED_ACTIVE + SCHED_HALF * half_empty
                      + (j == 0) * (SCHED_FIRST + SCHED_PREFETCH * has_next), 0)
    as_i32 = lambda a: a.astype(I32)
    return tuple(map(as_i32, (blk, out_blk, out_col, e, col, flags, e[nxt], col[nxt])))


def _dispatch_kernel(pos_ref, zero_ref, f_ref, xs_hbm, zbuf, sem, zsem, *, rows, n_blk):
    i = pl.program_id(0)
    tm = f_ref.shape[0]

    def zero_copy(b):
        return pltpu.make_async_copy(zbuf, xs_hbm.at[pl.ds(b * rows, rows)], zsem)

    @pl.when(i == 0)
    def _():
        zbuf[...] = jnp.zeros_like(zbuf)

        def issue(b, c):
            @pl.when(zero_ref[b] != 0)
            def _():
                zero_copy(b).start()
            return c

        def drain(b, c):
            @pl.when(zero_ref[b] != 0)
            def _():
                zero_copy(b).wait()
            return c

        lax.fori_loop(0, n_blk, issue, 0)
        lax.fori_loop(0, n_blk, drain, 0)

    def row_copy(t, k):
        row = pos_ref[(i * tm + t) * TOP_K + k]
        return pltpu.make_async_copy(f_ref.at[pl.ds(t, 1)], xs_hbm.at[pl.ds(row, 1)], sem)

    def drain_rows(t, c):
        for k in range(TOP_K):
            row_copy(t, k).wait()
        return c

    for t in range(tm):
        for k in range(TOP_K):
            row_copy(t, k).start(priority=k % 2)
    lax.fori_loop(0, tm, drain_rows, 0, unroll=DMA_ISSUE_UNROLL)


def dispatch(fpk, pos, zero_blk, *, rows=EXPERT_TM, tm=NORM_TM):
    t, half = fpk.shape
    n_blk = zero_blk.shape[0]
    grid_spec = pltpu.PrefetchScalarGridSpec(
        num_scalar_prefetch=2,
        grid=(t // tm,),
        in_specs=[pl.BlockSpec((tm, half), lambda i, p, z: (i, 0))],
        out_specs=pl.BlockSpec(memory_space=pl.ANY),
        scratch_shapes=[pltpu.VMEM((rows, half), fpk.dtype), pltpu.SemaphoreType.DMA(()),
                        pltpu.SemaphoreType.DMA(())],
    )
    return pl.pallas_call(
        functools.partial(_dispatch_kernel, rows=rows, n_blk=n_blk),
        grid_spec=grid_spec,
        out_shape=jax.ShapeDtypeStruct((n_blk * rows, half), fpk.dtype),
        compiler_params=_cparams(("arbitrary",), has_side_effects=True),
        name="dispatch",
    )(pos, zero_blk, fpk)


def _weight_tile_refresh(s, flag, tile_copies, cast):
    @pl.when((flag & SCHED_FIRST) != 0)
    def _():
        @pl.when(s == 0)
        def _():
            for cp in tile_copies(False):
                cp.start()

        for cp in tile_copies(False):
            cp.wait()
        cast()

        @pl.when((flag & SCHED_PREFETCH) != 0)
        def _():
            for cp in tile_copies(True):
                cp.start(priority=1)


def _expert_up_kernel(xb_ref, ob_ref, oc_ref, se_ref, sc_ref, flag_ref, ne_ref, nc_ref, xs_ref, w_hbm, bg_ref,
                      bu_ref, act_ref, wstage, wgb, wub, sem, *, tf, nf):
    s = pl.program_id(0)
    half = xs_ref.shape[1]
    flag = flag_ref[s]

    def tile_copies(following):
        e = (ne_ref if following else se_ref)[s]
        c = (nc_ref if following else sc_ref)[s]
        gate_cols = pl.ds(pl.multiple_of(c * tf, tf), tf)
        up_cols = pl.ds(pl.multiple_of((nf + c) * tf, tf), tf)
        return (pltpu.make_async_copy(w_hbm.at[e, :, gate_cols], wstage.at[0], sem.at[0]),
                pltpu.make_async_copy(w_hbm.at[e, :, up_cols], wstage.at[1], sem.at[1]))

    def cast():
        wgb[...] = wstage[0].astype(BF16)
        wub[...] = wstage[1].astype(BF16)

    _weight_tile_refresh(s, flag, tile_copies, cast)

    @pl.when((flag & SCHED_ACTIVE) != 0)
    def _():
        def rows_compute(rows):
            lo, hi = _unpack_bf16_pairs(xs_ref[0:rows, :])
            glu = (jnp.dot(lo, wgb[0:half, :], preferred_element_type=F32)
                   + jnp.dot(hi, wgb[half:, :], preferred_element_type=F32) + bg_ref[...])
            up = (jnp.dot(lo, wub[0:half, :], preferred_element_type=F32)
                  + jnp.dot(hi, wub[half:, :], preferred_element_type=F32) + bu_ref[...])
            glu = jnp.minimum(glu, SWIGLU_LIMIT)
            up = jnp.clip(up, -SWIGLU_LIMIT, SWIGLU_LIMIT)
            act = (up + 1.0) * (glu * jax.nn.sigmoid(SWIGLU_ALPHA * glu))
            act_ref[0:rows, :] = act.astype(act_ref.dtype)

        tm = xs_ref.shape[0]

        @pl.when((flag & SCHED_HALF) == 0)
        def _():
            rows_compute(tm)

        @pl.when((flag & SCHED_HALF) != 0)
        def _():
            rows_compute(tm // 2)
            act_ref[tm // 2:, :] = jnp.zeros((tm - tm // 2, act_ref.shape[1]), act_ref.dtype)

    @pl.when((flag & SCHED_ACTIVE) == 0)
    def _():
        act_ref[...] = jnp.zeros_like(act_ref)


def expert_up(xs, w_gate_up, b_gate_up, sched, *, tm=EXPERT_TM, tf=EXPERT_TF):
    n_rows, half = xs.shape
    d = 2 * half
    ff = EXPERT_FF
    nf = ff // tf
    n_steps = sched[0].shape[0]
    grid_spec = pltpu.PrefetchScalarGridSpec(
        num_scalar_prefetch=len(sched),
        grid=(n_steps,),
        in_specs=[pl.BlockSpec((tm, half), lambda s, *t: (t[0][s], 0)),
                  pl.BlockSpec(memory_space=pl.ANY),
                  pl.BlockSpec((None, 1, tf), lambda s, *t: (t[3][s], 0, t[4][s])),
                  pl.BlockSpec((None, 1, tf), lambda s, *t: (t[3][s], 0, nf + t[4][s]))],
        out_specs=pl.BlockSpec((tm, tf), lambda s, *t: (t[1][s], t[2][s])),
        scratch_shapes=[pltpu.VMEM((2, d, tf), F32), pltpu.VMEM((d, tf), BF16), pltpu.VMEM((d, tf), BF16),
                        pltpu.SemaphoreType.DMA((2,))],
    )
    return pl.pallas_call(
        functools.partial(_expert_up_kernel, tf=tf, nf=nf),
        grid_spec=grid_spec,
        out_shape=jax.ShapeDtypeStruct((n_rows, ff), BF16),
        compiler_params=_cparams(("arbitrary",)),
        name="expert_up",
    )(*sched, xs, w_gate_up, b_gate_up, b_gate_up)


def _expert_down_kernel(xb_ref, ob_ref, oc_ref, se_ref, sc_ref, flag_ref, ne_ref, nc_ref, act_ref, w_hbm, bd_ref,
                        y_ref, wstage, wdb, sem, *, tn):
    s = pl.program_id(0)
    flag = flag_ref[s]

    def tile_copies(following):
        e = (ne_ref if following else se_ref)[s]
        c = (nc_ref if following else sc_ref)[s]
        cols = pl.ds(pl.multiple_of(c * tn, tn), tn)
        return (pltpu.make_async_copy(w_hbm.at[e, :, cols], wstage, sem),)

    def cast():
        wdb[...] = wstage[...].astype(BF16)

    _weight_tile_refresh(s, flag, tile_copies, cast)

    @pl.when((flag & SCHED_ACTIVE) != 0)
    def _():
        def rows_compute(rows):
            y = jnp.dot(act_ref[0:rows, :], wdb[...], preferred_element_type=F32) + bd_ref[...]
            y_ref[0:rows, :] = _pack_bf16_pairs(y)

        tm = act_ref.shape[0]

        @pl.when((flag & SCHED_HALF) == 0)
        def _():
            rows_compute(tm)

        @pl.when((flag & SCHED_HALF) != 0)
        def _():
            rows_compute(tm // 2)
            y_ref[tm // 2:, :] = jnp.zeros((tm - tm // 2, y_ref.shape[1]), y_ref.dtype)

    @pl.when((flag & SCHED_ACTIVE) == 0)
    def _():
        y_ref[...] = jnp.zeros_like(y_ref)


def expert_down(act, w_down, b_down, sched, *, tm=EXPERT_TM, tn=EXPERT_TN):
    n_rows, ff = act.shape
    d = w_down.shape[2]
    n_steps = sched[0].shape[0]
    grid_spec = pltpu.PrefetchScalarGridSpec(
        num_scalar_prefetch=len(sched),
        grid=(n_steps,),
        in_specs=[pl.BlockSpec((tm, ff), lambda s, *t: (t[0][s], 0)),
                  pl.BlockSpec(memory_space=pl.ANY),
                  pl.BlockSpec((None, 1, tn), lambda s, *t: (t[3][s], 0, t[4][s]))],
        out_specs=pl.BlockSpec((tm, tn // 2), lambda s, *t: (t[1][s], t[2][s])),
        scratch_shapes=[pltpu.VMEM((ff, tn), F32), pltpu.VMEM((ff, tn), BF16), pltpu.SemaphoreType.DMA(())],
    )
    return pl.pallas_call(
        functools.partial(_expert_down_kernel, tn=tn),
        grid_spec=grid_spec,
        out_shape=jax.ShapeDtypeStruct((n_rows, d // 2), U32),
        compiler_params=_cparams(("arbitrary",)),
        name="expert_down",
    )(*sched, act, w_down, b_down)


def _combine_ple_kernel(pos_ref, y_hbm, h_ref, gate_ref, gn_ref, w_ref, p_ref, wp_ref, gf_ref, o_ref, gbuf, a2, sem,
                        *, tm, tn, pair, final):
    i = pl.program_id(0)
    j = pl.program_id(1)
    d = o_ref.shape[1]

    def row_copy(blk, t, k):
        row = pos_ref[(blk * tm + t) * TOP_K + k]
        return pltpu.make_async_copy(y_hbm.at[pl.ds(row, 1)], gbuf.at[k, pl.ds(t, 1)], sem)

    def issue(blk, t0, count):
        def body(t, carry):
            for k in range(TOP_K):
                row_copy(blk, t0 + t, k).start(priority=1)
            return carry

        lax.fori_loop(0, count, body, 0, unroll=DMA_ISSUE_UNROLL)

    def drain(t, carry):
        for k in range(TOP_K):
            row_copy(i, t, k).wait()
        return carry

    per_step = tm // (d // tn)

    @pl.when(j == 0)
    def _():
        @pl.when(i == 0)
        def _():
            issue(0, 0, tm)

        lax.fori_loop(0, tm, drain, 0, unroll=DMA_ISSUE_UNROLL)
        gates = gate_ref[...]
        hi_mask = jnp.uint32(0xFFFF0000)
        half = pair // 2
        ssq = jnp.zeros((tm, 1), F32)
        for c in range(d // pair):
            for part, unpack in enumerate((lambda w: w << 16, lambda w: w & hi_mask)):
                terms = [gates[:, k:k + 1] * pltpu.bitcast(unpack(gbuf[k, :, c * half:(c + 1) * half]), F32)
                         for k in range(TOP_K)]
                cols = slice(c * pair + part * half, c * pair + (part + 1) * half)
                val = h_ref[:, cols] + ((terms[0] + terms[1]) + (terms[2] + terms[3]))
                o_ref[:, cols] = val
                ssq = ssq + jnp.sum(val * val, axis=-1, keepdims=True)
        r = lax.rsqrt(ssq / d + NORM_EPS)
        for c0 in range(0, d, half):
            a2[:, c0:c0 + half] = (o_ref[:, c0:c0 + half] * r * gn_ref[:, c0:c0 + half]).astype(a2.dtype)

    last_i = pl.num_programs(0) - 1
    nxt = jnp.minimum(i + 1, last_i)
    for t in range(per_step):
        for k in range(TOP_K):
            row_copy(nxt, j * per_step + t, k).start(priority=1 if k else 0)

    gate = jax.nn.sigmoid(jnp.dot(a2[...], w_ref[...], preferred_element_type=F32))
    emb = jnp.dot(p_ref[...], wp_ref[...], preferred_element_type=F32)
    cols = pl.ds(pl.multiple_of(j * tn, tn), tn)
    o_ref[:, cols] = o_ref[:, cols] + gate * emb

    @pl.when((i == last_i) & (j == pl.num_programs(1) - 1))
    def _():
        lax.fori_loop(0, tm, drain, 0, unroll=DMA_ISSUE_UNROLL)

    if final:
        @pl.when(j == pl.num_programs(1) - 1)
        def _():
            o_ref[...] = _rmsnorm_body(o_ref[...], gf_ref[...])


def combine_ple(y, pos, h, gates, g_ple, w_gate, p, w_proj, g_final, *, pair, tm=COMBINE_PLE_TM, tn=MM_TN):
    t, d = h.shape
    pk = p.shape[1]
    tm, tn = min(tm, t), min(tn, d)
    final = g_final is not None
    g_last = (g_final if final else jnp.ones((d,), F32)).reshape(1, d)
    rows = lambda i, j, pos: (i, 0)
    const = lambda i, j, pos: (0, 0)
    colt = lambda i, j, pos: (0, j)
    grid_spec = pltpu.PrefetchScalarGridSpec(
        num_scalar_prefetch=1,
        grid=(t // tm, d // tn),
        in_specs=[pl.BlockSpec(memory_space=pl.ANY),
                  pl.BlockSpec((tm, d), rows),
                  pl.BlockSpec((tm, V7X_LANES), rows),
                  pl.BlockSpec((1, d), const),
                  pl.BlockSpec((d, tn), colt),
                  pl.BlockSpec((tm, pk), rows),
                  pl.BlockSpec((pk, tn), colt),
                  pl.BlockSpec((1, d), const)],
        out_specs=pl.BlockSpec((tm, d), rows),
        scratch_shapes=[pltpu.VMEM((TOP_K, tm, d // 2), U32), pltpu.VMEM((tm, d), BF16), pltpu.SemaphoreType.DMA(())],
    )
    return pl.pallas_call(
        functools.partial(_combine_ple_kernel, tm=tm, tn=tn, pair=pair, final=final),
        grid_spec=grid_spec,
        out_shape=jax.ShapeDtypeStruct((t, d), F32),
        compiler_params=_cparams(("arbitrary", "arbitrary")),
        name="combine_ple",
    )(pos, y, h, gates, g_ple.reshape(1, d), w_gate, p, w_proj, g_last)


def kernel(x, p, positions, attn_norm_g, w_in, conv_w, conv_b, b_igate, b_fgate, mh_norm_g, w_out,
           ffn_norm_g, w_router, b_router, w_gate_up, b_gate_up, w_down, b_down, ple_norm_g, w_ple_gate,
           w_ple_proj, final_norm_g):
    batch, seq, d = x.shape
    depth = p.shape[0]
    t = batch * seq
    n_main = 3 * ATTN_WIDTH + 2 * MLSTM_QK_WIDTH + 2 * MLSTM_V_WIDTH
    nh = MLSTM_HEADS
    h = x.reshape(t, d)
    cos, sin = rope_tables(positions.reshape(t, 1))
    for i in range(depth):
        a = rmsnorm(h, attn_norm_g[i], BF16)
        w_gates = jnp.zeros((d, V7X_LANES), BF16).at[:, :2 * nh].set(w_in[i][:, n_main:].astype(BF16))
        z, gates = in_projection(a, w_in[i].astype(BF16), n_main, w_gates, cos, sin)
        attn = dilated_attention(z, batch, seq)
        gate_bias = jnp.zeros((1, V7X_LANES), F32).at[0, :nh].set(b_igate[i]).at[0, nh:2 * nh].set(b_fgate[i])
        ml = mlstm(z, gates, gate_bias, conv_w[i], conv_b[i], mh_norm_g[i], batch, seq)
        h = out_projection(attn, ml, w_out[i].astype(BF16), h)
        fpk, top_e, top_g, rank, counts = ffn_norm_router(h, ffn_norm_g[i], w_router[i], b_router[i])
        pos, zero_blk, layout = routing_tables(top_e[:, :TOP_K], rank[:, :TOP_K],
                                               counts[0, :N_EXPERTS].astype(I32), t)
        xs = dispatch(fpk, pos, zero_blk)
        act = expert_up(xs, w_gate_up[i], b_gate_up[i].reshape(N_EXPERTS, 1, 2 * EXPERT_FF),
                        resident_schedule(layout, EXPERT_FF // EXPERT_TF))
        tn = min(EXPERT_TN, d)
        y = expert_down(act, w_down[i], b_down[i].reshape(N_EXPERTS, 1, d),
                        resident_schedule(layout, d // tn), tn=tn)
        h = combine_ple(y, pos, h, top_g, ple_norm_g[i], w_ple_gate[i].astype(BF16),
                        p[i].reshape(t, -1).astype(BF16), w_ple_proj[i].astype(BF16),
                        final_norm_g if i == depth - 1 else None, pair=tn)
    return h.astype(x.dtype).reshape(batch, seq, d)
```

```python
import functools

import jax
import jax.numpy as jnp
from jax import lax
from jax.experimental import pallas as pl
from jax.experimental.pallas import tpu as pltpu

F32 = jnp.float32
BF16 = jnp.bfloat16
U32 = jnp.uint32
I32 = jnp.int32

V7X_VMEM_BYTES = 64 * 1024 * 1024
V7X_LANES = 128
VMEM_LIMIT = V7X_VMEM_BYTES - 8 * 1024 * 1024

ATTN_HEADS = 16
HEAD_DIM = 128
ATTN_WIDTH = ATTN_HEADS * HEAD_DIM
ATTN_BLOCK = 128
DILATIONS = (1, 4, 16)
ATTN_SUPER = ATTN_BLOCK * max(DILATIONS)
ROPE_THETA = 10000.0
MLSTM_HEADS = 4
MLSTM_QK_DIM = 256
MLSTM_V_DIM = 512
MLSTM_QK_WIDTH = MLSTM_HEADS * MLSTM_QK_DIM
MLSTM_V_WIDTH = MLSTM_HEADS * MLSTM_V_DIM
MLSTM_CHUNK = 256
CONV_WIDTH = 4
CONV_PAD = 8
N_EXPERTS = 32
TOP_K = 4
EXPERT_FF = 1536
SWIGLU_LIMIT = 7.0
SWIGLU_ALPHA = 1.702
NORM_EPS = 1e-6
NEG_BIG = -1e30

MM_TM = 1024
MM_TN = 1024
NORM_TM = 256
EXPERT_TM = 512
EXPERT_TF = 512
EXPERT_TN = 2048
COMBINE_PLE_TM = 256
DMA_ISSUE_UNROLL = 4


def _cparams(sem, **kw):
    return pltpu.CompilerParams(dimension_semantics=sem, vmem_limit_bytes=VMEM_LIMIT, **kw)


def _rmsnorm_body(x, g):
    r = lax.rsqrt(jnp.mean(x * x, axis=-1, keepdims=True) + NORM_EPS)
    return x * r * g


def _rmsnorm_kernel(x_ref, g_ref, o_ref):
    o_ref[...] = _rmsnorm_body(x_ref[...], g_ref[...]).astype(o_ref.dtype)


def rmsnorm(x, g, out_dtype):
    t, d = x.shape
    return pl.pallas_call(
        _rmsnorm_kernel,
        grid=(t // NORM_TM,),
        in_specs=[pl.BlockSpec((NORM_TM, d), lambda i: (i, 0)),
                  pl.BlockSpec((1, d), lambda i: (0, 0))],
        out_specs=pl.BlockSpec((NORM_TM, d), lambda i: (i, 0)),
        out_shape=jax.ShapeDtypeStruct((t, d), out_dtype),
        compiler_params=_cparams(("parallel",)),
        name="rmsnorm",
    )(x, g.reshape(1, d))


def _rope_table_kernel(pos_ref, freq_ref, sign_ref, cos_ref, sin_ref):
    ang = pos_ref[...].astype(F32) * freq_ref[...]
    cos_ref[...] = jnp.cos(ang)
    sin_ref[...] = jnp.sin(ang) * sign_ref[...]


def rope_tables(positions_col):
    t = positions_col.shape[0]
    half = HEAD_DIM // 2
    inv_freq = jnp.power(ROPE_THETA, -jnp.arange(half, dtype=F32) / half)
    freq = jnp.concatenate([inv_freq, inv_freq]).reshape(1, HEAD_DIM)
    sign = jnp.concatenate([-jnp.ones((half,), F32), jnp.ones((half,), F32)]).reshape(1, HEAD_DIM)
    tm = 512
    return pl.pallas_call(
        _rope_table_kernel,
        grid=(t // tm,),
        in_specs=[pl.BlockSpec((tm, 1), lambda i: (i, 0)),
                  pl.BlockSpec((1, HEAD_DIM), lambda i: (0, 0)),
                  pl.BlockSpec((1, HEAD_DIM), lambda i: (0, 0))],
        out_specs=[pl.BlockSpec((tm, HEAD_DIM), lambda i: (i, 0))] * 2,
        out_shape=[jax.ShapeDtypeStruct((t, HEAD_DIM), F32)] * 2,
        compiler_params=_cparams(("parallel",)),
        name="rope_tables",
    )(positions_col, freq, sign)


def _inproj_kernel(a_ref, w_ref, wg_ref, cos_ref, sin_ref, z_ref, gate_ref, *, n_q_tiles, n_k_tiles,
                   q_scale):
    j = pl.program_id(1)

    @pl.when(j == 0)
    def _():
        gate_ref[...] = jnp.dot(a_ref[...], wg_ref[...], preferred_element_type=F32)

    def rope(scale):
        cos = cos_ref[...]
        sin = sin_ref[...]
        pair = 2 * HEAD_DIM
        for c0 in range(0, z_ref.shape[1], pair):
            acc = jnp.dot(a_ref[...], w_ref[:, c0:c0 + pair], preferred_element_type=F32)
            for c in range(c0, c0 + pair, HEAD_DIM):
                t = acc[:, c - c0:c - c0 + HEAD_DIM]
                r = pltpu.roll(t, HEAD_DIM // 2, axis=1)
                z_ref[:, c:c + HEAD_DIM] = ((t * cos + r * sin) * scale).astype(z_ref.dtype)

    @pl.when(j < n_q_tiles)
    def _():
        rope(q_scale)

    @pl.when((j >= n_q_tiles) & (j < n_q_tiles + n_k_tiles))
    def _():
        rope(1.0)

    @pl.when(j >= n_q_tiles + n_k_tiles)
    def _():
        z_ref[...] = jnp.dot(a_ref[...], w_ref[...], preferred_element_type=F32).astype(z_ref.dtype)


def in_projection(a, w, n, w_gates, cos, sin, *, tm=MM_TM, tn=MM_TN):
    t, k = a.shape
    tm, tn = min(tm, t), min(tn, n)
    kern = functools.partial(_inproj_kernel, n_q_tiles=ATTN_WIDTH // tn, n_k_tiles=ATTN_WIDTH // tn,
                             q_scale=HEAD_DIM ** -0.5)
    return pl.pallas_call(
        kern,
        grid=(t // tm, n // tn),
        in_specs=[pl.BlockSpec((tm, k), lambda i, j: (i, 0)),
                  pl.BlockSpec((k, tn), lambda i, j: (0, j)),
                  pl.BlockSpec((k, V7X_LANES), lambda i, j: (0, 0)),
                  pl.BlockSpec((tm, HEAD_DIM), lambda i, j: (i, 0)),
                  pl.BlockSpec((tm, HEAD_DIM), lambda i, j: (i, 0))],
        out_specs=[pl.BlockSpec((tm, tn), lambda i, j: (i, j)),
                   pl.BlockSpec((tm, V7X_LANES), lambda i, j: (i, 0))],
        out_shape=[jax.ShapeDtypeStruct((t, n), BF16),
                   jax.ShapeDtypeStruct((t, V7X_LANES), F32)],
        compiler_params=_cparams(("parallel", "arbitrary")),
        name="in_projection",
    )(a, w, w_gates, cos, sin)


def _band_block(qb, kb, vb, bias):
    s = lax.dot_general(qb, kb, (((1,), (1,)), ((), ())), preferred_element_type=F32) + bias
    m = jnp.max(s, axis=-1, keepdims=True)
    p = jnp.exp(s - m)
    l = jnp.sum(p, axis=-1, keepdims=True)
    o = jnp.dot(p.astype(BF16), vb, preferred_element_type=F32) / l
    return o, m + jnp.log(l)


def _attn_kernel(q_ref, kp_ref, kc_ref, vp_ref, vc_ref, o_ref, qf, kf, vf, ob, lb):
    sb = ATTN_SUPER
    blk = ATTN_BLOCK
    n = pl.program_id(2)
    qf[...] = q_ref[...].astype(F32)
    kf[0:sb, :] = kp_ref[...].astype(F32)
    kf[sb:2 * sb, :] = kc_ref[...].astype(F32)
    vf[0:sb, :] = vp_ref[...].astype(F32)
    vf[sb:2 * sb, :] = vc_ref[...].astype(F32)

    qi = lax.broadcasted_iota(I32, (blk, 2 * blk), 0)
    kj = lax.broadcasted_iota(I32, (blk, 2 * blk), 1)
    dist = blk + qi - kj
    band = (dist >= 0) & (dist <= blk)
    bias_in = jnp.where(band, 0.0, NEG_BIG).astype(F32)
    prev_ok = jnp.where(n > 0, 0.0, NEG_BIG).astype(F32)
    bias_edge = bias_in + jnp.where(kj < blk, prev_ok, 0.0)

    for p, d in enumerate(DILATIONS):
        per_class = sb // (d * blk)
        for r in range(d):
            for nb in range(per_class):
                q0 = r + d * blk * nb
                k0 = sb + q0 - d * blk
                if d == 1:
                    qb = qf[pl.ds(q0, blk), :]
                    kb = kf[pl.ds(k0, 2 * blk), :]
                    vb = vf[pl.ds(k0, 2 * blk), :]
                else:
                    qb = qf[pl.ds(q0, blk, stride=d), :]
                    kb = kf[pl.ds(k0, 2 * blk, stride=d), :]
                    vb = vf[pl.ds(k0, 2 * blk, stride=d), :]
                bias = bias_edge if nb == 0 else bias_in
                o, lse = _band_block(qb.astype(BF16), kb.astype(BF16), vb.astype(BF16), bias)
                lse_b = jnp.broadcast_to(lse, (blk, HEAD_DIM))
                if d == 1:
                    ob[p, pl.ds(q0, blk), :] = o
                    lb[p, pl.ds(q0, blk), :] = lse_b
                else:
                    ob[p, pl.ds(q0, blk, stride=d), :] = o
                    lb[p, pl.ds(q0, blk, stride=d), :] = lse_b

    l0, l1, l2 = lb[0], lb[1], lb[2]
    mx = jnp.maximum(jnp.maximum(l0, l1), l2)
    w0, w1, w2 = jnp.exp(l0 - mx), jnp.exp(l1 - mx), jnp.exp(l2 - mx)
    out = (w0 * ob[0] + w1 * ob[1] + w2 * ob[2]) / (w0 + w1 + w2)
    o_ref[...] = out.astype(o_ref.dtype)


def dilated_attention(z, batch, seq):
    t = z.shape[0]
    sb = ATTN_SUPER
    nsb = seq // sb
    h = ATTN_HEADS
    cur = lambda off: (lambda b, hh, n: (b * nsb + n, off + hh))
    prev = lambda off: (lambda b, hh, n: (b * nsb + jnp.maximum(n - 1, 0), off + hh))
    spec = lambda im: pl.BlockSpec((sb, HEAD_DIM), im)
    return pl.pallas_call(
        _attn_kernel,
        grid=(batch, h, nsb),
        in_specs=[spec(cur(0)), spec(prev(h)), spec(cur(h)), spec(prev(2 * h)), spec(cur(2 * h))],
        out_specs=spec(cur(0)),
        out_shape=jax.ShapeDtypeStruct((t, ATTN_WIDTH), BF16),
        scratch_shapes=[pltpu.VMEM((sb, HEAD_DIM), F32),
                        pltpu.VMEM((2 * sb, HEAD_DIM), F32),
                        pltpu.VMEM((2 * sb, HEAD_DIM), F32),
                        pltpu.VMEM((len(DILATIONS), sb, HEAD_DIM), F32),
                        pltpu.VMEM((len(DILATIONS), sb, HEAD_DIM), F32)],
        compiler_params=_cparams(("parallel", "parallel", "arbitrary")),
        name="dilated_attention",
    )(z, z, z, z, z)


def _log_sigmoid(x):
    return jnp.minimum(x, 0.0) - jnp.log(1.0 + jnp.exp(-jnp.abs(x)))


def _mlstm_kernel(q_ref, k_ref, v_ref, om_ref, g_ref, gb_ref, cw_ref, cb_ref, mhg_ref, o_ref,
                  xbuf, c_st, n_st, m_st):
    L = MLSTM_CHUNK
    dqk, dv, nh = MLSTM_QK_DIM, MLSTM_V_DIM, MLSTM_HEADS
    qw = nh * dqk
    c = pl.program_id(1)

    @pl.when(c == 0)
    def _():
        xbuf[0:CONV_PAD, :] = jnp.zeros((CONV_PAD, 2 * qw), F32)
        c_st[...] = jnp.zeros_like(c_st)
        n_st[...] = jnp.zeros_like(n_st)
        m_st[...] = jnp.zeros_like(m_st)

    @pl.when(c > 0)
    def _():
        xbuf[0:CONV_PAD, :] = xbuf[L:L + CONV_PAD, :]

    xbuf[CONV_PAD:CONV_PAD + L, 0:qw] = q_ref[...].astype(F32)
    xbuf[CONV_PAD:CONV_PAD + L, qw:2 * qw] = k_ref[...].astype(F32)
    y = cb_ref[...]
    for j in range(CONV_WIDTH):
        y = y + cw_ref[j:j + 1, :] * xbuf[pl.ds(CONV_PAD - CONV_WIDTH + 1 + j, L), :]
    qk = y * jax.nn.sigmoid(y)

    pre = g_ref[...] + gb_ref[...]
    lf = _log_sigmoid(pre)
    row = lax.broadcasted_iota(I32, (L, L), 0)
    col = lax.broadcasted_iota(I32, (L, L), 1)
    causal = col <= row
    tril = jnp.where(causal, 1.0, 0.0).astype(F32)
    bcum = jnp.dot(tril, lf, preferred_element_type=F32, precision=lax.Precision.HIGHEST)
    pre_t = pre.T
    bcum_t = bcum.T

    for h in range(nh):
        i_c = pre[:, h:h + 1]
        b_c = bcum[:, nh + h:nh + h + 1]
        i_r = pre_t[h:h + 1, :]
        b_r = bcum_t[nh + h:nh + h + 1, :]
        g = bcum[L - 1:L, nh + h:nh + h + 1]
        m = m_st[h, 0:1, 0:1]
        qq = (qk[:, h * dqk:(h + 1) * dqk] * (dqk ** -0.5))
        kk = qk[:, qw + h * dqk:qw + (h + 1) * dqk]
        vv = v_ref[:, h * dv:(h + 1) * dv]
        qb = qq.astype(BF16)

        log_d = jnp.where(causal, b_c - b_r + i_r, NEG_BIG)
        inter = b_c + m
        m_t = jnp.maximum(inter, jnp.max(log_d, axis=-1, keepdims=True))
        w_inter = jnp.exp(inter - m_t)
        s = lax.dot_general(qb, kk.astype(BF16), (((1,), (1,)), ((), ())),
                            preferred_element_type=F32) * jnp.exp(log_d - m_t)
        num = (w_inter * jnp.dot(qb, c_st[h].astype(BF16), preferred_element_type=F32)
               + jnp.dot(s.astype(BF16), vv, preferred_element_type=F32))
        den = (w_inter * jnp.sum(qq * n_st[h], axis=-1, keepdims=True)
               + jnp.sum(s, axis=-1, keepdims=True))
        hh = num / jnp.maximum(jnp.abs(den), jnp.exp(-m_t))

        log_w = g - b_c + i_c
        m_new = jnp.maximum(g + m, jnp.max(log_w, axis=0, keepdims=True))
        decay = jnp.exp(g + m - m_new)
        wk = kk * jnp.exp(log_w - m_new)
        c_st[h] = decay * c_st[h] + jnp.dot(wk.T.astype(BF16), vv, preferred_element_type=F32)
        n_st[h] = decay * n_st[h] + jnp.sum(wk, axis=0, keepdims=True)
        m_st[h] = jnp.broadcast_to(m_new, m_st.shape[1:])

        hn = hh * lax.rsqrt(jnp.mean(hh * hh, axis=-1, keepdims=True) + NORM_EPS)
        hn = hn * mhg_ref[:, h * dv:(h + 1) * dv]
        og = jax.nn.sigmoid(om_ref[:, h * dv:(h + 1) * dv].astype(F32))
        o_ref[:, h * dv:(h + 1) * dv] = (og * hn).astype(o_ref.dtype)


def mlstm(z, gates, gate_bias, conv_w, conv_b, mh_norm_g, batch, seq):
    t = z.shape[0]
    L = MLSTM_CHUNK
    nc = seq // L
    qw, vw = MLSTM_QK_WIDTH, MLSTM_V_WIDTH
    q_off = 3 * ATTN_WIDTH
    rows = lambda blk: (lambda b, c: (b * nc + c, blk))
    const = lambda b, c: (0, 0)
    return pl.pallas_call(
        _mlstm_kernel,
        grid=(batch, nc),
        in_specs=[pl.BlockSpec((L, qw), rows(q_off // qw)),
                  pl.BlockSpec((L, qw), rows(q_off // qw + 1)),
                  pl.BlockSpec((L, vw), rows((q_off + 2 * qw) // vw)),
                  pl.BlockSpec((L, vw), rows((q_off + 2 * qw) // vw + 1)),
                  pl.BlockSpec((L, V7X_LANES), rows(0)),
                  pl.BlockSpec((1, V7X_LANES), const),
                  pl.BlockSpec((CONV_WIDTH, 2 * qw), const),
                  pl.BlockSpec((1, 2 * qw), const),
                  pl.BlockSpec((1, vw), const)],
        out_specs=pl.BlockSpec((L, vw), rows(0)),
        out_shape=jax.ShapeDtypeStruct((t, vw), BF16),
        scratch_shapes=[pltpu.VMEM((L + CONV_PAD, 2 * qw), F32),
                        pltpu.VMEM((MLSTM_HEADS, MLSTM_QK_DIM, MLSTM_V_DIM), F32),
                        pltpu.VMEM((MLSTM_HEADS, 1, MLSTM_QK_DIM), F32),
                        pltpu.VMEM((MLSTM_HEADS, 8, V7X_LANES), F32)],
        compiler_params=_cparams(("parallel", "arbitrary")),
        name="mlstm",
    )(z, z, z, z, gates, gate_bias, conv_w, conv_b.reshape(1, 2 * qw), mh_norm_g.reshape(1, vw))


def _outproj_kernel(a1_ref, a2_ref, w_ref, res_ref, o_ref):
    k1 = a1_ref.shape[1]
    acc = jnp.dot(a1_ref[...], w_ref[0:k1, :], preferred_element_type=F32)
    acc = acc + jnp.dot(a2_ref[...], w_ref[k1:, :], preferred_element_type=F32)
    o_ref[...] = res_ref[...] + acc


def out_projection(a1, a2, w, res, *, tm=MM_TM, tn=MM_TN):
    t, k1 = a1.shape
    k2 = a2.shape[1]
    n = w.shape[1]
    tm, tn = min(tm, t), min(tn, n)
    return pl.pallas_call(
        _outproj_kernel,
        grid=(t // tm, n // tn),
        in_specs=[pl.BlockSpec((tm, k1), lambda i, j: (i, 0)),
                  pl.BlockSpec((tm, k2), lambda i, j: (i, 0)),
                  pl.BlockSpec((k1 + k2, tn), lambda i, j: (0, j)),
                  pl.BlockSpec((tm, tn), lambda i, j: (i, j))],
        out_specs=pl.BlockSpec((tm, tn), lambda i, j: (i, j)),
        out_shape=jax.ShapeDtypeStruct((t, n), F32),
        compiler_params=_cparams(("parallel", "arbitrary")),
        name="out_projection",
    )(a1, a2, w, res)


def _pack_bf16_pairs(f):
    half = f.shape[1] // 2
    fb = f.astype(BF16).astype(F32)
    lo = pltpu.bitcast(fb[:, :half], U32) >> 16
    hi = pltpu.bitcast(fb[:, half:], U32) & jnp.uint32(0xFFFF0000)
    return lo | hi


def _unpack_bf16_pairs(w):
    lo = pltpu.bitcast(w << 16, F32).astype(BF16)
    hi = pltpu.bitcast(w & jnp.uint32(0xFFFF0000), F32).astype(BF16)
    return lo, hi


def _router_kernel(h_ref, g_ref, wr_ref, br_ref, fpk_ref, e_ref, gate_ref, rank_ref, cnt_ref, carry):
    i = pl.program_id(0)

    @pl.when(i == 0)
    def _():
        carry[...] = jnp.zeros_like(carry)

    f = _rmsnorm_body(h_ref[...], g_ref[...])
    fpk_ref[...] = _pack_bf16_pairs(f)
    f_hi = f.astype(BF16)
    f_lo = (f - f_hi.astype(F32)).astype(BF16)
    logits = (jnp.dot(f_hi, wr_ref[0], preferred_element_type=F32)
              + (jnp.dot(f_lo, wr_ref[0], preferred_element_type=F32)
                 + jnp.dot(f_hi, wr_ref[1], preferred_element_type=F32))) + br_ref[...]
    lane = lax.broadcasted_iota(I32, logits.shape, 1)
    cur = logits
    vals, idxs = [], []
    for _ in range(TOP_K):
        mx = jnp.max(cur, axis=-1, keepdims=True)
        idx = jnp.min(jnp.where(cur == mx, lane, V7X_LANES), axis=-1, keepdims=True)
        vals.append(mx)
        idxs.append(idx)
        cur = jnp.where(lane == idx, -jnp.inf, cur)
    exps = [jnp.exp(v - vals[0]) for v in vals]
    tot = exps[0] + exps[1] + exps[2] + exps[3]
    e_out = jnp.zeros(logits.shape, I32)
    g_out = jnp.zeros(logits.shape, F32)
    for k in range(TOP_K):
        e_out = jnp.where(lane == k, idxs[k], e_out)
        g_out = jnp.where(lane == k, exps[k] / tot, g_out)
    e_ref[...] = e_out
    gate_ref[...] = g_out

    tm = logits.shape[0]
    onehots = [lane == idxs[k] for k in range(TOP_K)]
    member = jnp.zeros(logits.shape, F32)
    for k in range(TOP_K):
        member = jnp.where(onehots[k], 1.0, member)
    row = lax.broadcasted_iota(I32, (tm, tm), 0)
    col = lax.broadcasted_iota(I32, (tm, tm), 1)
    stril = jnp.where(col < row, 1.0, 0.0).astype(BF16)
    before = carry[0:1, :] + jnp.dot(stril, member.astype(BF16), preferred_element_type=F32)
    r_out = jnp.zeros(logits.shape, I32)
    for k in range(TOP_K):
        rk = jnp.sum(jnp.where(onehots[k], before, 0.0), axis=-1, keepdims=True)
        r_out = jnp.where(lane == k, rk.astype(I32), r_out)
    rank_ref[...] = r_out
    total = carry[0:1, :] + jnp.sum(member, axis=0, keepdims=True)
    carry[...] = jnp.broadcast_to(total, carry.shape)
    cnt_ref[...] = jnp.broadcast_to(total, cnt_ref.shape)


def ffn_norm_router(h, g, w_router, b_router, *, tm=NORM_TM):
    t, d = h.shape
    nt = t // tm
    wr = jnp.zeros((d, V7X_LANES), F32).at[:, :N_EXPERTS].set(w_router)
    wr_hi = wr.astype(BF16)
    wr = jnp.stack([wr_hi, (wr - wr_hi.astype(F32)).astype(BF16)])
    br = jnp.full((1, V7X_LANES), NEG_BIG, F32).at[0, :N_EXPERTS].set(b_router)
    rows = lambda i: (i, 0)
    const = lambda i: (0, 0)
    return pl.pallas_call(
        _router_kernel,
        grid=(nt,),
        in_specs=[pl.BlockSpec((tm, d), rows),
                  pl.BlockSpec((1, d), const),
                  pl.BlockSpec((2, d, V7X_LANES), lambda i: (0, 0, 0)),
                  pl.BlockSpec((1, V7X_LANES), const)],
        out_specs=[pl.BlockSpec((tm, d // 2), rows),
                   pl.BlockSpec((tm, V7X_LANES), rows),
                   pl.BlockSpec((tm, V7X_LANES), rows),
                   pl.BlockSpec((tm, V7X_LANES), rows),
                   pl.BlockSpec((8, V7X_LANES), const)],
        out_shape=[jax.ShapeDtypeStruct((t, d // 2), U32),
                   jax.ShapeDtypeStruct((t, V7X_LANES), I32),
                   jax.ShapeDtypeStruct((t, V7X_LANES), F32),
                   jax.ShapeDtypeStruct((t, V7X_LANES), I32),
                   jax.ShapeDtypeStruct((8, V7X_LANES), F32)],
        scratch_shapes=[pltpu.VMEM((8, V7X_LANES), F32)],
        compiler_params=_cparams(("arbitrary",)),
        name="ffn_norm_router",
    )(h, g.reshape(1, d), wr, br)


def routing_tables(top_e, rank, counts, n_tokens):
    tk = n_tokens * TOP_K
    tm = EXPERT_TM
    n_blk = -(-(tk + N_EXPERTS * (tm - 1)) // tm)
    padded = (counts + tm - 1) // tm * tm
    pend = jnp.cumsum(padded)
    pstart = pend - padded
    pos = (pstart[top_e] + rank).reshape(tk).astype(I32)
    blk_start = jnp.arange(n_blk, dtype=I32) * tm
    blk_e = jnp.minimum(jnp.sum(blk_start[:, None] >= pend[None, :], axis=1), N_EXPERTS - 1).astype(I32)
    n_used = (pend[-1] // tm).astype(I32).reshape(1)
    is_last = jnp.any((blk_start[:, None] + tm == pend[None, :]) & (padded[None, :] > 0), axis=1)
    zero_blk = (is_last | (blk_start >= pend[-1])).astype(I32)
    return pos, zero_blk, (blk_e, n_used[0], pstart // tm, padded // tm, n_blk, counts)


SCHED_ACTIVE = 1
SCHED_FIRST = 2
SCHED_PREFETCH = 4
SCHED_HALF = 8


def resident_schedule(layout, n_col):
    blk_e, n_used, first_blk, n_blks, n_blk, counts = layout
    s = jnp.arange(n_blk * n_col, dtype=I32)
    n_active = n_col * n_used
    active = s < n_active
    sc = jnp.minimum(s, jnp.maximum(n_active - 1, 0))
    e = blk_e[jnp.minimum(sc // n_col, n_blk - 1)]
    b0 = first_blk[e]
    nb = jnp.maximum(n_blks[e], 1)
    local = sc - n_col * b0
    col = local // nb
    j = local % nb
    blk = b0 + j
    out_blk = jnp.where(active, blk, s // n_col)
    out_col = jnp.where(active, col, s % n_col)
    nxt = sc - j + nb
    has_next = nxt < n_active
    nxt = jnp.minimum(nxt, jnp.maximum(n_active - 1, 0))
    half_empty = counts[e] - j * EXPERT_TM <= EXPERT_TM // 2
    flags = jnp.where(active, SCHED_ACTIVE + SCHED_HALF * half_empty
                      + (j == 0) * (SCHED_FIRST + SCHED_PREFETCH * has_next), 0)
    as_i32 = lambda a: a.astype(I32)
    return tuple(map(as_i32, (blk, out_blk, out_col, e, col, flags, e[nxt], col[nxt])))


def _dispatch_kernel(pos_ref, zero_ref, f_ref, xs_hbm, zbuf, sem, zsem, *, rows, n_blk):
    i = pl.program_id(0)
    tm = f_ref.shape[0]

    def zero_copy(b):
        return pltpu.make_async_copy(zbuf, xs_hbm.at[pl.ds(b * rows, rows)], zsem)

    @pl.when(i == 0)
    def _():
        zbuf[...] = jnp.zeros_like(zbuf)

        def issue(b, c):
            @pl.when(zero_ref[b] != 0)
            def _():
                zero_copy(b).start()
            return c

        def drain(b, c):
            @pl.when(zero_ref[b] != 0)
            def _():
                zero_copy(b).wait()
            return c

        lax.fori_loop(0, n_blk, issue, 0)
        lax.fori_loop(0, n_blk, drain, 0)

    def row_copy(t, k):
        row = pos_ref[(i * tm + t) * TOP_K + k]
        return pltpu.make_async_copy(f_ref.at[pl.ds(t, 1)], xs_hbm.at[pl.ds(row, 1)], sem)

    def drain_rows(t, c):
        for k in range(TOP_K):
            row_copy(t, k).wait()
        return c

    for t in range(tm):
        for k in range(TOP_K):
            row_copy(t, k).start(priority=k % 2)
    lax.fori_loop(0, tm, drain_rows, 0, unroll=DMA_ISSUE_UNROLL)


def dispatch(fpk, pos, zero_blk, *, rows=EXPERT_TM, tm=NORM_TM):
    t, half = fpk.shape
    n_blk = zero_blk.shape[0]
    grid_spec = pltpu.PrefetchScalarGridSpec(
        num_scalar_prefetch=2,
        grid=(t // tm,),
        in_specs=[pl.BlockSpec((tm, half), lambda i, p, z: (i, 0))],
        out_specs=pl.BlockSpec(memory_space=pl.ANY),
        scratch_shapes=[pltpu.VMEM((rows, half), fpk.dtype), pltpu.SemaphoreType.DMA(()),
                        pltpu.SemaphoreType.DMA(())],
    )
    return pl.pallas_call(
        functools.partial(_dispatch_kernel, rows=rows, n_blk=n_blk),
        grid_spec=grid_spec,
        out_shape=jax.ShapeDtypeStruct((n_blk * rows, half), fpk.dtype),
        compiler_params=_cparams(("arbitrary",), has_side_effects=True),
        name="dispatch",
    )(pos, zero_blk, fpk)


def _weight_tile_refresh(s, flag, tile_copies, cast):
    @pl.when((flag & SCHED_FIRST) != 0)
    def _():
        @pl.when(s == 0)
        def _():
            for cp in tile_copies(False):
                cp.start()

        for cp in tile_copies(False):
            cp.wait()
        cast()

        @pl.when((flag & SCHED_PREFETCH) != 0)
        def _():
            for cp in tile_copies(True):
                cp.start(priority=1)


def _expert_up_kernel(xb_ref, ob_ref, oc_ref, se_ref, sc_ref, flag_ref, ne_ref, nc_ref, xs_ref, w_hbm, bg_ref,
                      bu_ref, act_ref, wstage, wgb, wub, sem, *, tf, nf):
    s = pl.program_id(0)
    half = xs_ref.shape[1]
    flag = flag_ref[s]

    def tile_copies(following):
        e = (ne_ref if following else se_ref)[s]
        c = (nc_ref if following else sc_ref)[s]
        gate_cols = pl.ds(pl.multiple_of(c * tf, tf), tf)
        up_cols = pl.ds(pl.multiple_of((nf + c) * tf, tf), tf)
        return (pltpu.make_async_copy(w_hbm.at[e, :, gate_cols], wstage.at[0], sem.at[0]),
                pltpu.make_async_copy(w_hbm.at[e, :, up_cols], wstage.at[1], sem.at[1]))

    def cast():
        wgb[...] = wstage[0].astype(BF16)
        wub[...] = wstage[1].astype(BF16)

    _weight_tile_refresh(s, flag, tile_copies, cast)

    @pl.when((flag & SCHED_ACTIVE) != 0)
    def _():
        def rows_compute(rows):
            lo, hi = _unpack_bf16_pairs(xs_ref[0:rows, :])
            glu = (jnp.dot(lo, wgb[0:half, :], preferred_element_type=F32)
                   + jnp.dot(hi, wgb[half:, :], preferred_element_type=F32) + bg_ref[...])
            up = (jnp.dot(lo, wub[0:half, :], preferred_element_type=F32)
                  + jnp.dot(hi, wub[half:, :], preferred_element_type=F32) + bu_ref[...])
            glu = jnp.minimum(glu, SWIGLU_LIMIT)
            up = jnp.clip(up, -SWIGLU_LIMIT, SWIGLU_LIMIT)
            act = (up + 1.0) * (glu * jax.nn.sigmoid(SWIGLU_ALPHA * glu))
            act_ref[0:rows, :] = act.astype(act_ref.dtype)

        tm = xs_ref.shape[0]

        @pl.when((flag & SCHED_HALF) == 0)
        def _():
            rows_compute(tm)

        @pl.when((flag & SCHED_HALF) != 0)
        def _():
            rows_compute(tm // 2)
            act_ref[tm // 2:, :] = jnp.zeros((tm - tm // 2, act_ref.shape[1]), act_ref.dtype)

    @pl.when((flag & SCHED_ACTIVE) == 0)
    def _():
        act_ref[...] = jnp.zeros_like(act_ref)


def expert_up(xs, w_gate_up, b_gate_up, sched, *, tm=EXPERT_TM, tf=EXPERT_TF):
    n_rows, half = xs.shape
    d = 2 * half
    ff = EXPERT_FF
    nf = ff // tf
    n_steps = sched[0].shape[0]
    grid_spec = pltpu.PrefetchScalarGridSpec(
        num_scalar_prefetch=len(sched),
        grid=(n_steps,),
        in_specs=[pl.BlockSpec((tm, half), lambda s, *t: (t[0][s], 0)),
                  pl.BlockSpec(memory_space=pl.ANY),
                  pl.BlockSpec((None, 1, tf), lambda s, *t: (t[3][s], 0, t[4][s])),
                  pl.BlockSpec((None, 1, tf), lambda s, *t: (t[3][s], 0, nf + t[4][s]))],
        out_specs=pl.BlockSpec((tm, tf), lambda s, *t: (t[1][s], t[2][s])),
        scratch_shapes=[pltpu.VMEM((2, d, tf), F32), pltpu.VMEM((d, tf), BF16), pltpu.VMEM((d, tf), BF16),
                        pltpu.SemaphoreType.DMA((2,))],
    )
    return pl.pallas_call(
        functools.partial(_expert_up_kernel, tf=tf, nf=nf),
        grid_spec=grid_spec,
        out_shape=jax.ShapeDtypeStruct((n_rows, ff), BF16),
        compiler_params=_cparams(("arbitrary",)),
        name="expert_up",
    )(*sched, xs, w_gate_up, b_gate_up, b_gate_up)


def _expert_down_kernel(xb_ref, ob_ref, oc_ref, se_ref, sc_ref, flag_ref, ne_ref, nc_ref, act_ref, w_hbm, bd_ref,
                        y_ref, wstage, wdb, sem, *, tn):
    s = pl.program_id(0)
    flag = flag_ref[s]

    def tile_copies(following):
        e = (ne_ref if following else se_ref)[s]
        c = (nc_ref if following else sc_ref)[s]
        cols = pl.ds(pl.multiple_of(c * tn, tn), tn)
        return (pltpu.make_async_copy(w_hbm.at[e, :, cols], wstage, sem),)

    def cast():
        wdb[...] = wstage[...].astype(BF16)

    _weight_tile_refresh(s, flag, tile_copies, cast)

    @pl.when((flag & SCHED_ACTIVE) != 0)
    def _():
        def rows_compute(rows):
            y = jnp.dot(act_ref[0:rows, :], wdb[...], preferred_element_type=F32) + bd_ref[...]
            y_ref[0:rows, :] = _pack_bf16_pairs(y)

        tm = act_ref.shape[0]

        @pl.when((flag & SCHED_HALF) == 0)
        def _():
            rows_compute(tm)

        @pl.when((flag & SCHED_HALF) != 0)
        def _():
            rows_compute(tm // 2)
            y_ref[tm // 2:, :] = jnp.zeros((tm - tm // 2, y_ref.shape[1]), y_ref.dtype)

    @pl.when((flag & SCHED_ACTIVE) == 0)
    def _():
        y_ref[...] = jnp.zeros_like(y_ref)


def expert_down(act, w_down, b_down, sched, *, tm=EXPERT_TM, tn=EXPERT_TN):
    n_rows, ff = act.shape
    d = w_down.shape[2]
    n_steps = sched[0].shape[0]
    grid_spec = pltpu.PrefetchScalarGridSpec(
        num_scalar_prefetch=len(sched),
        grid=(n_steps,),
        in_specs=[pl.BlockSpec((tm, ff), lambda s, *t: (t[0][s], 0)),
                  pl.BlockSpec(memory_space=pl.ANY),
                  pl.BlockSpec((None, 1, tn), lambda s, *t: (t[3][s], 0, t[4][s]))],
        out_specs=pl.BlockSpec((tm, tn // 2), lambda s, *t: (t[1][s], t[2][s])),
        scratch_shapes=[pltpu.VMEM((ff, tn), F32), pltpu.VMEM((ff, tn), BF16), pltpu.SemaphoreType.DMA(())],
    )
    return pl.pallas_call(
        functools.partial(_expert_down_kernel, tn=tn),
        grid_spec=grid_spec,
        out_shape=jax.ShapeDtypeStruct((n_rows, d // 2), U32),
        compiler_params=_cparams(("arbitrary",)),
        name="expert_down",
    )(*sched, act, w_down, b_down)


def _combine_ple_kernel(pos_ref, y_hbm, h_ref, gate_ref, gn_ref, w_ref, p_ref, wp_ref, gf_ref, o_ref, gbuf, a2, sem,
                        *, tm, tn, pair, final):
    i = pl.program_id(0)
    j = pl.program_id(1)
    d = o_ref.shape[1]

    def row_copy(blk, t, k):
        row = pos_ref[(blk * tm + t) * TOP_K + k]
        return pltpu.make_async_copy(y_hbm.at[pl.ds(row, 1)], gbuf.at[k, pl.ds(t, 1)], sem)

    def issue(blk, t0, count):
        def body(t, carry):
            for k in range(TOP_K):
                row_copy(blk, t0 + t, k).start(priority=1)
            return carry

        lax.fori_loop(0, count, body, 0, unroll=DMA_ISSUE_UNROLL)

    def drain(t, carry):
        for k in range(TOP_K):
            row_copy(i, t, k).wait()
        return carry

    per_step = tm // (d // tn)

    @pl.when(j == 0)
    def _():
        @pl.when(i == 0)
        def _():
            issue(0, 0, tm)

        lax.fori_loop(0, tm, drain, 0, unroll=DMA_ISSUE_UNROLL)
        gates = gate_ref[...]
        hi_mask = jnp.uint32(0xFFFF0000)
        half = pair // 2
        ssq = jnp.zeros((tm, 1), F32)
        for c in range(d // pair):
            for part, unpack in enumerate((lambda w: w << 16, lambda w: w & hi_mask)):
                terms = [gates[:, k:k + 1] * pltpu.bitcast(unpack(gbuf[k, :, c * half:(c + 1) * half]), F32)
                         for k in range(TOP_K)]
                cols = slice(c * pair + part * half, c * pair + (part + 1) * half)
                val = h_ref[:, cols] + ((terms[0] + terms[1]) + (terms[2] + terms[3]))
                o_ref[:, cols] = val
                ssq = ssq + jnp.sum(val * val, axis=-1, keepdims=True)
        r = lax.rsqrt(ssq / d + NORM_EPS)
        for c0 in range(0, d, half):
            a2[:, c0:c0 + half] = (o_ref[:, c0:c0 + half] * r * gn_ref[:, c0:c0 + half]).astype(a2.dtype)

    last_i = pl.num_programs(0) - 1
    nxt = jnp.minimum(i + 1, last_i)
    for t in range(per_step):
        for k in range(TOP_K):
            row_copy(nxt, j * per_step + t, k).start(priority=k % 2)

    gate = jax.nn.sigmoid(jnp.dot(a2[...], w_ref[...], preferred_element_type=F32))
    emb = jnp.dot(p_ref[...], wp_ref[...], preferred_element_type=F32)
    cols = pl.ds(pl.multiple_of(j * tn, tn), tn)
    o_ref[:, cols] = o_ref[:, cols] + gate * emb

    @pl.when((i == last_i) & (j == pl.num_programs(1) - 1))
    def _():
        lax.fori_loop(0, tm, drain, 0, unroll=DMA_ISSUE_UNROLL)

    if final:
        @pl.when(j == pl.num_programs(1) - 1)
        def _():
            o_ref[...] = _rmsnorm_body(o_ref[...], gf_ref[...])


def combine_ple(y, pos, h, gates, g_ple, w_gate, p, w_proj, g_final, *, pair, tm=COMBINE_PLE_TM, tn=MM_TN):
    t, d = h.shape
    pk = p.shape[1]
    tm, tn = min(tm, t), min(tn, d)
    final = g_final is not None
    g_last = (g_final if final else jnp.ones((d,), F32)).reshape(1, d)
    rows = lambda i, j, pos: (i, 0)
    const = lambda i, j, pos: (0, 0)
    colt = lambda i, j, pos: (0, j)
    grid_spec = pltpu.PrefetchScalarGridSpec(
        num_scalar_prefetch=1,
        grid=(t // tm, d // tn),
        in_specs=[pl.BlockSpec(memory_space=pl.ANY),
                  pl.BlockSpec((tm, d), rows),
                  pl.BlockSpec((tm, V7X_LANES), rows),
                  pl.BlockSpec((1, d), const),
                  pl.BlockSpec((d, tn), colt),
                  pl.BlockSpec((tm, pk), rows),
                  pl.BlockSpec((pk, tn), colt),
                  pl.BlockSpec((1, d), const)],
        out_specs=pl.BlockSpec((tm, d), rows),
        scratch_shapes=[pltpu.VMEM((TOP_K, tm, d // 2), U32), pltpu.VMEM((tm, d), BF16), pltpu.SemaphoreType.DMA(())],
    )
    return pl.pallas_call(
        functools.partial(_combine_ple_kernel, tm=tm, tn=tn, pair=pair, final=final),
        grid_spec=grid_spec,
        out_shape=jax.ShapeDtypeStruct((t, d), F32),
        compiler_params=_cparams(("arbitrary", "arbitrary")),
        name="combine_ple",
    )(pos, y, h, gates, g_ple.reshape(1, d), w_gate, p, w_proj, g_last)


def kernel(x, p, positions, attn_norm_g, w_in, conv_w, conv_b, b_igate, b_fgate, mh_norm_g, w_out,
           ffn_norm_g, w_router, b_router, w_gate_up, b_gate_up, w_down, b_down, ple_norm_g, w_ple_gate,
           w_ple_proj, final_norm_g):
    batch, seq, d = x.shape
    depth = p.shape[0]
    t = batch * seq
    n_main = 3 * ATTN_WIDTH + 2 * MLSTM_QK_WIDTH + 2 * MLSTM_V_WIDTH
    nh = MLSTM_HEADS
    h = x.reshape(t, d)
    cos, sin = rope_tables(positions.reshape(t, 1))
    for i in range(depth):
        a = rmsnorm(h, attn_norm_g[i], BF16)
        w_gates = jnp.zeros((d, V7X_LANES), BF16).at[:, :2 * nh].set(w_in[i][:, n_main:].astype(BF16))
        z, gates = in_projection(a, w_in[i].astype(BF16), n_main, w_gates, cos, sin)
        attn = dilated_attention(z, batch, seq)
        gate_bias = jnp.zeros((1, V7X_LANES), F32).at[0, :nh].set(b_igate[i]).at[0, nh:2 * nh].set(b_fgate[i])
        ml = mlstm(z, gates, gate_bias, conv_w[i], conv_b[i], mh_norm_g[i], batch, seq)
        h = out_projection(attn, ml, w_out[i].astype(BF16), h)
        fpk, top_e, top_g, rank, counts = ffn_norm_router(h, ffn_norm_g[i], w_router[i], b_router[i])
        pos, zero_blk, layout = routing_tables(top_e[:, :TOP_K], rank[:, :TOP_K],
                                               counts[0, :N_EXPERTS].astype(I32), t)
        xs = dispatch(fpk, pos, zero_blk)
        act = expert_up(xs, w_gate_up[i], b_gate_up[i].reshape(N_EXPERTS, 1, 2 * EXPERT_FF),
                        resident_schedule(layout, EXPERT_FF // EXPERT_TF))
        tn = min(EXPERT_TN, d)
        y = expert_down(act, w_down[i], b_down[i].reshape(N_EXPERTS, 1, d),
                        resident_schedule(layout, d // tn), tn=tn)
        h = combine_ple(y, pos, h, top_g, ple_norm_g[i], w_ple_gate[i].astype(BF16),
                        p[i].reshape(t, -1).astype(BF16), w_ple_proj[i].astype(BF16),
                        final_norm_g if i == depth - 1 else None, pair=tn)
    return h.astype(x.dtype).reshape(batch, seq, d)
```
